```python
import math
import jax, jax.numpy as jnp
from jax import lax
import numpy as np

D_MODEL = 2048
BATCH = 2
SEQ = 8192
DEPTH = 1

CHUNK = 64
Q_BLOCK = 128
HEAD_DIM = 128
N_SB_HEADS = 8
N_DIFF_HEADS = 4
SB_WIDTH = N_SB_HEADS * HEAD_DIM
DIFF_WIDTH = N_DIFF_HEADS * 2 * HEAD_DIM
D_MIX = SB_WIDTH + DIFF_WIDTH
IN_COLS = 3 * SB_WIDTH + 3 * DIFF_WIDTH
ROPE_THETA = 10000.0
N_GROUPS = 4
EXPERTS_PER_GROUP = 8
N_EXPERTS = N_GROUPS * EXPERTS_PER_GROUP
D_EXPERT = 512
TOP_K_INNER = 2
NORM_EPS = 1e-6
NEG_INF = -1e30

kernel_name = "hybrid_stickbreak_diffattn_hmoe"


def rms_norm(x, gain):
    xf = x.astype(jnp.float32)
    y = xf * lax.rsqrt(jnp.mean(xf * xf, axis=-1, keepdims=True) + NORM_EPS)
    return (y * gain.astype(jnp.float32)).astype(x.dtype)


def lambda_init(layer_idx):
    return 0.8 - 0.6 * math.exp(-0.3 * layer_idx)


def rotary(x, pos):
    half = x.shape[-1] // 2
    inv_freq = 1.0 / (ROPE_THETA ** (jnp.arange(half, dtype=jnp.float32) / half))
    ang = pos.astype(jnp.float32)[:, None] * inv_freq[None, :]
    cos = jnp.cos(ang)[None, :, None, :]
    sin = jnp.sin(ang)[None, :, None, :]
    xf = x.astype(jnp.float32)
    x1, x2 = xf[..., :half], xf[..., half:]
    out = jnp.concatenate([x1 * cos - x2 * sin, x2 * cos + x1 * sin], axis=-1)
    return out.astype(x.dtype)


def sweep_query_blocks(block_fn, seq):
    n_blocks = seq // Q_BLOCK
    out = lax.map(block_fn, jnp.arange(n_blocks))
    nb, b, qb, h, e = out.shape
    return out.transpose(1, 0, 2, 3, 4).reshape(b, nb * qb, h, e)


def stick_breaking_attention(q, k, v):
    seq, d = q.shape[1], q.shape[-1]
    scale = 1.0 / math.sqrt(d)
    kpos = jnp.arange(seq)

    def block(i):
        start = i * Q_BLOCK
        qb = lax.dynamic_slice_in_dim(q, start, Q_BLOCK, axis=1)
        z = jnp.einsum('bqhd,bkhd->bhqk', qb, k).astype(jnp.float32) * scale
        qpos = start + jnp.arange(Q_BLOCK)
        strict = kpos[None, :] < qpos[:, None]
        log_beta = jax.nn.log_sigmoid(z)
        log_keep = jnp.where(strict, log_beta - z, 0.0)
        later = lax.cumsum(log_keep, axis=3, reverse=True) - log_keep
        a = jnp.where(strict, jnp.exp(log_beta + later), 0.0)
        return jnp.einsum('bhqk,bkhd->bqhd', a.astype(v.dtype), v)

    return sweep_query_blocks(block, seq)


def differential_attention(q1, q2, k1, k2, v, lam):
    seq, d = q1.shape[1], q1.shape[-1]
    scale = 1.0 / math.sqrt(d)
    kchunk = jnp.arange(seq) // CHUNK

    def block(i):
        start = i * Q_BLOCK
        qb1 = lax.dynamic_slice_in_dim(q1, start, Q_BLOCK, axis=1)
        qb2 = lax.dynamic_slice_in_dim(q2, start, Q_BLOCK, axis=1)
        qchunk = (start + jnp.arange(Q_BLOCK)) // CHUNK
        mask = kchunk[None, :] <= qchunk[:, None]
        s1 = jnp.einsum('bqhd,bkhd->bhqk', qb1, k1).astype(jnp.float32) * scale
        s2 = jnp.einsum('bqhd,bkhd->bhqk', qb2, k2).astype(jnp.float32) * scale
        p1 = jax.nn.softmax(jnp.where(mask, s1, NEG_INF), axis=-1)
        p2 = jax.nn.softmax(jnp.where(mask, s2, NEG_INF), axis=-1)
        a = p1 - lam * p2
        return jnp.einsum('bhqk,bkhe->bqhe', a.astype(v.dtype), v)

    return sweep_query_blocks(block, seq)


def hierarchical_moe(h, w_group_router, b_group_router, w_expert_router,
                     b_expert_router, w_gate, w_up, w_down):
    b, s, d = h.shape
    t = h.reshape(b * s, d)
    g_prob = jax.nn.softmax((t @ w_group_router).astype(jnp.float32)
                            + b_group_router.astype(jnp.float32), axis=-1)
    g_val, g_idx = lax.top_k(g_prob, 1)
    e_all = (t @ w_expert_router).astype(jnp.float32) + b_expert_router.astype(jnp.float32)
    e_all = e_all.reshape(-1, N_GROUPS, EXPERTS_PER_GROUP)
    g_onehot = jax.nn.one_hot(g_idx[:, 0], N_GROUPS, dtype=jnp.float32)
    e_logits = jnp.sum(e_all * g_onehot[:, :, None], axis=1)
    e_val, e_idx = lax.top_k(jax.nn.softmax(e_logits, axis=-1), TOP_K_INNER)
    e_val = e_val / jnp.sum(e_val, axis=-1, keepdims=True)
    weights = g_val * e_val
    global_idx = g_idx * EXPERTS_PER_GROUP + e_idx
    combine = jnp.sum(jax.nn.one_hot(global_idx, N_EXPERTS, dtype=jnp.float32)
                      * weights[..., None], axis=1)

    def expert_step(acc, xs):
        wg, wu, wd, c = xs
        hid = jax.nn.silu(t @ wg) * (t @ wu)
        return acc + c[:, None] * (hid @ wd).astype(jnp.float32), None

    acc, _ = lax.scan(expert_step, jnp.zeros((b * s, d), jnp.float32),
                      (w_gate, w_up, w_down, combine.T))
    return acc.astype(h.dtype).reshape(b, s, d)


def setup_inputs(seed: int = 0) -> dict:
    key = jax.random.key(seed)
    ks = jax.random.split(key, 20)
    f32 = jnp.float32
    nrm = lambda k, shape, sc: (jax.random.normal(k, shape, f32) * sc)
    return {
        "x": nrm(ks[0], (BATCH, SEQ, D_MODEL), 1.0),
        "attn_norm_gain": 1.0 + nrm(ks[1], (DEPTH, D_MODEL), 0.02),
        "w_in": nrm(ks[2], (DEPTH, D_MODEL, IN_COLS), D_MODEL ** -0.5),
        "sb_norm_gain": 1.0 + nrm(ks[3], (DEPTH, HEAD_DIM), 0.02),
        "diff_lambda_q1": nrm(ks[4], (DEPTH, HEAD_DIM), 0.1),
        "diff_lambda_k1": nrm(ks[5], (DEPTH, HEAD_DIM), 0.1),
        "diff_lambda_q2": nrm(ks[6], (DEPTH, HEAD_DIM), 0.1),
        "diff_lambda_k2": nrm(ks[7], (DEPTH, HEAD_DIM), 0.1),
        "diff_subln_gain": 1.0 + nrm(ks[8], (DEPTH, 2 * HEAD_DIM), 0.02),
        "w_out": nrm(ks[9], (DEPTH, D_MIX, D_MODEL), D_MIX ** -0.5),
        "ffn_norm_gain": 1.0 + nrm(ks[10], (DEPTH, D_MODEL), 0.02),
        "w_group_router": nrm(ks[11], (DEPTH, D_MODEL, N_GROUPS), D_MODEL ** -0.5),
        "b_group_router": nrm(ks[12], (DEPTH, N_GROUPS), 0.01),
        "w_expert_router": nrm(ks[13], (DEPTH, D_MODEL, N_EXPERTS), D_MODEL ** -0.5),
        "b_expert_router": nrm(ks[14], (DEPTH, N_EXPERTS), 0.01),
        "w_gate": nrm(ks[15], (DEPTH, N_EXPERTS, D_MODEL, D_EXPERT), D_MODEL ** -0.5),
        "w_up": nrm(ks[16], (DEPTH, N_EXPERTS, D_MODEL, D_EXPERT), D_MODEL ** -0.5),
        "w_down": nrm(ks[17], (DEPTH, N_EXPERTS, D_EXPERT, D_MODEL), D_EXPERT ** -0.5),
        "final_norm_gain": 1.0 + nrm(ks[18], (D_MODEL,), 0.02),
    }


def reference(x, attn_norm_gain, w_in, sb_norm_gain, diff_lambda_q1, diff_lambda_k1,
              diff_lambda_q2, diff_lambda_k2, diff_subln_gain, w_out, ffn_norm_gain,
              w_group_router, b_group_router, w_expert_router, b_expert_router,
              w_gate, w_up, w_down, final_norm_gain):
    b, s, _ = x.shape
    pos = jnp.arange(s)
    for layer in range(DEPTH):
        h = rms_norm(x, attn_norm_gain[layer])
        proj = h @ w_in[layer]
        o = 0
        sb_q = proj[..., o:o + SB_WIDTH].reshape(b, s, N_SB_HEADS, HEAD_DIM); o += SB_WIDTH
        sb_k = proj[..., o:o + SB_WIDTH].reshape(b, s, N_SB_HEADS, HEAD_DIM); o += SB_WIDTH
        sb_v = proj[..., o:o + SB_WIDTH].reshape(b, s, N_SB_HEADS, HEAD_DIM); o += SB_WIDTH
        dq = proj[..., o:o + DIFF_WIDTH].reshape(b, s, N_DIFF_HEADS, 2, HEAD_DIM); o += DIFF_WIDTH
        dk = proj[..., o:o + DIFF_WIDTH].reshape(b, s, N_DIFF_HEADS, 2, HEAD_DIM); o += DIFF_WIDTH
        dv = proj[..., o:o + DIFF_WIDTH].reshape(b, s, N_DIFF_HEADS, 2 * HEAD_DIM)

        sb_out = stick_breaking_attention(sb_q, sb_k, sb_v)
        sb_out = rms_norm(sb_out, sb_norm_gain[layer]).reshape(b, s, SB_WIDTH)

        lam_init = lambda_init(layer)
        lam = (jnp.exp(jnp.sum(diff_lambda_q1[layer].astype(jnp.float32) * diff_lambda_k1[layer].astype(jnp.float32)))
               - jnp.exp(jnp.sum(diff_lambda_q2[layer].astype(jnp.float32) * diff_lambda_k2[layer].astype(jnp.float32)))
               + lam_init)
        q1 = rotary(dq[..., 0, :], pos)
        q2 = rotary(dq[..., 1, :], pos)
        k1 = rotary(dk[..., 0, :], pos)
        k2 = rotary(dk[..., 1, :], pos)
        d_out = differential_attention(q1, q2, k1, k2, dv, lam)
        d_out = (rms_norm(d_out, diff_subln_gain[layer]) * (1.0 - lam_init)).astype(x.dtype)
        d_out = d_out.reshape(b, s, DIFF_WIDTH)

        mixed = jnp.concatenate([sb_out, d_out], axis=-1)
        x = x + mixed @ w_out[layer]

        h = rms_norm(x, ffn_norm_gain[layer])
        x = x + hierarchical_moe(h, w_group_router[layer], b_group_router[layer],
                                 w_expert_router[layer], b_expert_router[layer],
                                 w_gate[layer], w_up[layer], w_down[layer])
    return rms_norm(x, final_norm_gain)
```

```python
import functools
import math

import jax
import jax.numpy as jnp
from jax import lax
from jax.experimental import pallas as pl
from jax.experimental.pallas import tpu as pltpu

F32 = jnp.float32
BF16 = jnp.bfloat16

HEAD_DIM = 128
N_SB_HEADS = 8
N_DIFF_HEADS = 4
SECTION = 1024
CHUNK = 64
ROPE_THETA = 10000.0
N_GROUPS = 4
EXPERTS_PER_GROUP = 8
N_EXPERTS = N_GROUPS * EXPERTS_PER_GROUP
NORM_EPS = 1e-6
NEG_INF = -1e30
LANES = 128
SCALE = 1.0 / math.sqrt(HEAD_DIM)

VMEM_LIMIT = 56 * 1024 * 1024


def _cparams(sem):
    return pltpu.CompilerParams(dimension_semantics=sem, vmem_limit_bytes=VMEM_LIMIT)


def _dot(a, b):
    return jnp.dot(a, b, preferred_element_type=F32)


def _dot_nt(a, b):
    return lax.dot_general(a, b, (((1,), (1,)), ((), ())), preferred_element_type=F32)


def _proj_kernel(x_ref, g_ref, w_ref, cos_ref, sin_ref, o_ref, h_ref, *, tn, rows):
    j = pl.program_id(1)
    tm = x_ref.shape[0]

    @pl.when(j == 0)
    def _():
        for r in range(0, tm, rows):
            x = x_ref[r:r + rows, :]
            ms = jnp.mean(x * x, axis=-1, keepdims=True)
            h_ref[r:r + rows, :] = (x * lax.rsqrt(ms + NORM_EPS) * g_ref[...]).astype(BF16)

    acc = _dot(h_ref[...], w_ref[...])
    sec = j // (SECTION // tn)

    def rotary(scale):
        for c in range(tn // HEAD_DIM):
            a = acc[:, c * HEAD_DIM:(c + 1) * HEAD_DIM]
            rot = a * cos_ref[...] + pltpu.roll(a, HEAD_DIM // 2, 1) * sin_ref[...]
            if scale != 1.0:
                rot = rot * scale
            o_ref[:, c * HEAD_DIM:(c + 1) * HEAD_DIM] = rot.astype(BF16)

    @pl.when(sec == 0)
    def _():
        o_ref[...] = (acc * (-SCALE)).astype(BF16)

    @pl.when((sec == 1) | (sec == 2) | (sec == 5))
    def _():
        o_ref[...] = acc.astype(BF16)

    @pl.when(sec == 3)
    def _():
        rotary(SCALE)

    @pl.when(sec == 4)
    def _():
        rotary(1.0)


def _proj_call(x2d, gain, w_in, cos, sin, seq, *, tm, tn):
    t, d = x2d.shape
    n = w_in.shape[1]
    assert t % tm == 0 and seq % tm == 0 and n % tn == 0 and SECTION % tn == 0
    rows = min(tm, 256)
    nseq = seq // tm
    return pl.pallas_call(
        functools.partial(_proj_kernel, tn=tn, rows=rows),
        grid=(t // tm, n // tn),
        in_specs=[
            pl.BlockSpec((tm, d), lambda i, j: (i, 0)),
            pl.BlockSpec((1, d), lambda i, j: (0, 0)),
            pl.BlockSpec((d, tn), lambda i, j: (0, j)),
            pl.BlockSpec((tm, HEAD_DIM), lambda i, j: (i % nseq, 0)),
            pl.BlockSpec((tm, HEAD_DIM), lambda i, j: (i % nseq, 0)),
        ],
        out_specs=pl.BlockSpec((tm, tn), lambda i, j: (i, j)),
        out_shape=jax.ShapeDtypeStruct((t, n), BF16),
        scratch_shapes=[pltpu.VMEM((tm, d), BF16)],
        compiler_params=_cparams(("arbitrary", "arbitrary")),
        name="proj",
    )(x2d, gain, w_in, cos, sin)


def _sb_kernel(q_ref, k_ref, v_ref, tri_ref, g_ref, o_ref, acc_ref, *, tb):
    qi = pl.program_id(2)
    q = q_ref[0]
    row = lax.broadcasted_iota(jnp.int32, (tb, tb), 0)
    col = lax.broadcasted_iota(jnp.int32, (tb, tb), 1)
    strict = col < row

    def block(kj, carry, masked):
        start = pl.multiple_of(kj * tb, tb)
        k = k_ref[0, pl.ds(start, tb), :]
        v = v_ref[0, pl.ds(start, tb), :]
        n = _dot_nt(q, k)
        lk = jnp.minimum(n, 0.0) - jnp.log(1.0 + jnp.exp(-jnp.abs(n)))
        if masked:
            lk = jnp.where(strict, lk, 0.0)
        cum = _dot(lk.astype(BF16), tri_ref[...])
        a = jnp.exp(cum + carry - n)
        if masked:
            a = jnp.where(strict, a, 0.0)
        acc_ref[...] += _dot(a.astype(BF16), v)
        return carry + jnp.sum(lk, axis=-1, keepdims=True)

    acc_ref[...] = jnp.zeros_like(acc_ref)
    carry = block(qi, jnp.zeros((tb, 1), F32), True)
    lax.fori_loop(0, qi, lambda i, c: block(qi - 1 - i, c, False), carry)

    o = acc_ref[...]
    ms = jnp.mean(o * o, axis=-1, keepdims=True)
    o_ref[0] = (o * lax.rsqrt(ms + NORM_EPS) * g_ref[...]).astype(BF16)


def _sb_call(proj3d, tri, gain, *, tb):
    b, s, _ = proj3d.shape
    assert s % tb == 0
    hq, hk, hv = 0, SECTION // HEAD_DIM, 2 * SECTION // HEAD_DIM
    return pl.pallas_call(
        functools.partial(_sb_kernel, tb=tb),
        grid=(b, N_SB_HEADS, s // tb),
        in_specs=[
            pl.BlockSpec((1, tb, HEAD_DIM), lambda bi, h, i: (bi, i, hq + h)),
            pl.BlockSpec((1, s, HEAD_DIM), lambda bi, h, i: (bi, 0, hk + h)),
            pl.BlockSpec((1, s, HEAD_DIM), lambda bi, h, i: (bi, 0, hv + h)),
            pl.BlockSpec((tb, tb), lambda bi, h, i: (0, 0)),
            pl.BlockSpec((1, HEAD_DIM), lambda bi, h, i: (0, 0)),
        ],
        out_specs=pl.BlockSpec((1, tb, HEAD_DIM), lambda bi, h, i: (bi, i, h)),
        out_shape=jax.ShapeDtypeStruct((b, s, N_SB_HEADS * HEAD_DIM), BF16),
        scratch_shapes=[pltpu.VMEM((tb, HEAD_DIM), F32)],
        compiler_params=_cparams(("arbitrary", "arbitrary", "arbitrary")),
        name="sb_attn",
    )(proj3d, proj3d, proj3d, tri, gain)


def _diff_kernel(q_ref, k_ref, v_ref, lq1_ref, lk1_ref, lq2_ref, lk2_ref, g_ref, o_ref,
                 acc1_ref, acc2_ref, *, tb, lam_init):
    qi = pl.program_id(2)
    d = HEAD_DIM
    q1 = q_ref[0, :, :d]
    q2 = q_ref[0, :, d:]
    row = lax.broadcasted_iota(jnp.int32, (tb, tb), 0)
    col = lax.broadcasted_iota(jnp.int32, (tb, tb), 1)
    visible = (col // CHUNK) <= (row // CHUNK)

    def softmax_step(s, m, l, acc_ref, v):
        m_new = jnp.maximum(m, jnp.max(s, axis=-1, keepdims=True))
        alpha = jnp.exp(m - m_new)
        p = jnp.exp(s - m_new)
        acc_ref[...] = alpha * acc_ref[...] + _dot(p.astype(BF16), v)
        return m_new, alpha * l + jnp.sum(p, axis=-1, keepdims=True)

    def block(kj, carry, masked):
        m1, l1, m2, l2 = carry
        start = pl.multiple_of(kj * tb, tb)
        k = k_ref[0, pl.ds(start, tb), :]
        v = v_ref[0, pl.ds(start, tb), :]
        s1 = _dot_nt(q1, k[:, :d])
        s2 = _dot_nt(q2, k[:, d:])
        if masked:
            s1 = jnp.where(visible, s1, NEG_INF)
            s2 = jnp.where(visible, s2, NEG_INF)
        m1, l1 = softmax_step(s1, m1, l1, acc1_ref, v)
        m2, l2 = softmax_step(s2, m2, l2, acc2_ref, v)
        return m1, l1, m2, l2

    acc1_ref[...] = jnp.zeros_like(acc1_ref)
    acc2_ref[...] = jnp.zeros_like(acc2_ref)
    neg = jnp.full((tb, 1), NEG_INF, F32)
    zero = jnp.zeros((tb, 1), F32)
    carry = block(qi, (neg, zero, neg, zero), True)
    _, l1, _, l2 = lax.fori_loop(0, qi, lambda i, c: block(i, c, False), carry)

    lam = (jnp.exp(jnp.sum(lq1_ref[...] * lk1_ref[...], axis=-1, keepdims=True))
           - jnp.exp(jnp.sum(lq2_ref[...] * lk2_ref[...], axis=-1, keepdims=True)) + lam_init)
    o = acc1_ref[...] / l1 - lam * (acc2_ref[...] / l2)
    ms = jnp.mean(o * o, axis=-1, keepdims=True)
    o_ref[0] = (o * lax.rsqrt(ms + NORM_EPS) * g_ref[...] * (1.0 - lam_init)).astype(BF16)


def _diff_call(proj3d, lq1, lk1, lq2, lk2, gain, lam_init, *, tb):
    b, s, _ = proj3d.shape
    assert s % tb == 0 and tb % CHUNK == 0
    w = 2 * HEAD_DIM
    hq, hk, hv = 3 * SECTION // w, 4 * SECTION // w, 5 * SECTION // w
    vec = pl.BlockSpec((1, HEAD_DIM), lambda bi, h, i: (0, 0))
    return pl.pallas_call(
        functools.partial(_diff_kernel, tb=tb, lam_init=lam_init),
        grid=(b, N_DIFF_HEADS, s // tb),
        in_specs=[
            pl.BlockSpec((1, tb, w), lambda bi, h, i: (bi, i, hq + h)),
            pl.BlockSpec((1, s, w), lambda bi, h, i: (bi, 0, hk + h)),
            pl.BlockSpec((1, s, w), lambda bi, h, i: (bi, 0, hv + h)),
            vec, vec, vec, vec,
            pl.BlockSpec((1, w), lambda bi, h, i: (0, 0)),
        ],
        out_specs=pl.BlockSpec((1, tb, w), lambda bi, h, i: (bi, i, h)),
        out_shape=jax.ShapeDtypeStruct((b, s, N_DIFF_HEADS * w), BF16),
        scratch_shapes=[pltpu.VMEM((tb, w), F32), pltpu.VMEM((tb, w), F32)],
        compiler_params=_cparams(("arbitrary", "arbitrary", "arbitrary")),
        name="diff_attn",
    )(proj3d, proj3d, proj3d, lq1, lk1, lq2, lk2, gain)


def _outproj_kernel(x_ref, sb_ref, df_ref, wo_ref, g_ref, wr_ref, br_ref, tri_ref,
                    x1_ref, h2_ref, ri_ref, rf_ref, cnt_ref):
    i = pl.program_id(0)
    tm = x_ref.shape[0]
    half = sb_ref.shape[1]
    x1 = x_ref[...] + _dot(sb_ref[...], wo_ref[:half, :]) + _dot(df_ref[...], wo_ref[half:, :])
    x1_ref[...] = x1
    ms = jnp.mean(x1 * x1, axis=-1, keepdims=True)
    h2 = x1 * lax.rsqrt(ms + NORM_EPS) * g_ref[...]
    h2_ref[...] = h2

    logits = _dot(h2.astype(BF16), wr_ref[...]) + br_ref[...]
    lane = lax.broadcasted_iota(jnp.int32, (tm, LANES), 1).astype(F32)
    ninf = -jnp.inf

    def first_argmax(vals):
        top = jnp.max(vals, axis=-1, keepdims=True)
        idx = jnp.min(jnp.where(vals == top, lane, float(LANES)), axis=-1, keepdims=True)
        return top, idx

    gl = jnp.where(lane < N_GROUPS, logits, ninf)
    gmax, gidx = first_argmax(gl)
    g_val = 1.0 / jnp.sum(jnp.exp(gl - gmax), axis=-1, keepdims=True)
    lo = N_GROUPS + EXPERTS_PER_GROUP * gidx
    el = jnp.where((lane >= lo) & (lane < lo + EXPERTS_PER_GROUP), logits, ninf)
    l1, i1 = first_argmax(el)
    l2, i2 = first_argmax(jnp.where(lane == i1, ninf, el))
    r = jnp.exp(l2 - l1)
    w1 = g_val / (1.0 + r)
    w2 = g_val * r / (1.0 + r)
    e1 = i1 - N_GROUPS
    e2 = i2 - N_GROUPS

    @pl.when(i == 0)
    def _():
        cnt_ref[...] = jnp.zeros_like(cnt_ref)

    onehot = jnp.where((lane == e1) | (lane == e2), 1.0, 0.0)
    before = _dot(tri_ref[...], onehot.astype(BF16)) + cnt_ref[0:1, :]
    rank1 = jnp.sum(jnp.where(lane == e1, before, 0.0), axis=-1, keepdims=True)
    rank2 = jnp.sum(jnp.where(lane == e2, before, 0.0), axis=-1, keepdims=True)
    cnt_ref[0:1, :] = cnt_ref[0:1, :] + jnp.sum(onehot, axis=0, keepdims=True)

    ri = jnp.where(lane == 0, e1, jnp.where(lane == 1, e2,
                   jnp.where(lane == 2, rank1, jnp.where(lane == 3, rank2, 0.0))))
    ri_ref[...] = ri.astype(jnp.int32)
    rf_ref[...] = jnp.where(lane == 0, w1, jnp.where(lane == 1, w2, 0.0))


def _outproj_call(x2d, sb_out, d_out, w_out, gain, w_router, b_router, tri, *, tm):
    t, d = x2d.shape
    half = sb_out.shape[1]
    assert t % tm == 0
    row = lambda i: (i, 0)
    fixed = lambda i: (0, 0)
    return pl.pallas_call(
        _outproj_kernel,
        grid=(t // tm,),
        in_specs=[
            pl.BlockSpec((tm, d), row),
            pl.BlockSpec((tm, half), row),
            pl.BlockSpec((tm, half), row),
            pl.BlockSpec((2 * half, d), fixed),
            pl.BlockSpec((1, d), fixed),
            pl.BlockSpec((d, LANES), fixed),
            pl.BlockSpec((1, LANES), fixed),
            pl.BlockSpec((tm, tm), fixed),
        ],
        out_specs=[
            pl.BlockSpec((tm, d), row),
            pl.BlockSpec((tm, d), row),
            pl.BlockSpec((tm, LANES), row),
            pl.BlockSpec((tm, LANES), row),
            pl.BlockSpec((8, LANES), fixed),
        ],
        out_shape=[
            jax.ShapeDtypeStruct((t, d), F32),
            jax.ShapeDtypeStruct((t, d), F32),
            jax.ShapeDtypeStruct((t, LANES), jnp.int32),
            jax.ShapeDtypeStruct((t, LANES), F32),
            jax.ShapeDtypeStruct((8, LANES), F32),
        ],
        compiler_params=_cparams(("arbitrary",)),
        name="outproj_router",
    )(x2d, sb_out, d_out, w_out, gain, w_router, b_router, tri)


def _dispatch_kernel(pos_hbm, h_hbm, xs_hbm, pos_smem, sem_idx, sem, *, td):
    i = pl.program_id(0)
    idx_copy = pltpu.make_async_copy(pos_hbm.at[i], pos_smem, sem_idx)
    idx_copy.start()
    idx_copy.wait()

    def row_copy(t, k):
        return pltpu.make_async_copy(h_hbm.at[pl.ds(i * td + t, 1)],
                                     xs_hbm.at[pl.ds(pos_smem[0, 2 * t + k], 1)], sem)

    def issue(t, c):
        row_copy(t, 0).start()
        row_copy(t, 1).start()
        return c

    lax.fori_loop(0, td, issue, 0)

    def drain(t, c):
        row_copy(t, 0).wait()
        row_copy(t, 1).wait()
        return c

    lax.fori_loop(0, td, drain, 0)


def _dispatch_call(pos, h2, n_rows, *, td):
    t, d = h2.shape
    assert t % td == 0
    pos3 = pos.reshape(t // td, 1, 2 * td)
    return pl.pallas_call(
        functools.partial(_dispatch_kernel, td=td),
        grid=(t // td,),
        in_specs=[pl.BlockSpec(memory_space=pl.ANY), pl.BlockSpec(memory_space=pl.ANY)],
        out_specs=pl.BlockSpec(memory_space=pl.ANY),
        out_shape=jax.ShapeDtypeStruct((n_rows, d), h2.dtype),
        scratch_shapes=[pltpu.SMEM((1, 2 * td), jnp.int32), pltpu.SemaphoreType.DMA,
                        pltpu.SemaphoreType.DMA],
        compiler_params=_cparams(("arbitrary",)),
        name="dispatch",
    )(pos3, h2)


def _experts_kernel(wt_ref, we_ref, nw_ref, lo_ref, hi_ref, xs_ref, wg_ref, wu_ref, wd_ref, ys_ref):
    w = pl.program_id(0)
    tx = xs_ref.shape[0]
    tile = wt_ref[w]
    first = (w == 0) | (wt_ref[jnp.maximum(w - 1, 0)] != tile)

    @pl.when(w < nw_ref[0])
    def _():
        e = we_ref[w]
        rows = tile * tx + lax.broadcasted_iota(jnp.int32, (tx, 1), 0)
        member = (rows >= lo_ref[e]) & (rows < hi_ref[e])
        x = xs_ref[...].astype(BF16)
        gate = _dot(x, wg_ref[0].astype(BF16))
        up = _dot(x, wu_ref[0].astype(BF16))
        hid = gate * (1.0 / (1.0 + jnp.exp(-gate))) * up
        y = _dot(jnp.where(member, hid, 0.0).astype(BF16), wd_ref[0].astype(BF16))

        @pl.when(first)
        def _():
            ys_ref[...] = y

        @pl.when(jnp.logical_not(first))
        def _():
            ys_ref[...] += y


def _experts_call(work_tile, work_expert, n_work, seg_lo, seg_hi, xs, w_gate, w_up, w_down, *, tx):
    p, d = xs.shape
    de = w_gate.shape[2]
    assert p % tx == 0
    n_items = work_tile.shape[0]
    tile = lambda w, wt, we, nw, lo, hi: (wt[w], 0)
    expert = lambda w, wt, we, nw, lo, hi: (we[w], 0, 0)
    return pl.pallas_call(
        _experts_kernel,
        grid_spec=pltpu.PrefetchScalarGridSpec(
            num_scalar_prefetch=5,
            grid=(n_items,),
            in_specs=[
                pl.BlockSpec((tx, d), tile),
                pl.BlockSpec((1, d, de), expert),
                pl.BlockSpec((1, d, de), expert),
                pl.BlockSpec((1, de, d), expert),
            ],
            out_specs=pl.BlockSpec((tx, d), tile),
        ),
        out_shape=jax.ShapeDtypeStruct((p, d), F32),
        compiler_params=_cparams(("arbitrary",)),
        name="experts",
    )(work_tile, work_expert, n_work, seg_lo, seg_hi, xs, w_gate, w_up, w_down)


def _combine_kernel(pos_hbm, x1_ref, rf_ref, g_ref, ys_hbm, o_ref, pos_smem, y0_ref, y1_ref,
                    sem_idx, sem, *, tc):
    i = pl.program_id(0)
    idx_copy = pltpu.make_async_copy(pos_hbm.at[i], pos_smem, sem_idx)
    idx_copy.start()
    idx_copy.wait()

    def row_copy(t, k):
        dst = y0_ref if k == 0 else y1_ref
        return pltpu.make_async_copy(ys_hbm.at[pl.ds(pos_smem[0, 2 * t + k], 1)],
                                     dst.at[pl.ds(t, 1)], sem)

    def issue(t, c):
        row_copy(t, 0).start()
        row_copy(t, 1).start()
        return c

    lax.fori_loop(0, tc, issue, 0)

    def drain(t, c):
        row_copy(t, 0).wait()
        row_copy(t, 1).wait()
        return c

    lax.fori_loop(0, tc, drain, 0)

    x = x1_ref[...] + rf_ref[:, 0:1] * y0_ref[...] + rf_ref[:, 1:2] * y1_ref[...]
    ms = jnp.mean(x * x, axis=-1, keepdims=True)
    o_ref[...] = x * lax.rsqrt(ms + NORM_EPS) * g_ref[...]


def _combine_call(pos, x1, rf, gain, ys, *, tc):
    t, d = x1.shape
    assert t % tc == 0
    pos3 = pos.reshape(t // tc, 1, 2 * tc)
    row = lambda i: (i, 0)
    return pl.pallas_call(
        functools.partial(_combine_kernel, tc=tc),
        grid=(t // tc,),
        in_specs=[
            pl.BlockSpec(memory_space=pl.ANY),
            pl.BlockSpec((tc, d), row),
            pl.BlockSpec((tc, LANES), row),
            pl.BlockSpec((1, d), lambda i: (0, 0)),
            pl.BlockSpec(memory_space=pl.ANY),
        ],
        out_specs=pl.BlockSpec((tc, d), row),
        out_shape=jax.ShapeDtypeStruct((t, d), F32),
        scratch_shapes=[pltpu.SMEM((1, 2 * tc), jnp.int32), pltpu.VMEM((tc, d), F32),
                        pltpu.VMEM((tc, d), F32), pltpu.SemaphoreType.DMA,
                        pltpu.SemaphoreType.DMA],
        compiler_params=_cparams(("arbitrary",)),
        name="combine",
    )(pos3, x1, rf, gain, ys)


def _pick(n, pref):
    while n % pref:
        pref //= 2
    return pref


def _layer(x2d, b, s, layer, attn_norm_gain, w_in, sb_norm_gain, lq1, lk1, lq2, lk2, subln_gain,
           w_out, ffn_norm_gain, w_gr, b_gr, w_er, b_er, w_gate, w_up, w_down):
    t, d = x2d.shape
    tb = _pick(s, 256)
    tm_proj = _pick(s, 1024)
    tm_out = _pick(t, 256)
    tx = 256

    half = HEAD_DIM // 2
    inv_freq = 1.0 / (ROPE_THETA ** (jnp.arange(half, dtype=F32) / half))
    ang = jnp.arange(s, dtype=F32)[:, None] * inv_freq[None, :]
    cos = jnp.concatenate([jnp.cos(ang), jnp.cos(ang)], axis=-1)
    sin = jnp.concatenate([-jnp.sin(ang), jnp.sin(ang)], axis=-1)

    proj = _proj_call(x2d, attn_norm_gain.reshape(1, d), w_in.astype(BF16), cos, sin, s,
                      tm=tm_proj, tn=512)
    proj3d = proj.reshape(b, s, proj.shape[1])

    r = jnp.arange(tb)
    tri_suffix = (r[:, None] >= r[None, :]).astype(BF16)
    sb_out = _sb_call(proj3d, tri_suffix, sb_norm_gain.reshape(1, HEAD_DIM), tb=tb)

    lam_init = 0.8 - 0.6 * math.exp(-0.3 * layer)
    d_out = _diff_call(proj3d, lq1.reshape(1, -1), lk1.reshape(1, -1), lq2.reshape(1, -1),
                       lk2.reshape(1, -1), subln_gain.reshape(1, -1), lam_init, tb=tb)

    w_router = jnp.zeros((d, LANES), F32)
    w_router = w_router.at[:, :N_GROUPS].set(w_gr).at[:, N_GROUPS:N_GROUPS + N_EXPERTS].set(w_er)
    b_router = jnp.zeros((1, LANES), F32)
    b_router = b_router.at[0, :N_GROUPS].set(b_gr).at[0, N_GROUPS:N_GROUPS + N_EXPERTS].set(b_er)
    rr = jnp.arange(tm_out)
    tri_before = (rr[None, :] < rr[:, None]).astype(BF16)
    x1, h2, ri, rf, cnt = _outproj_call(
        x2d, sb_out.reshape(t, -1), d_out.reshape(t, -1), w_out.astype(BF16),
        ffn_norm_gain.reshape(1, d), w_router.astype(BF16), b_router, tri_before, tm=tm_out)

    counts = cnt[0, :N_EXPERTS].astype(jnp.int32)
    seg_hi = jnp.cumsum(counts)
    seg_lo = seg_hi - counts
    pos = (seg_lo[ri[:, 0:2]] + ri[:, 2:4]).reshape(-1)
    first_tile = seg_lo // tx
    items = jnp.where(counts > 0, (seg_hi - 1) // tx - first_tile + 1, 0)
    item_hi = jnp.cumsum(items)
    n_work = item_hi[-1]
    w = jnp.minimum(jnp.arange(2 * t // tx + N_EXPERTS - 1, dtype=jnp.int32), n_work - 1)
    work_expert = jnp.sum((item_hi[None, :] <= w[:, None]).astype(jnp.int32), axis=1)
    work_tile = (first_tile[work_expert] + w - (item_hi - items)[work_expert]).astype(jnp.int32)

    xs = _dispatch_call(pos, h2, 2 * t, td=tm_out)
    ys = _experts_call(work_tile, work_expert, n_work.reshape(1).astype(jnp.int32), seg_lo, seg_hi,
                       xs, w_gate, w_up, w_down, tx=tx)
    return x1, pos, rf, ys


def kernel(x, attn_norm_gain, w_in, sb_norm_gain, diff_lambda_q1, diff_lambda_k1, diff_lambda_q2,
           diff_lambda_k2, diff_subln_gain, w_out, ffn_norm_gain, w_group_router, b_group_router,
           w_expert_router, b_expert_router, w_gate, w_up, w_down, final_norm_gain):
    b, s, d = x.shape
    assert w_in.shape[0] == 1, "the combine stage fuses the final norm: single-layer stacks only"
    layer = 0
    x2d = x.reshape(b * s, d)
    x1, pos, rf, ys = _layer(
        x2d, b, s, layer, attn_norm_gain[layer], w_in[layer], sb_norm_gain[layer],
        diff_lambda_q1[layer], diff_lambda_k1[layer], diff_lambda_q2[layer],
        diff_lambda_k2[layer], diff_subln_gain[layer], w_out[layer], ffn_norm_gain[layer],
        w_group_router[layer], b_group_router[layer], w_expert_router[layer],
        b_expert_router[layer], w_gate[layer], w_up[layer], w_down[layer])
    out = _combine_call(pos, x1, rf, final_norm_gain.reshape(1, d), ys, tc=_pick(b * s, 256))
    return out.reshape(b, s, d)
```

```python
import functools
import math

import jax
import jax.numpy as jnp
from jax import lax
from jax.experimental import pallas as pl
from jax.experimental.pallas import tpu as pltpu

F32 = jnp.float32
BF16 = jnp.bfloat16

HEAD_DIM = 128
N_SB_HEADS = 8
N_DIFF_HEADS = 4
SECTION = 1024
CHUNK = 64
ROPE_THETA = 10000.0
N_GROUPS = 4
EXPERTS_PER_GROUP = 8
N_EXPERTS = N_GROUPS * EXPERTS_PER_GROUP
NORM_EPS = 1e-6
NEG_INF = -1e30
LANES = 128
ROW_CHUNK = 32
SCALE_LOG2E = math.log2(math.e) / math.sqrt(HEAD_DIM)

VMEM_LIMIT = 56 * 1024 * 1024


def _cparams(sem):
    return pltpu.CompilerParams(dimension_semantics=sem, vmem_limit_bytes=VMEM_LIMIT)


def _dot(a, b):
    return jnp.dot(a, b, preferred_element_type=F32)


def _dot_nt(a, b):
    return lax.dot_general(a, b, (((1,), (1,)), ((), ())), preferred_element_type=F32)


def _proj_kernel(x_ref, g_ref, w_ref, cos_ref, sin_ref, o_ref, h_ref, *, tn, rows):
    j = pl.program_id(1)
    tm = x_ref.shape[0]

    @pl.when(j == 0)
    def _():
        for r in range(0, tm, rows):
            x = x_ref[r:r + rows, :]
            ms = jnp.mean(x * x, axis=-1, keepdims=True)
            h_ref[r:r + rows, :] = (x * lax.rsqrt(ms + NORM_EPS) * g_ref[...]).astype(BF16)

    acc = _dot(h_ref[...], w_ref[...])
    sec = j // (SECTION // tn)

    def rotary(scale):
        for c in range(tn // HEAD_DIM):
            a = acc[:, c * HEAD_DIM:(c + 1) * HEAD_DIM]
            rot = a * cos_ref[...] + pltpu.roll(a, HEAD_DIM // 2, 1) * sin_ref[...]
            if scale != 1.0:
                rot = rot * scale
            o_ref[:, c * HEAD_DIM:(c + 1) * HEAD_DIM] = rot.astype(BF16)

    @pl.when(sec == 0)
    def _():
        o_ref[...] = (acc * (-SCALE_LOG2E)).astype(BF16)

    @pl.when((sec == 1) | (sec == 2) | (sec == 5))
    def _():
        o_ref[...] = acc.astype(BF16)

    @pl.when(sec == 3)
    def _():
        rotary(SCALE_LOG2E)

    @pl.when(sec == 4)
    def _():
        rotary(1.0)


def _proj_call(x2d, gain, w_in, cos, sin, seq, *, tm, tn):
    t, d = x2d.shape
    n = w_in.shape[1]
    assert t % tm == 0 and seq % tm == 0 and n % tn == 0 and SECTION % tn == 0
    rows = min(tm, 256)
    nseq = seq // tm
    return pl.pallas_call(
        functools.partial(_proj_kernel, tn=tn, rows=rows),
        grid=(t // tm, n // tn),
        in_specs=[
            pl.BlockSpec((tm, d), lambda i, j: (i, 0)),
            pl.BlockSpec((1, d), lambda i, j: (0, 0)),
            pl.BlockSpec((d, tn), lambda i, j: (0, j)),
            pl.BlockSpec((tm, HEAD_DIM), lambda i, j: (i % nseq, 0)),
            pl.BlockSpec((tm, HEAD_DIM), lambda i, j: (i % nseq, 0)),
        ],
        out_specs=pl.BlockSpec((tm, tn), lambda i, j: (i, j)),
        out_shape=jax.ShapeDtypeStruct((t, n), BF16),
        scratch_shapes=[pltpu.VMEM((tm, d), BF16)],
        compiler_params=_cparams(("arbitrary", "arbitrary")),
        name="proj",
    )(x2d, gain, w_in, cos, sin)


def _sb_kernel(q_ref, k_ref, v_ref, tri_ref, g_ref, o_ref, acc_ref, *, tb, nsub):
    qi = pl.program_id(2)
    tq = nsub * tb
    q = q_ref[0]
    row = lax.broadcasted_iota(jnp.int32, (tq, tb), 0)
    col = lax.broadcasted_iota(jnp.int32, (tq, tb), 1)

    def keep_logs(kj, mask):
        start = pl.multiple_of(kj * tb, tb)
        n = _dot_nt(q, k_ref[0, pl.ds(start, tb), :])
        lks, sums = [], []
        for r in range(0, tq, ROW_CHUNK):
            nc = n[r:r + ROW_CHUNK]
            neg_abs = pltpu.bitcast(pltpu.bitcast(nc, jnp.uint32) | jnp.uint32(0x80000000), F32)
            lk = jnp.minimum(nc, 0.0) - jnp.log2(1.0 + jnp.exp2(neg_abs))
            if mask is not None:
                lk = jnp.where(mask[r:r + ROW_CHUNK], lk, 0.0)
            lks.append(lk.astype(BF16))
            sums.append(jnp.sum(lk, axis=-1, keepdims=True))
        return n, jnp.concatenate(lks, axis=0), jnp.concatenate(sums, axis=0)

    def accumulate(kj, n, lk, lk_sum, carry, mask):
        start = pl.multiple_of(kj * tb, tb)
        cum = _dot(lk, tri_ref[...])
        parts = []
        for r in range(0, tq, ROW_CHUNK):
            rows = slice(r, r + ROW_CHUNK)
            a = jnp.exp2(cum[rows] + carry[rows] - n[rows])
            if mask is not None:
                a = jnp.where(mask[rows], a, 0.0)
            parts.append(a.astype(BF16))
        acc_ref[...] += _dot(jnp.concatenate(parts, axis=0), v_ref[0, pl.ds(start, tb), :])
        return carry + lk_sum

    def single(kj, carry, mask):
        return accumulate(kj, *keep_logs(kj, mask), carry, mask)

    def pair(kj, carry):
        first = keep_logs(kj, None)
        second = keep_logs(kj - 1, None)
        carry = accumulate(kj, *first, carry, None)
        return accumulate(kj - 1, *second, carry, None)

    acc_ref[...] = jnp.zeros_like(acc_ref)
    base = qi * nsub
    carry = jnp.zeros((tq, 1), F32)
    for sub in reversed(range(nsub)):
        carry = single(base + sub, carry, col + sub * tb < row)
    lax.fori_loop(0, base // 2, lambda i, c: pair(base - 1 - 2 * i, c), carry)

    o = acc_ref[...]
    ms = jnp.mean(o * o, axis=-1, keepdims=True)
    o_ref[0] = (o * lax.rsqrt(ms + NORM_EPS) * g_ref[...]).astype(BF16)


def _sb_call(proj3d, tri, gain, *, tb, nsub):
    b, s, _ = proj3d.shape
    tq = nsub * tb
    assert s % tq == 0 and nsub % 2 == 0 and tq % ROW_CHUNK == 0
    hq, hk, hv = 0, SECTION // HEAD_DIM, 2 * SECTION // HEAD_DIM
    return pl.pallas_call(
        functools.partial(_sb_kernel, tb=tb, nsub=nsub),
        grid=(b, N_SB_HEADS, s // tq),
        in_specs=[
            pl.BlockSpec((1, tq, HEAD_DIM), lambda bi, h, i: (bi, i, hq + h)),
            pl.BlockSpec((1, s, HEAD_DIM), lambda bi, h, i: (bi, 0, hk + h)),
            pl.BlockSpec((1, s, HEAD_DIM), lambda bi, h, i: (bi, 0, hv + h)),
            pl.BlockSpec((tb, tb), lambda bi, h, i: (0, 0)),
            pl.BlockSpec((1, HEAD_DIM), lambda bi, h, i: (0, 0)),
        ],
        out_specs=pl.BlockSpec((1, tq, HEAD_DIM), lambda bi, h, i: (bi, i, h)),
        out_shape=jax.ShapeDtypeStruct((b, s, N_SB_HEADS * HEAD_DIM), BF16),
        scratch_shapes=[pltpu.VMEM((tq, HEAD_DIM), F32)],
        compiler_params=_cparams(("arbitrary", "arbitrary", "arbitrary")),
        name="sb_attn",
    )(proj3d, proj3d, proj3d, tri, gain)


def _diff_kernel(q_ref, k_ref, v_ref, lq1_ref, lk1_ref, lq2_ref, lk2_ref, g_ref, o_ref,
                 acc1_ref, acc2_ref, *, tq, lam_init):
    qi = pl.program_id(2)
    d = HEAD_DIM
    q1 = q_ref[0, :, :d]
    q2 = q_ref[0, :, d:]
    row = lax.broadcasted_iota(jnp.int32, (tq, tq), 0)
    col = lax.broadcasted_iota(jnp.int32, (tq, tq), 1)
    visible = (col // CHUNK) <= (row // CHUNK)

    def softmax_step(s, m, l, acc_ref, v):
        m_new = jnp.maximum(m, jnp.max(s, axis=-1, keepdims=True))
        alpha = jnp.exp2(m - m_new)
        p = jnp.exp2(s - m_new)
        acc_ref[...] = alpha * acc_ref[...] + _dot(p.astype(BF16), v)
        return m_new, alpha * l + jnp.sum(p, axis=-1, keepdims=True)

    def block(kj, carry, masked):
        m1, l1, m2, l2 = carry
        start = pl.multiple_of(kj * tq, tq)
        k = k_ref[0, pl.ds(start, tq), :]
        v = v_ref[0, pl.ds(start, tq), :]
        s1 = _dot_nt(q1, k[:, :d])
        s2 = _dot_nt(q2, k[:, d:])
        if masked:
            s1 = jnp.where(visible, s1, NEG_INF)
            s2 = jnp.where(visible, s2, NEG_INF)
        m1, l1 = softmax_step(s1, m1, l1, acc1_ref, v)
        m2, l2 = softmax_step(s2, m2, l2, acc2_ref, v)
        return m1, l1, m2, l2

    acc1_ref[...] = jnp.zeros_like(acc1_ref)
    acc2_ref[...] = jnp.zeros_like(acc2_ref)
    neg = jnp.full((tq, 1), NEG_INF, F32)
    zero = jnp.zeros((tq, 1), F32)
    carry = block(qi, (neg, zero, neg, zero), True)
    _, l1, _, l2 = lax.fori_loop(0, qi, lambda i, c: block(i, c, False), carry)

    lam = (jnp.exp(jnp.sum(lq1_ref[...] * lk1_ref[...], axis=-1, keepdims=True))
           - jnp.exp(jnp.sum(lq2_ref[...] * lk2_ref[...], axis=-1, keepdims=True)) + lam_init)
    o = acc1_ref[...] / l1 - lam * (acc2_ref[...] / l2)
    ms = jnp.mean(o * o, axis=-1, keepdims=True)
    o_ref[0] = (o * lax.rsqrt(ms + NORM_EPS) * g_ref[...] * (1.0 - lam_init)).astype(BF16)


def _diff_call(proj3d, lq1, lk1, lq2, lk2, gain, lam_init, *, tq):
    b, s, _ = proj3d.shape
    assert s % tq == 0 and tq % CHUNK == 0
    w = 2 * HEAD_DIM
    hq, hk, hv = 3 * SECTION // w, 4 * SECTION // w, 5 * SECTION // w
    vec = pl.BlockSpec((1, HEAD_DIM), lambda bi, h, i: (0, 0))
    return pl.pallas_call(
        functools.partial(_diff_kernel, tq=tq, lam_init=lam_init),
        grid=(b, N_DIFF_HEADS, s // tq),
        in_specs=[
            pl.BlockSpec((1, tq, w), lambda bi, h, i: (bi, i, hq + h)),
            pl.BlockSpec((1, s, w), lambda bi, h, i: (bi, 0, hk + h)),
            pl.BlockSpec((1, s, w), lambda bi, h, i: (bi, 0, hv + h)),
            vec, vec, vec, vec,
            pl.BlockSpec((1, w), lambda bi, h, i: (0, 0)),
        ],
        out_specs=pl.BlockSpec((1, tq, w), lambda bi, h, i: (bi, i, h)),
        out_shape=jax.ShapeDtypeStruct((b, s, N_DIFF_HEADS * w), BF16),
        scratch_shapes=[pltpu.VMEM((tq, w), F32), pltpu.VMEM((tq, w), F32)],
        compiler_params=_cparams(("arbitrary", "arbitrary", "arbitrary")),
        name="diff_attn",
    )(proj3d, proj3d, proj3d, lq1, lk1, lq2, lk2, gain)


def _outproj_kernel(x_ref, sb_ref, df_ref, wo_ref, g_ref, wr_ref, br_ref, tri_ref,
                    x1_ref, h2_ref, ri_ref, rf_ref, cnt_ref):
    i = pl.program_id(0)
    tm = x_ref.shape[0]
    half = sb_ref.shape[1]
    x1 = x_ref[...] + _dot(sb_ref[...], wo_ref[:half, :]) + _dot(df_ref[...], wo_ref[half:, :])
    x1_ref[...] = x1
    ms = jnp.mean(x1 * x1, axis=-1, keepdims=True)
    h2 = x1 * lax.rsqrt(ms + NORM_EPS) * g_ref[...]
    h2_ref[...] = h2

    logits = _dot(h2.astype(BF16), wr_ref[...]) + br_ref[...]
    lane = lax.broadcasted_iota(jnp.int32, (tm, LANES), 1).astype(F32)
    ninf = -jnp.inf

    def first_argmax(vals):
        top = jnp.max(vals, axis=-1, keepdims=True)
        idx = jnp.min(jnp.where(vals == top, lane, float(LANES)), axis=-1, keepdims=True)
        return top, idx

    gl = jnp.where(lane < N_GROUPS, logits, ninf)
    gmax, gidx = first_argmax(gl)
    g_val = 1.0 / jnp.sum(jnp.exp(gl - gmax), axis=-1, keepdims=True)
    lo = N_GROUPS + EXPERTS_PER_GROUP * gidx
    el = jnp.where((lane >= lo) & (lane < lo + EXPERTS_PER_GROUP), logits, ninf)
    l1, i1 = first_argmax(el)
    l2, i2 = first_argmax(jnp.where(lane == i1, ninf, el))
    r = jnp.exp(l2 - l1)
    w1 = g_val / (1.0 + r)
    w2 = g_val * r / (1.0 + r)
    e1 = i1 - N_GROUPS
    e2 = i2 - N_GROUPS

    @pl.when(i == 0)
    def _():
        cnt_ref[...] = jnp.zeros_like(cnt_ref)

    onehot = jnp.where((lane == e1) | (lane == e2), 1.0, 0.0)
    before = _dot(tri_ref[...], onehot.astype(BF16)) + cnt_ref[0:1, :]
    rank1 = jnp.sum(jnp.where(lane == e1, before, 0.0), axis=-1, keepdims=True)
    rank2 = jnp.sum(jnp.where(lane == e2, before, 0.0), axis=-1, keepdims=True)
    cnt_ref[0:1, :] = cnt_ref[0:1, :] + jnp.sum(onehot, axis=0, keepdims=True)

    ri = jnp.where(lane == 0, e1, jnp.where(lane == 1, e2,
                   jnp.where(lane == 2, rank1, jnp.where(lane == 3, rank2, 0.0))))
    ri_ref[...] = ri.astype(jnp.int32)
    rf_ref[...] = jnp.where(lane == 0, w1, jnp.where(lane == 1, w2, 0.0))


def _outproj_call(x2d, sb_out, d_out, w_out, gain, w_router, b_router, tri, *, tm):
    t, d = x2d.shape
    half = sb_out.shape[1]
    assert t % tm == 0
    row = lambda i: (i, 0)
    fixed = lambda i: (0, 0)
    return pl.pallas_call(
        _outproj_kernel,
        grid=(t // tm,),
        in_specs=[
            pl.BlockSpec((tm, d), row),
            pl.BlockSpec((tm, half), row),
            pl.BlockSpec((tm, half), row),
            pl.BlockSpec((2 * half, d), fixed),
            pl.BlockSpec((1, d), fixed),
            pl.BlockSpec((d, LANES), fixed),
            pl.BlockSpec((1, LANES), fixed),
            pl.BlockSpec((tm, tm), fixed),
        ],
        out_specs=[
            pl.BlockSpec((tm, d), row),
            pl.BlockSpec((tm, d), row),
            pl.BlockSpec((tm, LANES), row),
            pl.BlockSpec((tm, LANES), row),
            pl.BlockSpec((8, LANES), fixed),
        ],
        out_shape=[
            jax.ShapeDtypeStruct((t, d), F32),
            jax.ShapeDtypeStruct((t, d), F32),
            jax.ShapeDtypeStruct((t, LANES), jnp.int32),
            jax.ShapeDtypeStruct((t, LANES), F32),
            jax.ShapeDtypeStruct((8, LANES), F32),
        ],
        compiler_params=_cparams(("arbitrary",)),
        name="outproj_router",
    )(x2d, sb_out, d_out, w_out, gain, w_router, b_router, tri)


def _dispatch_kernel(pos_hbm, h_ref, xs_hbm, pos_smem, sem_idx, sem, *, td):
    i = pl.program_id(0)
    idx_copy = pltpu.make_async_copy(pos_hbm.at[i], pos_smem, sem_idx)
    idx_copy.start()
    idx_copy.wait()

    def row_copy(t, k):
        return pltpu.make_async_copy(h_ref.at[pl.ds(t, 1)],
                                     xs_hbm.at[pl.ds(pos_smem[0, 2 * t + k], 1)], sem)

    def issue(t, c):
        row_copy(t, 0).start()
        row_copy(t, 1).start()
        return c

    lax.fori_loop(0, td, issue, 0)

    def drain(t, c):
        row_copy(t, 0).wait()
        row_copy(t, 1).wait()
        return c

    lax.fori_loop(0, td, drain, 0)


def _dispatch_call(pos, h2, n_rows, *, td):
    t, d = h2.shape
    assert t % td == 0
    pos3 = pos.reshape(t // td, 1, 2 * td)
    return pl.pallas_call(
        functools.partial(_dispatch_kernel, td=td),
        grid=(t // td,),
        in_specs=[pl.BlockSpec(memory_space=pl.ANY), pl.BlockSpec((td, d), lambda i: (i, 0))],
        out_specs=pl.BlockSpec(memory_space=pl.ANY),
        out_shape=jax.ShapeDtypeStruct((n_rows, d), h2.dtype),
        scratch_shapes=[pltpu.SMEM((1, 2 * td), jnp.int32), pltpu.SemaphoreType.DMA,
                        pltpu.SemaphoreType.DMA],
        compiler_params=_cparams(("arbitrary",)),
        name="dispatch",
    )(pos3, h2)


def _experts_kernel(wt_ref, we_ref, nw_ref, lo_ref, hi_ref, xs_ref, wg_ref, wu_ref, wd_ref, ys_ref):
    w = pl.program_id(0)
    tx = xs_ref.shape[0]
    tile = wt_ref[w]
    first = (w == 0) | (wt_ref[jnp.maximum(w - 1, 0)] != tile)

    @pl.when(w < nw_ref[0])
    def _():
        e = we_ref[w]
        rows = tile * tx + lax.broadcasted_iota(jnp.int32, (tx, 1), 0)
        member = (rows >= lo_ref[e]) & (rows < hi_ref[e])
        x = xs_ref[...].astype(BF16)
        gate = _dot(x, wg_ref[0].astype(BF16))
        up = _dot(x, wu_ref[0].astype(BF16))
        hid = gate * (1.0 / (1.0 + jnp.exp(-gate))) * up
        y = _dot(jnp.where(member, hid, 0.0).astype(BF16), wd_ref[0].astype(BF16))

        @pl.when(first)
        def _():
            ys_ref[...] = y

        @pl.when(jnp.logical_not(first))
        def _():
            ys_ref[...] += y


def _experts_call(work_tile, work_expert, n_work, seg_lo, seg_hi, xs, w_gate, w_up, w_down, *, tx):
    p, d = xs.shape
    de = w_gate.shape[2]
    assert p % tx == 0
    n_items = work_tile.shape[0]
    tile = lambda w, wt, we, nw, lo, hi: (wt[w], 0)
    expert = lambda w, wt, we, nw, lo, hi: (we[w], 0, 0)
    return pl.pallas_call(
        _experts_kernel,
        grid_spec=pltpu.PrefetchScalarGridSpec(
            num_scalar_prefetch=5,
            grid=(n_items,),
            in_specs=[
                pl.BlockSpec((tx, d), tile),
                pl.BlockSpec((1, d, de), expert),
                pl.BlockSpec((1, d, de), expert),
                pl.BlockSpec((1, de, d), expert),
            ],
            out_specs=pl.BlockSpec((tx, d), tile),
        ),
        out_shape=jax.ShapeDtypeStruct((p, d), F32),
        compiler_params=_cparams(("arbitrary",)),
        name="experts",
    )(work_tile, work_expert, n_work, seg_lo, seg_hi, xs, w_gate, w_up, w_down)


def _combine_kernel(pos_hbm, x1_ref, rf_ref, g_ref, ys_hbm, o_ref, pos_smem, y0_ref, y1_ref,
                    sem_idx, sem, *, tc):
    i = pl.program_id(0)
    idx_copy = pltpu.make_async_copy(pos_hbm.at[i], pos_smem, sem_idx)
    idx_copy.start()
    idx_copy.wait()

    def row_copy(t, k):
        dst = y0_ref if k == 0 else y1_ref
        return pltpu.make_async_copy(ys_hbm.at[pl.ds(pos_smem[0, 2 * t + k], 1)],
                                     dst.at[pl.ds(t, 1)], sem)

    def issue(t, c):
        row_copy(t, 0).start()
        row_copy(t, 1).start()
        return c

    lax.fori_loop(0, tc, issue, 0)

    def drain(t, c):
        row_copy(t, 0).wait()
        row_copy(t, 1).wait()
        return c

    lax.fori_loop(0, tc, drain, 0)

    x = x1_ref[...] + rf_ref[:, 0:1] * y0_ref[...] + rf_ref[:, 1:2] * y1_ref[...]
    ms = jnp.mean(x * x, axis=-1, keepdims=True)
    o_ref[...] = x * lax.rsqrt(ms + NORM_EPS) * g_ref[...]


def _combine_call(pos, x1, rf, gain, ys, *, tc):
    t, d = x1.shape
    assert t % tc == 0
    pos3 = pos.reshape(t // tc, 1, 2 * tc)
    row = lambda i: (i, 0)
    return pl.pallas_call(
        functools.partial(_combine_kernel, tc=tc),
        grid=(t // tc,),
        in_specs=[
            pl.BlockSpec(memory_space=pl.ANY),
            pl.BlockSpec((tc, d), row),
            pl.BlockSpec((tc, LANES), row),
            pl.BlockSpec((1, d), lambda i: (0, 0)),
            pl.BlockSpec(memory_space=pl.ANY),
        ],
        out_specs=pl.BlockSpec((tc, d), row),
        out_shape=jax.ShapeDtypeStruct((t, d), F32),
        scratch_shapes=[pltpu.SMEM((1, 2 * tc), jnp.int32), pltpu.VMEM((tc, d), F32),
                        pltpu.VMEM((tc, d), F32), pltpu.SemaphoreType.DMA,
                        pltpu.SemaphoreType.DMA],
        compiler_params=_cparams(("arbitrary",)),
        name="combine",
    )(pos3, x1, rf, gain, ys)


def _pick(n, pref):
    while n % pref:
        pref //= 2
    return pref


def _layer(x2d, b, s, layer, attn_norm_gain, w_in, sb_norm_gain, lq1, lk1, lq2, lk2, subln_gain,
           w_out, ffn_norm_gain, w_gr, b_gr, w_er, b_er, w_gate, w_up, w_down):
    t, d = x2d.shape
    tb = _pick(s, 256)
    tm_proj = _pick(s, 1024)
    tm_out = _pick(t, 256)
    tx = 256

    half = HEAD_DIM // 2
    inv_freq = 1.0 / (ROPE_THETA ** (jnp.arange(half, dtype=F32) / half))
    ang = jnp.arange(s, dtype=F32)[:, None] * inv_freq[None, :]
    cos = jnp.concatenate([jnp.cos(ang), jnp.cos(ang)], axis=-1)
    sin = jnp.concatenate([-jnp.sin(ang), jnp.sin(ang)], axis=-1)

    proj = _proj_call(x2d, attn_norm_gain.reshape(1, d), w_in.astype(BF16), cos, sin, s,
                      tm=tm_proj, tn=512)
    proj3d = proj.reshape(b, s, proj.shape[1])

    r = jnp.arange(tb)
    tri_suffix = (r[:, None] >= r[None, :]).astype(BF16)
    sb_out = _sb_call(proj3d, tri_suffix, sb_norm_gain.reshape(1, HEAD_DIM), tb=tb, nsub=2)

    lam_init = 0.8 - 0.6 * math.exp(-0.3 * layer)
    d_out = _diff_call(proj3d, lq1.reshape(1, -1), lk1.reshape(1, -1), lq2.reshape(1, -1),
                       lk2.reshape(1, -1), subln_gain.reshape(1, -1), lam_init, tq=2 * tb)

    w_router = jnp.zeros((d, LANES), F32)
    w_router = w_router.at[:, :N_GROUPS].set(w_gr).at[:, N_GROUPS:N_GROUPS + N_EXPERTS].set(w_er)
    b_router = jnp.zeros((1, LANES), F32)
    b_router = b_router.at[0, :N_GROUPS].set(b_gr).at[0, N_GROUPS:N_GROUPS + N_EXPERTS].set(b_er)
    rr = jnp.arange(tm_out)
    tri_before = (rr[None, :] < rr[:, None]).astype(BF16)
    x1, h2, ri, rf, cnt = _outproj_call(
        x2d, sb_out.reshape(t, -1), d_out.reshape(t, -1), w_out.astype(BF16),
        ffn_norm_gain.reshape(1, d), w_router.astype(BF16), b_router, tri_before, tm=tm_out)

    counts = cnt[0, :N_EXPERTS].astype(jnp.int32)
    seg_hi = jnp.cumsum(counts)
    seg_lo = seg_hi - counts
    pos = (seg_lo[ri[:, 0:2]] + ri[:, 2:4]).reshape(-1)
    first_tile = seg_lo // tx
    items = jnp.where(counts > 0, (seg_hi - 1) // tx - first_tile + 1, 0)
    item_hi = jnp.cumsum(items)
    n_work = item_hi[-1]
    w = jnp.minimum(jnp.arange(2 * t // tx + N_EXPERTS - 1, dtype=jnp.int32), n_work - 1)
    work_expert = jnp.sum((item_hi[None, :] <= w[:, None]).astype(jnp.int32), axis=1)
    work_tile = (first_tile[work_expert] + w - (item_hi - items)[work_expert]).astype(jnp.int32)

    xs = _dispatch_call(pos, h2, 2 * t, td=tm_out)
    ys = _experts_call(work_tile, work_expert, n_work.reshape(1).astype(jnp.int32), seg_lo, seg_hi,
                       xs, w_gate, w_up, w_down, tx=tx)
    return x1, pos, rf, ys


def kernel(x, attn_norm_gain, w_in, sb_norm_gain, diff_lambda_q1, diff_lambda_k1, diff_lambda_q2,
           diff_lambda_k2, diff_subln_gain, w_out, ffn_norm_gain, w_group_router, b_group_router,
           w_expert_router, b_expert_router, w_gate, w_up, w_down, final_norm_gain):
    b, s, d = x.shape
    assert w_in.shape[0] == 1, "the combine stage fuses the final norm: single-layer stacks only"
    layer = 0
    x2d = x.reshape(b * s, d)
    x1, pos, rf, ys = _layer(
        x2d, b, s, layer, attn_norm_gain[layer], w_in[layer], sb_norm_gain[layer],
        diff_lambda_q1[layer], diff_lambda_k1[layer], diff_lambda_q2[layer],
        diff_lambda_k2[layer], diff_subln_gain[layer], w_out[layer], ffn_norm_gain[layer],
        w_group_router[layer], b_group_router[layer], w_expert_router[layer],
        b_expert_router[layer], w_gate[layer], w_up[layer], w_down[layer])
    out = _combine_call(pos, x1, rf, final_norm_gain.reshape(1, d), ys, tc=_pick(b * s, 256))
    return out.reshape(b, s, d)
```

```python
import functools
import math

import jax
import jax.numpy as jnp
from jax import lax
from jax.experimental import pallas as pl
from jax.experimental.pallas import tpu as pltpu

F32 = jnp.float32
BF16 = jnp.bfloat16

HEAD_DIM = 128
N_SB_HEADS = 8
N_DIFF_HEADS = 4
SECTION = 1024
CHUNK = 64
ROPE_THETA = 10000.0
N_GROUPS = 4
EXPERTS_PER_GROUP = 8
N_EXPERTS = N_GROUPS * EXPERTS_PER_GROUP
NORM_EPS = 1e-6
NEG_INF = -1e30
LANES = 128
ROW_CHUNK = 32
DMA_UNROLL = 8
SCALE_LOG2E = math.log2(math.e) / math.sqrt(HEAD_DIM)

VMEM_LIMIT = 56 * 1024 * 1024


def _cparams(sem):
    return pltpu.CompilerParams(dimension_semantics=sem, vmem_limit_bytes=VMEM_LIMIT)


def _dot(a, b):
    return jnp.dot(a, b, preferred_element_type=F32)


def _dot_nt(a, b):
    return lax.dot_general(a, b, (((1,), (1,)), ((), ())), preferred_element_type=F32)


def _proj_kernel(x_ref, g_ref, w_ref, cos_ref, sin_ref, o_ref, h_ref, *, tn, rows):
    j = pl.program_id(1)
    tm = x_ref.shape[0]

    @pl.when(j == 0)
    def _():
        for r in range(0, tm, rows):
            x = x_ref[r:r + rows, :]
            ms = jnp.mean(x * x, axis=-1, keepdims=True)
            h_ref[r:r + rows, :] = (x * lax.rsqrt(ms + NORM_EPS) * g_ref[...]).astype(BF16)

    acc = _dot(h_ref[...], w_ref[...])
    sec = j // (SECTION // tn)

    def rotary(scale):
        for c in range(tn // HEAD_DIM):
            a = acc[:, c * HEAD_DIM:(c + 1) * HEAD_DIM]
            rot = a * cos_ref[...] + pltpu.roll(a, HEAD_DIM // 2, 1) * sin_ref[...]
            if scale != 1.0:
                rot = rot * scale
            o_ref[:, c * HEAD_DIM:(c + 1) * HEAD_DIM] = rot.astype(BF16)

    @pl.when(sec == 0)
    def _():
        o_ref[...] = (acc * (-SCALE_LOG2E)).astype(BF16)

    @pl.when((sec == 1) | (sec == 2) | (sec == 5))
    def _():
        o_ref[...] = acc.astype(BF16)

    @pl.when(sec == 3)
    def _():
        rotary(SCALE_LOG2E)

    @pl.when(sec == 4)
    def _():
        rotary(1.0)


def _proj_call(x2d, gain, w_in, cos, sin, seq, *, tm, tn):
    t, d = x2d.shape
    n = w_in.shape[1]
    assert t % tm == 0 and seq % tm == 0 and n % tn == 0 and SECTION % tn == 0
    rows = min(tm, 256)
    nseq = seq // tm
    return pl.pallas_call(
        functools.partial(_proj_kernel, tn=tn, rows=rows),
        grid=(t // tm, n // tn),
        in_specs=[
            pl.BlockSpec((tm, d), lambda i, j: (i, 0)),
            pl.BlockSpec((1, d), lambda i, j: (0, 0)),
            pl.BlockSpec((d, tn), lambda i, j: (0, j)),
            pl.BlockSpec((tm, HEAD_DIM), lambda i, j: (i % nseq, 0)),
            pl.BlockSpec((tm, HEAD_DIM), lambda i, j: (i % nseq, 0)),
        ],
        out_specs=pl.BlockSpec((tm, tn), lambda i, j: (i, j)),
        out_shape=jax.ShapeDtypeStruct((t, n), BF16),
        scratch_shapes=[pltpu.VMEM((tm, d), BF16)],
        compiler_params=_cparams(("arbitrary", "arbitrary")),
        name="proj",
    )(x2d, gain, w_in, cos, sin)


def _sb_kernel(q_ref, k_ref, v_ref, tri_ref, g_ref, o_ref, acc_ref, *, tb, nsub):
    qi = pl.program_id(2)
    tq = nsub * tb
    lo, hi = slice(0, tb), slice(tb, tq)
    row = lax.broadcasted_iota(jnp.int32, (tb, tb), 0)
    col = lax.broadcasted_iota(jnp.int32, (tb, tb), 1)
    strict = col < row

    def scores(rows, kj):
        start = pl.multiple_of(kj * tb, tb)
        return _dot_nt(q_ref[0, rows, :], k_ref[0, pl.ds(start, tb), :])

    def keep_logs(n, mask):
        lks = []
        for r in range(0, n.shape[0], ROW_CHUNK):
            nc = n[r:r + ROW_CHUNK]
            lk = jnp.minimum(nc, 0.0) - jnp.log2(1.0 + jnp.exp2(-jnp.abs(nc)))
            if mask is not None:
                lk = jnp.where(mask[r:r + ROW_CHUNK], lk, 0.0)
            lks.append(lk.astype(BF16))
        return jnp.concatenate(lks, axis=0)

    def accumulate(rows, kj, n, lk, carry, mask):
        start = pl.multiple_of(kj * tb, tb)
        cum = _dot(lk, tri_ref[...])
        parts = []
        for r in range(0, n.shape[0], ROW_CHUNK):
            chunk = slice(r, r + ROW_CHUNK)
            a = jnp.exp2(cum[chunk] + carry[chunk] - n[chunk])
            if mask is not None:
                a = jnp.where(mask[chunk], a, 0.0)
            parts.append(a.astype(BF16))
        acc_ref[rows, :] += _dot(jnp.concatenate(parts, axis=0), v_ref[0, pl.ds(start, tb), :])
        return carry + cum[:, 0:1]

    def several(rows, kjs, carry, mask=None):
        staged = []
        for kj in kjs:
            n = scores(rows, kj)
            staged.append((kj, n, keep_logs(n, mask)))
        for kj, n, lk in staged:
            carry = accumulate(rows, kj, n, lk, carry, mask)
        return carry

    acc_ref[...] = jnp.zeros_like(acc_ref)
    base = qi * nsub
    zero = jnp.zeros((tb, 1), F32)
    carry_hi = several(hi, [base + 1], zero, strict)
    carry_lo = several(lo, [base], zero, strict)
    carry_hi = several(hi, [base], carry_hi)
    carry = jnp.concatenate([carry_lo, carry_hi], axis=0)
    full = slice(0, tq)
    npairs = base // 2
    odd = npairs % 2
    carry = lax.cond(odd == 1, lambda c: several(full, [base - 1, base - 2], c), lambda c: c, carry)
    top = base - 1 - 2 * odd
    lax.fori_loop(0, npairs // 2,
                  lambda i, c: several(full, [top - 4 * i - j for j in range(4)], c), carry)

    o = acc_ref[...]
    ms = jnp.mean(o * o, axis=-1, keepdims=True)
    o_ref[0] = (o * lax.rsqrt(ms + NORM_EPS) * g_ref[...]).astype(BF16)


def _sb_call(proj3d, tri, gain, *, tb, nsub):
    b, s, _ = proj3d.shape
    tq = nsub * tb
    assert s % tq == 0 and nsub == 2 and tb % ROW_CHUNK == 0
    hq, hk, hv = 0, SECTION // HEAD_DIM, 2 * SECTION // HEAD_DIM
    return pl.pallas_call(
        functools.partial(_sb_kernel, tb=tb, nsub=nsub),
        grid=(b, N_SB_HEADS, s // tq),
        in_specs=[
            pl.BlockSpec((1, tq, HEAD_DIM), lambda bi, h, i: (bi, i, hq + h)),
            pl.BlockSpec((1, s, HEAD_DIM), lambda bi, h, i: (bi, 0, hk + h)),
            pl.BlockSpec((1, s, HEAD_DIM), lambda bi, h, i: (bi, 0, hv + h)),
            pl.BlockSpec((tb, tb), lambda bi, h, i: (0, 0)),
            pl.BlockSpec((1, HEAD_DIM), lambda bi, h, i: (0, 0)),
        ],
        out_specs=pl.BlockSpec((1, tq, HEAD_DIM), lambda bi, h, i: (bi, i, h)),
        out_shape=jax.ShapeDtypeStruct((b, s, N_SB_HEADS * HEAD_DIM), BF16),
        scratch_shapes=[pltpu.VMEM((tq, HEAD_DIM), F32)],
        compiler_params=_cparams(("arbitrary", "arbitrary", "arbitrary")),
        name="sb_attn",
    )(proj3d, proj3d, proj3d, tri, gain)


def _diff_kernel(q_ref, k_ref, v_ref, lq1_ref, lk1_ref, lq2_ref, lk2_ref, g_ref, o_ref,
                 acc1_ref, acc2_ref, *, tq, lam_init):
    qi = pl.program_id(2)
    d = HEAD_DIM
    q1 = q_ref[0, :, :d]
    q2 = q_ref[0, :, d:]
    row = lax.broadcasted_iota(jnp.int32, (tq, tq), 0)
    col = lax.broadcasted_iota(jnp.int32, (tq, tq), 1)
    visible = (col // CHUNK) <= (row // CHUNK)

    def softmax_step(s, m, l, acc_ref, v):
        m_new = jnp.maximum(m, jnp.max(s, axis=-1, keepdims=True))
        alpha = jnp.exp2(m - m_new)
        p = jnp.exp2(s - m_new)
        acc_ref[...] = alpha * acc_ref[...] + _dot(p.astype(BF16), v)
        return m_new, alpha * l + jnp.sum(p, axis=-1, keepdims=True)

    def block(start, width, carry, masked):
        m1, l1, m2, l2 = carry
        k = k_ref[0, pl.ds(start, width), :]
        v = v_ref[0, pl.ds(start, width), :]
        s1 = _dot_nt(q1, k[:, :d])
        s2 = _dot_nt(q2, k[:, d:])
        if masked:
            s1 = jnp.where(visible, s1, NEG_INF)
            s2 = jnp.where(visible, s2, NEG_INF)
        m1, l1 = softmax_step(s1, m1, l1, acc1_ref, v)
        m2, l2 = softmax_step(s2, m2, l2, acc2_ref, v)
        return m1, l1, m2, l2

    acc1_ref[...] = jnp.zeros_like(acc1_ref)
    acc2_ref[...] = jnp.zeros_like(acc2_ref)
    neg = jnp.full((tq, 1), NEG_INF, F32)
    zero = jnp.zeros((tq, 1), F32)
    carry = block(pl.multiple_of(qi * tq, tq), tq, (neg, zero, neg, zero), True)
    odd = qi % 2
    carry = lax.cond(odd == 1, lambda c: block(pl.multiple_of((qi - 1) * tq, tq), tq, c, False),
                     lambda c: c, carry)
    _, l1, _, l2 = lax.fori_loop(
        0, qi // 2, lambda i, c: block(pl.multiple_of(2 * i * tq, 2 * tq), 2 * tq, c, False), carry)

    lam = (jnp.exp(jnp.sum(lq1_ref[...] * lk1_ref[...], axis=-1, keepdims=True))
           - jnp.exp(jnp.sum(lq2_ref[...] * lk2_ref[...], axis=-1, keepdims=True)) + lam_init)
    o = acc1_ref[...] / l1 - lam * (acc2_ref[...] / l2)
    ms = jnp.mean(o * o, axis=-1, keepdims=True)
    o_ref[0] = (o * lax.rsqrt(ms + NORM_EPS) * g_ref[...] * (1.0 - lam_init)).astype(BF16)


def _diff_call(proj3d, lq1, lk1, lq2, lk2, gain, lam_init, *, tq):
    b, s, _ = proj3d.shape
    assert s % tq == 0 and tq % CHUNK == 0
    w = 2 * HEAD_DIM
    hq, hk, hv = 3 * SECTION // w, 4 * SECTION // w, 5 * SECTION // w
    vec = pl.BlockSpec((1, HEAD_DIM), lambda bi, h, i: (0, 0))
    return pl.pallas_call(
        functools.partial(_diff_kernel, tq=tq, lam_init=lam_init),
        grid=(b, N_DIFF_HEADS, s // tq),
        in_specs=[
            pl.BlockSpec((1, tq, w), lambda bi, h, i: (bi, i, hq + h)),
            pl.BlockSpec((1, s, w), lambda bi, h, i: (bi, 0, hk + h)),
            pl.BlockSpec((1, s, w), lambda bi, h, i: (bi, 0, hv + h)),
            vec, vec, vec, vec,
            pl.BlockSpec((1, w), lambda bi, h, i: (0, 0)),
        ],
        out_specs=pl.BlockSpec((1, tq, w), lambda bi, h, i: (bi, i, h)),
        out_shape=jax.ShapeDtypeStruct((b, s, N_DIFF_HEADS * w), BF16),
        scratch_shapes=[pltpu.VMEM((tq, w), F32), pltpu.VMEM((tq, w), F32)],
        compiler_params=_cparams(("arbitrary", "arbitrary", "arbitrary")),
        name="diff_attn",
    )(proj3d, proj3d, proj3d, lq1, lk1, lq2, lk2, gain)


def _outproj_kernel(x_ref, sb_ref, df_ref, wo_ref, g_ref, wr_ref, br_ref, tri_ref,
                    x1_ref, h2_ref, ri_ref, rf_ref, cnt_ref):
    i = pl.program_id(0)
    tm = x_ref.shape[0]
    half = sb_ref.shape[1]
    x1 = x_ref[...] + _dot(sb_ref[...], wo_ref[:half, :]) + _dot(df_ref[...], wo_ref[half:, :])
    x1_ref[...] = x1
    ms = jnp.mean(x1 * x1, axis=-1, keepdims=True)
    h2 = x1 * lax.rsqrt(ms + NORM_EPS) * g_ref[...]
    h2_ref[...] = h2

    logits = _dot(h2.astype(BF16), wr_ref[...]) + br_ref[...]
    lane = lax.broadcasted_iota(jnp.int32, (tm, LANES), 1).astype(F32)
    ninf = -jnp.inf

    def first_argmax(vals):
        top = jnp.max(vals, axis=-1, keepdims=True)
        idx = jnp.min(jnp.where(vals == top, lane, float(LANES)), axis=-1, keepdims=True)
        return top, idx

    gl = jnp.where(lane < N_GROUPS, logits, ninf)
    gmax, gidx = first_argmax(gl)
    g_val = 1.0 / jnp.sum(jnp.exp(gl - gmax), axis=-1, keepdims=True)
    lo = N_GROUPS + EXPERTS_PER_GROUP * gidx
    el = jnp.where((lane >= lo) & (lane < lo + EXPERTS_PER_GROUP), logits, ninf)
    l1, i1 = first_argmax(el)
    l2, i2 = first_argmax(jnp.where(lane == i1, ninf, el))
    r = jnp.exp(l2 - l1)
    w1 = g_val / (1.0 + r)
    w2 = g_val * r / (1.0 + r)
    e1 = i1 - N_GROUPS
    e2 = i2 - N_GROUPS

    @pl.when(i == 0)
    def _():
        cnt_ref[...] = jnp.zeros_like(cnt_ref)

    onehot = jnp.where((lane == e1) | (lane == e2), 1.0, 0.0)
    before = _dot(tri_ref[...], onehot.astype(BF16)) + cnt_ref[0:1, :]
    rank1 = jnp.sum(jnp.where(lane == e1, before, 0.0), axis=-1, keepdims=True)
    rank2 = jnp.sum(jnp.where(lane == e2, before, 0.0), axis=-1, keepdims=True)
    cnt_ref[0:1, :] = cnt_ref[0:1, :] + jnp.sum(onehot, axis=0, keepdims=True)

    ri = jnp.where(lane == 0, e1, jnp.where(lane == 1, e2,
                   jnp.where(lane == 2, rank1, jnp.where(lane == 3, rank2, 0.0))))
    ri_ref[...] = ri.astype(jnp.int32)
    rf_ref[...] = jnp.where(lane == 0, w1, jnp.where(lane == 1, w2, 0.0))


def _outproj_call(x2d, sb_out, d_out, w_out, gain, w_router, b_router, tri, *, tm):
    t, d = x2d.shape
    half = sb_out.shape[1]
    assert t % tm == 0
    row = lambda i: (i, 0)
    fixed = lambda i: (0, 0)
    return pl.pallas_call(
        _outproj_kernel,
        grid=(t // tm,),
        in_specs=[
            pl.BlockSpec((tm, d), row),
            pl.BlockSpec((tm, half), row),
            pl.BlockSpec((tm, half), row),
            pl.BlockSpec((2 * half, d), fixed),
            pl.BlockSpec((1, d), fixed),
            pl.BlockSpec((d, LANES), fixed),
            pl.BlockSpec((1, LANES), fixed),
            pl.BlockSpec((tm, tm), fixed),
        ],
        out_specs=[
            pl.BlockSpec((tm, d), row),
            pl.BlockSpec((tm, d), row),
            pl.BlockSpec((tm, LANES), row),
            pl.BlockSpec((tm, LANES), row),
            pl.BlockSpec((8, LANES), fixed),
        ],
        out_shape=[
            jax.ShapeDtypeStruct((t, d), F32),
            jax.ShapeDtypeStruct((t, d), F32),
            jax.ShapeDtypeStruct((t, LANES), jnp.int32),
            jax.ShapeDtypeStruct((t, LANES), F32),
            jax.ShapeDtypeStruct((8, LANES), F32),
        ],
        compiler_params=_cparams(("arbitrary",)),
        name="outproj_router",
    )(x2d, sb_out, d_out, w_out, gain, w_router, b_router, tri)


def _dispatch_kernel(pos_hbm, h_ref, xs_hbm, pos_smem, sem_idx, sem, *, td):
    i = pl.program_id(0)
    idx_copy = pltpu.make_async_copy(pos_hbm.at[i], pos_smem, sem_idx)
    idx_copy.start()
    idx_copy.wait()

    def row_copy(t, k):
        return pltpu.make_async_copy(h_ref.at[pl.ds(t, 1)],
                                     xs_hbm.at[pl.ds(pos_smem[0, 2 * t + k], 1)], sem)

    def issue(t, c):
        row_copy(t, 0).start()
        row_copy(t, 1).start()
        return c

    lax.fori_loop(0, td, issue, 0, unroll=DMA_UNROLL)

    def drain(t, c):
        row_copy(t, 0).wait()
        row_copy(t, 1).wait()
        return c

    lax.fori_loop(0, td, drain, 0, unroll=DMA_UNROLL)


def _dispatch_call(pos, h2, n_rows, *, td):
    t, d = h2.shape
    assert t % td == 0
    pos3 = pos.reshape(t // td, 1, 2 * td)
    return pl.pallas_call(
        functools.partial(_dispatch_kernel, td=td),
        grid=(t // td,),
        in_specs=[pl.BlockSpec(memory_space=pl.ANY), pl.BlockSpec((td, d), lambda i: (i, 0))],
        out_specs=pl.BlockSpec(memory_space=pl.ANY),
        out_shape=jax.ShapeDtypeStruct((n_rows, d), h2.dtype),
        scratch_shapes=[pltpu.SMEM((1, 2 * td), jnp.int32), pltpu.SemaphoreType.DMA,
                        pltpu.SemaphoreType.DMA],
        compiler_params=_cparams(("arbitrary",)),
        name="dispatch",
    )(pos3, h2)


def _experts_kernel(wt_ref, we_ref, nw_ref, lo_ref, hi_ref, xs_ref, wg_ref, wu_ref, wd_ref, ys_ref):
    w = pl.program_id(0)
    tx = xs_ref.shape[0]
    tile = wt_ref[w]
    first = (w == 0) | (wt_ref[jnp.maximum(w - 1, 0)] != tile)

    @pl.when(w < nw_ref[0])
    def _():
        e = we_ref[w]
        rows = tile * tx + lax.broadcasted_iota(jnp.int32, (tx, 1), 0)
        member = (rows >= lo_ref[e]) & (rows < hi_ref[e])
        x = xs_ref[...].astype(BF16)
        gate = _dot(x, wg_ref[0].astype(BF16))
        up = _dot(x, wu_ref[0].astype(BF16))
        hid = gate * (1.0 / (1.0 + jnp.exp(-gate))) * up
        y = _dot(jnp.where(member, hid, 0.0).astype(BF16), wd_ref[0].astype(BF16))

        @pl.when(first)
        def _():
            ys_ref[...] = y

        @pl.when(jnp.logical_not(first))
        def _():
            ys_ref[...] += y


def _experts_call(work_tile, work_expert, n_work, seg_lo, seg_hi, xs, w_gate, w_up, w_down, *, tx):
    p, d = xs.shape
    de = w_gate.shape[2]
    assert p % tx == 0
    n_items = work_tile.shape[0]
    tile = lambda w, wt, we, nw, lo, hi: (wt[w], 0)
    expert = lambda w, wt, we, nw, lo, hi: (we[w], 0, 0)
    return pl.pallas_call(
        _experts_kernel,
        grid_spec=pltpu.PrefetchScalarGridSpec(
            num_scalar_prefetch=5,
            grid=(n_items,),
            in_specs=[
                pl.BlockSpec((tx, d), tile),
                pl.BlockSpec((1, d, de), expert),
                pl.BlockSpec((1, d, de), expert),
                pl.BlockSpec((1, de, d), expert),
            ],
            out_specs=pl.BlockSpec((tx, d), tile),
        ),
        out_shape=jax.ShapeDtypeStruct((p, d), F32),
        compiler_params=_cparams(("arbitrary",)),
        name="experts",
    )(work_tile, work_expert, n_work, seg_lo, seg_hi, xs, w_gate, w_up, w_down)


def _combine_kernel(pos_hbm, x1_ref, rf_ref, g_ref, ys_hbm, o_ref, pos_smem, y0_ref, y1_ref,
                    sem_idx, sem, *, tc):
    i = pl.program_id(0)
    idx_copy = pltpu.make_async_copy(pos_hbm.at[i], pos_smem, sem_idx)
    idx_copy.start()
    idx_copy.wait()

    def row_copy(t, k):
        dst = y0_ref if k == 0 else y1_ref
        return pltpu.make_async_copy(ys_hbm.at[pl.ds(pos_smem[0, 2 * t + k], 1)],
                                     dst.at[pl.ds(t, 1)], sem)

    def issue(t, c):
        row_copy(t, 0).start()
        row_copy(t, 1).start()
        return c

    lax.fori_loop(0, tc, issue, 0, unroll=DMA_UNROLL)

    def drain(t, c):
        row_copy(t, 0).wait()
        row_copy(t, 1).wait()
        return c

    lax.fori_loop(0, tc, drain, 0, unroll=DMA_UNROLL)

    x = x1_ref[...] + rf_ref[:, 0:1] * y0_ref[...] + rf_ref[:, 1:2] * y1_ref[...]
    ms = jnp.mean(x * x, axis=-1, keepdims=True)
    o_ref[...] = x * lax.rsqrt(ms + NORM_EPS) * g_ref[...]


def _combine_call(pos, x1, rf, gain, ys, *, tc):
    t, d = x1.shape
    assert t % tc == 0
    pos3 = pos.reshape(t // tc, 1, 2 * tc)
    row = lambda i: (i, 0)
    return pl.pallas_call(
        functools.partial(_combine_kernel, tc=tc),
        grid=(t // tc,),
        in_specs=[
            pl.BlockSpec(memory_space=pl.ANY),
            pl.BlockSpec((tc, d), row),
            pl.BlockSpec((tc, LANES), row),
            pl.BlockSpec((1, d), lambda i: (0, 0)),
            pl.BlockSpec(memory_space=pl.ANY),
        ],
        out_specs=pl.BlockSpec((tc, d), row),
        out_shape=jax.ShapeDtypeStruct((t, d), F32),
        scratch_shapes=[pltpu.SMEM((1, 2 * tc), jnp.int32), pltpu.VMEM((tc, d), F32),
                        pltpu.VMEM((tc, d), F32), pltpu.SemaphoreType.DMA,
                        pltpu.SemaphoreType.DMA],
        compiler_params=_cparams(("arbitrary",)),
        name="combine",
    )(pos3, x1, rf, gain, ys)


def _pick(n, pref):
    while n % pref:
        pref //= 2
    return pref


def _layer(x2d, b, s, layer, attn_norm_gain, w_in, sb_norm_gain, lq1, lk1, lq2, lk2, subln_gain,
           w_out, ffn_norm_gain, w_gr, b_gr, w_er, b_er, w_gate, w_up, w_down):
    t, d = x2d.shape
    tb = _pick(s, 256)
    tm_proj = _pick(s, 1024)
    tm_out = _pick(t, 256)
    tx = 256

    half = HEAD_DIM // 2
    inv_freq = 1.0 / (ROPE_THETA ** (jnp.arange(half, dtype=F32) / half))
    ang = jnp.arange(s, dtype=F32)[:, None] * inv_freq[None, :]
    cos = jnp.concatenate([jnp.cos(ang), jnp.cos(ang)], axis=-1)
    sin = jnp.concatenate([-jnp.sin(ang), jnp.sin(ang)], axis=-1)

    proj = _proj_call(x2d, attn_norm_gain.reshape(1, d), w_in.astype(BF16), cos, sin, s,
                      tm=tm_proj, tn=512)
    proj3d = proj.reshape(b, s, proj.shape[1])

    r = jnp.arange(tb)
    tri_suffix = (r[:, None] >= r[None, :]).astype(BF16)
    sb_out = _sb_call(proj3d, tri_suffix, sb_norm_gain.reshape(1, HEAD_DIM), tb=tb, nsub=2)

    lam_init = 0.8 - 0.6 * math.exp(-0.3 * layer)
    d_out = _diff_call(proj3d, lq1.reshape(1, -1), lk1.reshape(1, -1), lq2.reshape(1, -1),
                       lk2.reshape(1, -1), subln_gain.reshape(1, -1), lam_init, tq=2 * tb)

    w_router = jnp.zeros((d, LANES), F32)
    w_router = w_router.at[:, :N_GROUPS].set(w_gr).at[:, N_GROUPS:N_GROUPS + N_EXPERTS].set(w_er)
    b_router = jnp.zeros((1, LANES), F32)
    b_router = b_router.at[0, :N_GROUPS].set(b_gr).at[0, N_GROUPS:N_GROUPS + N_EXPERTS].set(b_er)
    rr = jnp.arange(tm_out)
    tri_before = (rr[None, :] < rr[:, None]).astype(BF16)
    x1, h2, ri, rf, cnt = _outproj_call(
        x2d, sb_out.reshape(t, -1), d_out.reshape(t, -1), w_out.astype(BF16),
        ffn_norm_gain.reshape(1, d), w_router.astype(BF16), b_router, tri_before, tm=tm_out)

    counts = cnt[0, :N_EXPERTS].astype(jnp.int32)
    seg_hi = jnp.cumsum(counts)
    seg_lo = seg_hi - counts
    pos = (seg_lo[ri[:, 0:2]] + ri[:, 2:4]).reshape(-1)
    first_tile = seg_lo // tx
    items = jnp.where(counts > 0, (seg_hi - 1) // tx - first_tile + 1, 0)
    item_hi = jnp.cumsum(items)
    n_work = item_hi[-1]
    w = jnp.minimum(jnp.arange(2 * t // tx + N_EXPERTS - 1, dtype=jnp.int32), n_work - 1)
    work_expert = jnp.sum((item_hi[None, :] <= w[:, None]).astype(jnp.int32), axis=1)
    work_tile = (first_tile[work_expert] + w - (item_hi - items)[work_expert]).astype(jnp.int32)

    xs = _dispatch_call(pos, h2, 2 * t, td=tm_out)
    ys = _experts_call(work_tile, work_expert, n_work.reshape(1).astype(jnp.int32), seg_lo, seg_hi,
                       xs, w_gate, w_up, w_down, tx=tx)
    return x1, pos, rf, ys


def kernel(x, attn_norm_gain, w_in, sb_norm_gain, diff_lambda_q1, diff_lambda_k1, diff_lambda_q2,
           diff_lambda_k2, diff_subln_gain, w_out, ffn_norm_gain, w_group_router, b_group_router,
           w_expert_router, b_expert_router, w_gate, w_up, w_down, final_norm_gain):
    b, s, d = x.shape
    assert w_in.shape[0] == 1, "the combine stage fuses the final norm: single-layer stacks only"
    layer = 0
    x2d = x.reshape(b * s, d)
    x1, pos, rf, ys = _layer(
        x2d, b, s, layer, attn_norm_gain[layer], w_in[layer], sb_norm_gain[layer],
        diff_lambda_q1[layer], diff_lambda_k1[layer], diff_lambda_q2[layer],
        diff_lambda_k2[layer], diff_subln_gain[layer], w_out[layer], ffn_norm_gain[layer],
        w_group_router[layer], b_group_router[layer], w_expert_router[layer],
        b_expert_router[layer], w_gate[layer], w_up[layer], w_down[layer])
    out = _combine_call(pos, x1, rf, final_norm_gain.reshape(1, d), ys, tc=_pick(b * s, 256))
    return out.reshape(b, s, d)
```

```python
import functools
import math

import jax
import jax.numpy as jnp
import numpy as np
from jax import lax
from jax.experimental import pallas as pl
from jax.experimental.pallas import tpu as pltpu

F32 = jnp.float32
BF16 = jnp.bfloat16

HEAD_DIM = 128
N_SB_HEADS = 8
N_DIFF_HEADS = 4
SECTION = 1024
CHUNK = 64
ROPE_THETA = 10000.0
N_GROUPS = 4
EXPERTS_PER_GROUP = 8
N_EXPERTS = N_GROUPS * EXPERTS_PER_GROUP
NORM_EPS = 1e-6
NEG_INF = -1e30
LANES = 128
ROW_CHUNK = 32
UNDERFLOW_LOG2 = -200.0
DMA_UNROLL = 8
SCALE_LOG2E = math.log2(math.e) / math.sqrt(HEAD_DIM)

VMEM_LIMIT = 56 * 1024 * 1024


def _cparams(sem):
    return pltpu.CompilerParams(dimension_semantics=sem, vmem_limit_bytes=VMEM_LIMIT)


def _dot(a, b):
    return jnp.dot(a, b, preferred_element_type=F32)


def _dot_nt(a, b):
    return lax.dot_general(a, b, (((1,), (1,)), ((), ())), preferred_element_type=F32)


def _proj_kernel(x_ref, g_ref, w_ref, cos_ref, sin_ref, o_ref, h_ref, *, tn, rows):
    j = pl.program_id(1)
    tm = x_ref.shape[0]

    @pl.when(j == 0)
    def _():
        for r in range(0, tm, rows):
            x = x_ref[r:r + rows, :]
            ms = jnp.mean(x * x, axis=-1, keepdims=True)
            h_ref[r:r + rows, :] = (x * lax.rsqrt(ms + NORM_EPS) * g_ref[...]).astype(BF16)

    acc = _dot(h_ref[...], w_ref[...])
    sec = j // (SECTION // tn)

    def rotary(scale):
        for c in range(tn // HEAD_DIM):
            a = acc[:, c * HEAD_DIM:(c + 1) * HEAD_DIM]
            rot = a * cos_ref[...] + pltpu.roll(a, HEAD_DIM // 2, 1) * sin_ref[...]
            if scale != 1.0:
                rot = rot * scale
            o_ref[:, c * HEAD_DIM:(c + 1) * HEAD_DIM] = rot.astype(BF16)

    @pl.when(sec == 0)
    def _():
        o_ref[...] = (acc * (-SCALE_LOG2E)).astype(BF16)

    @pl.when((sec == 1) | (sec == 2) | (sec == 5))
    def _():
        o_ref[...] = acc.astype(BF16)

    @pl.when(sec == 3)
    def _():
        rotary(SCALE_LOG2E)

    @pl.when(sec == 4)
    def _():
        rotary(1.0)


def _proj_call(x2d, gain, w_in, cos, sin, seq, *, tm, tn):
    t, d = x2d.shape
    n = w_in.shape[1]
    assert t % tm == 0 and seq % tm == 0 and n % tn == 0 and SECTION % tn == 0
    rows = min(tm, 256)
    nseq = seq // tm
    return pl.pallas_call(
        functools.partial(_proj_kernel, tn=tn, rows=rows),
        grid=(t // tm, n // tn),
        in_specs=[
            pl.BlockSpec((tm, d), lambda i, j: (i, 0)),
            pl.BlockSpec((1, d), lambda i, j: (0, 0)),
            pl.BlockSpec((d, tn), lambda i, j: (0, j)),
            pl.BlockSpec((tm, HEAD_DIM), lambda i, j: (i % nseq, 0)),
            pl.BlockSpec((tm, HEAD_DIM), lambda i, j: (i % nseq, 0)),
        ],
        out_specs=pl.BlockSpec((tm, tn), lambda i, j: (i, j)),
        out_shape=jax.ShapeDtypeStruct((t, n), BF16),
        scratch_shapes=[pltpu.VMEM((tm, d), BF16)],
        compiler_params=_cparams(("arbitrary", "arbitrary")),
        name="proj",
    )(x2d, gain, w_in, cos, sin)


def _sb_kernel(q_ref, k_ref, v_ref, tri_ref, g_ref, o_ref, acc_ref, *, tb, nsub):
    qi = pl.program_id(2)
    tq = nsub * tb
    lo, hi = slice(0, tb), slice(tb, tq)
    row = lax.broadcasted_iota(jnp.int32, (tb, tb), 0)
    col = lax.broadcasted_iota(jnp.int32, (tb, tb), 1)
    strict = col < row

    def scores(rows, kj):
        start = pl.multiple_of(kj * tb, tb)
        return _dot_nt(q_ref[0, rows, :], k_ref[0, pl.ds(start, tb), :])

    def keep_logs(n, mask):
        lks = []
        for r in range(0, n.shape[0], ROW_CHUNK):
            nc = n[r:r + ROW_CHUNK]
            lk = jnp.minimum(nc, 0.0) - jnp.log2(1.0 + jnp.exp2(-jnp.abs(nc)))
            if mask is not None:
                lk = jnp.where(mask[r:r + ROW_CHUNK], lk, 0.0)
            lks.append(lk.astype(BF16))
        return jnp.concatenate(lks, axis=0)

    def accumulate(rows, kj, n, lk, carry, mask):
        start = pl.multiple_of(kj * tb, tb)
        cum = _dot(lk, tri_ref[...])
        parts = []
        for r in range(0, n.shape[0], ROW_CHUNK):
            chunk = slice(r, r + ROW_CHUNK)
            a = jnp.exp2(cum[chunk] + carry[chunk] - n[chunk])
            if mask is not None:
                a = jnp.where(mask[chunk], a, 0.0)
            parts.append(a.astype(BF16))
        acc_ref[rows, :] += _dot(jnp.concatenate(parts, axis=0), v_ref[0, pl.ds(start, tb), :])
        return carry + cum[:, 0:1]

    def several(rows, kjs, carry, mask=None):
        staged = []
        for kj in kjs:
            n = scores(rows, kj)
            staged.append((kj, n, keep_logs(n, mask)))
        for kj, n, lk in staged:
            carry = accumulate(rows, kj, n, lk, carry, mask)
        return carry

    acc_ref[...] = jnp.zeros_like(acc_ref)
    base = qi * nsub
    zero = jnp.zeros((tb, 1), F32)
    carry_hi = several(hi, [base + 1], zero, strict)
    carry_lo = several(lo, [base], zero, strict)
    carry_hi = several(hi, [base], carry_hi)
    carry = jnp.concatenate([carry_lo, carry_hi], axis=0)
    full = slice(0, tq)

    def live(state):
        p, c = state
        return (p < base // 2) & (jnp.max(c) > UNDERFLOW_LOG2)

    def two_blocks(state):
        p, c = state
        top = base - 1 - 2 * p
        return p + 1, several(full, [top, top - 1], c)

    lax.while_loop(live, two_blocks, (jnp.int32(0), carry))

    o = acc_ref[...]
    ms = jnp.mean(o * o, axis=-1, keepdims=True)
    o_ref[0] = (o * lax.rsqrt(ms + NORM_EPS) * g_ref[...]).astype(BF16)


def _sb_call(proj3d, tri, gain, *, tb, nsub):
    b, s, _ = proj3d.shape
    tq = nsub * tb
    assert s % tq == 0 and nsub == 2 and tb % ROW_CHUNK == 0
    hq, hk, hv = 0, SECTION // HEAD_DIM, 2 * SECTION // HEAD_DIM
    return pl.pallas_call(
        functools.partial(_sb_kernel, tb=tb, nsub=nsub),
        grid=(b, N_SB_HEADS, s // tq),
        in_specs=[
            pl.BlockSpec((1, tq, HEAD_DIM), lambda bi, h, i: (bi, i, hq + h)),
            pl.BlockSpec((1, s, HEAD_DIM), lambda bi, h, i: (bi, 0, hk + h)),
            pl.BlockSpec((1, s, HEAD_DIM), lambda bi, h, i: (bi, 0, hv + h)),
            pl.BlockSpec((tb, tb), lambda bi, h, i: (0, 0)),
            pl.BlockSpec((1, HEAD_DIM), lambda bi, h, i: (0, 0)),
        ],
        out_specs=pl.BlockSpec((1, tq, HEAD_DIM), lambda bi, h, i: (bi, i, h)),
        out_shape=jax.ShapeDtypeStruct((b, s, N_SB_HEADS * HEAD_DIM), BF16),
        scratch_shapes=[pltpu.VMEM((tq, HEAD_DIM), F32)],
        compiler_params=_cparams(("arbitrary", "arbitrary", "arbitrary")),
        name="sb_attn",
    )(proj3d, proj3d, proj3d, tri, gain)


def _diff_kernel(q_ref, k_ref, v_ref, lq1_ref, lk1_ref, lq2_ref, lk2_ref, g_ref, o_ref,
                 acc1_ref, acc2_ref, *, tq, lam_init):
    qi = pl.program_id(2)
    d = HEAD_DIM
    q1 = q_ref[0, :, :d]
    q2 = q_ref[0, :, d:]
    row = lax.broadcasted_iota(jnp.int32, (tq, tq), 0)
    col = lax.broadcasted_iota(jnp.int32, (tq, tq), 1)
    visible = (col // CHUNK) <= (row // CHUNK)

    def softmax_step(s, m, l, acc_ref, v):
        m_new = jnp.maximum(m, jnp.max(s, axis=-1, keepdims=True))
        alpha = jnp.exp2(m - m_new)
        p = jnp.exp2(s - m_new)
        acc_ref[...] = alpha * acc_ref[...] + _dot(p.astype(BF16), v)
        return m_new, alpha * l + jnp.sum(p, axis=-1, keepdims=True)

    def block(start, width, carry, masked):
        m1, l1, m2, l2 = carry
        k = k_ref[0, pl.ds(start, width), :]
        v = v_ref[0, pl.ds(start, width), :]
        s1 = _dot_nt(q1, k[:, :d])
        s2 = _dot_nt(q2, k[:, d:])
        if masked:
            s1 = jnp.where(visible, s1, NEG_INF)
            s2 = jnp.where(visible, s2, NEG_INF)
        m1, l1 = softmax_step(s1, m1, l1, acc1_ref, v)
        m2, l2 = softmax_step(s2, m2, l2, acc2_ref, v)
        return m1, l1, m2, l2

    acc1_ref[...] = jnp.zeros_like(acc1_ref)
    acc2_ref[...] = jnp.zeros_like(acc2_ref)
    neg = jnp.full((tq, 1), NEG_INF, F32)
    zero = jnp.zeros((tq, 1), F32)
    carry = block(pl.multiple_of(qi * tq, tq), tq, (neg, zero, neg, zero), True)
    odd = qi % 2
    carry = lax.cond(odd == 1, lambda c: block(pl.multiple_of((qi - 1) * tq, tq), tq, c, False),
                     lambda c: c, carry)
    _, l1, _, l2 = lax.fori_loop(
        0, qi // 2, lambda i, c: block(pl.multiple_of(2 * i * tq, 2 * tq), 2 * tq, c, False), carry)

    lam = (jnp.exp(jnp.sum(lq1_ref[...] * lk1_ref[...], axis=-1, keepdims=True))
           - jnp.exp(jnp.sum(lq2_ref[...] * lk2_ref[...], axis=-1, keepdims=True)) + lam_init)
    o = acc1_ref[...] / l1 - lam * (acc2_ref[...] / l2)
    ms = jnp.mean(o * o, axis=-1, keepdims=True)
    o_ref[0] = (o * lax.rsqrt(ms + NORM_EPS) * g_ref[...] * (1.0 - lam_init)).astype(BF16)


def _diff_call(proj3d, lq1, lk1, lq2, lk2, gain, lam_init, *, tq):
    b, s, _ = proj3d.shape
    assert s % tq == 0 and tq % CHUNK == 0
    w = 2 * HEAD_DIM
    hq, hk, hv = 3 * SECTION // w, 4 * SECTION // w, 5 * SECTION // w
    vec = pl.BlockSpec((1, HEAD_DIM), lambda bi, h, i: (0, 0))
    return pl.pallas_call(
        functools.partial(_diff_kernel, tq=tq, lam_init=lam_init),
        grid=(b, N_DIFF_HEADS, s // tq),
        in_specs=[
            pl.BlockSpec((1, tq, w), lambda bi, h, i: (bi, i, hq + h)),
            pl.BlockSpec((1, s, w), lambda bi, h, i: (bi, 0, hk + h)),
            pl.BlockSpec((1, s, w), lambda bi, h, i: (bi, 0, hv + h)),
            vec, vec, vec, vec,
            pl.BlockSpec((1, w), lambda bi, h, i: (0, 0)),
        ],
        out_specs=pl.BlockSpec((1, tq, w), lambda bi, h, i: (bi, i, h)),
        out_shape=jax.ShapeDtypeStruct((b, s, N_DIFF_HEADS * w), BF16),
        scratch_shapes=[pltpu.VMEM((tq, w), F32), pltpu.VMEM((tq, w), F32)],
        compiler_params=_cparams(("arbitrary", "arbitrary", "arbitrary")),
        name="diff_attn",
    )(proj3d, proj3d, proj3d, lq1, lk1, lq2, lk2, gain)


def _outproj_kernel(x_ref, sb_ref, df_ref, wo_ref, g_ref, wr_ref, br_ref, tri_ref,
                    x1_ref, h2_ref, ri_ref, rf_ref, cnt_ref):
    i = pl.program_id(0)
    tm = x_ref.shape[0]
    half = sb_ref.shape[1]
    x1 = x_ref[...] + _dot(sb_ref[...], wo_ref[:half, :]) + _dot(df_ref[...], wo_ref[half:, :])
    x1_ref[...] = x1
    ms = jnp.mean(x1 * x1, axis=-1, keepdims=True)
    h2 = x1 * lax.rsqrt(ms + NORM_EPS) * g_ref[...]
    h2_ref[...] = h2

    logits = _dot(h2.astype(BF16), wr_ref[...]) + br_ref[...]
    lane = lax.broadcasted_iota(jnp.int32, (tm, LANES), 1).astype(F32)
    ninf = -jnp.inf

    def first_argmax(vals):
        top = jnp.max(vals, axis=-1, keepdims=True)
        idx = jnp.min(jnp.where(vals == top, lane, float(LANES)), axis=-1, keepdims=True)
        return top, idx

    gl = jnp.where(lane < N_GROUPS, logits, ninf)
    gmax, gidx = first_argmax(gl)
    g_val = 1.0 / jnp.sum(jnp.exp(gl - gmax), axis=-1, keepdims=True)
    lo = N_GROUPS + EXPERTS_PER_GROUP * gidx
    el = jnp.where((lane >= lo) & (lane < lo + EXPERTS_PER_GROUP), logits, ninf)
    l1, i1 = first_argmax(el)
    l2, i2 = first_argmax(jnp.where(lane == i1, ninf, el))
    r = jnp.exp(l2 - l1)
    w1 = g_val / (1.0 + r)
    w2 = g_val * r / (1.0 + r)
    e1 = i1 - N_GROUPS
    e2 = i2 - N_GROUPS

    @pl.when(i == 0)
    def _():
        cnt_ref[...] = jnp.zeros_like(cnt_ref)

    onehot = jnp.where((lane == e1) | (lane == e2), 1.0, 0.0)
    before = _dot(tri_ref[...], onehot.astype(BF16)) + cnt_ref[0:1, :]
    rank1 = jnp.sum(jnp.where(lane == e1, before, 0.0), axis=-1, keepdims=True)
    rank2 = jnp.sum(jnp.where(lane == e2, before, 0.0), axis=-1, keepdims=True)
    cnt_ref[0:1, :] = cnt_ref[0:1, :] + jnp.sum(onehot, axis=0, keepdims=True)

    ri = jnp.where(lane == 0, e1, jnp.where(lane == 1, e2,
                   jnp.where(lane == 2, rank1, jnp.where(lane == 3, rank2, 0.0))))
    ri_ref[...] = ri.astype(jnp.int32)
    rf_ref[...] = jnp.where(lane == 0, w1, jnp.where(lane == 1, w2, 0.0))


def _outproj_call(x2d, sb_out, d_out, w_out, gain, w_router, b_router, tri, *, tm):
    t, d = x2d.shape
    half = sb_out.shape[1]
    assert t % tm == 0
    row = lambda i: (i, 0)
    fixed = lambda i: (0, 0)
    return pl.pallas_call(
        _outproj_kernel,
        grid=(t // tm,),
        in_specs=[
            pl.BlockSpec((tm, d), row),
            pl.BlockSpec((tm, half), row),
            pl.BlockSpec((tm, half), row),
            pl.BlockSpec((2 * half, d), fixed),
            pl.BlockSpec((1, d), fixed),
            pl.BlockSpec((d, LANES), fixed),
            pl.BlockSpec((1, LANES), fixed),
            pl.BlockSpec((tm, tm), fixed),
        ],
        out_specs=[
            pl.BlockSpec((tm, d), row),
            pl.BlockSpec((tm, d), row),
            pl.BlockSpec((tm, LANES), row),
            pl.BlockSpec((tm, LANES), row),
            pl.BlockSpec((8, LANES), fixed),
        ],
        out_shape=[
            jax.ShapeDtypeStruct((t, d), F32),
            jax.ShapeDtypeStruct((t, d), F32),
            jax.ShapeDtypeStruct((t, LANES), jnp.int32),
            jax.ShapeDtypeStruct((t, LANES), F32),
            jax.ShapeDtypeStruct((8, LANES), F32),
        ],
        compiler_params=_cparams(("arbitrary",)),
        name="outproj_router",
    )(x2d, sb_out, d_out, w_out, gain, w_router, b_router, tri)


def _dispatch_kernel(pos_hbm, h_ref, xs_hbm, pos_smem, sem_idx, sem, *, td):
    i = pl.program_id(0)
    idx_copy = pltpu.make_async_copy(pos_hbm.at[i], pos_smem, sem_idx)
    idx_copy.start()
    idx_copy.wait()

    def row_copy(t, k):
        return pltpu.make_async_copy(h_ref.at[pl.ds(t, 1)],
                                     xs_hbm.at[pl.ds(pos_smem[0, k * td + t], 1)], sem)

    def issue(t, c):
        row_copy(t, 0).start()
        row_copy(t, 1).start()
        return c

    lax.fori_loop(0, td, issue, 0, unroll=DMA_UNROLL)

    def drain(t, c):
        row_copy(t, 0).wait()
        row_copy(t, 1).wait()
        return c

    lax.fori_loop(0, td, drain, 0, unroll=DMA_UNROLL)


def _dispatch_call(pos3, h2, n_rows):
    t, d = h2.shape
    td = pos3.shape[2] // 2
    assert pos3.shape[0] * td == t
    return pl.pallas_call(
        functools.partial(_dispatch_kernel, td=td),
        grid=(t // td,),
        in_specs=[pl.BlockSpec(memory_space=pl.ANY), pl.BlockSpec((td, d), lambda i: (i, 0))],
        out_specs=pl.BlockSpec(memory_space=pl.ANY),
        out_shape=jax.ShapeDtypeStruct((n_rows, d), h2.dtype),
        scratch_shapes=[pltpu.SMEM((1, 2 * td), jnp.int32), pltpu.SemaphoreType.DMA,
                        pltpu.SemaphoreType.DMA],
        compiler_params=_cparams(("arbitrary",)),
        name="dispatch",
    )(pos3, h2)


def _experts_kernel(wt_ref, we_ref, nw_ref, lo_ref, hi_ref, xs_ref, wg_ref, wu_ref, wd_ref, ys_ref):
    w = pl.program_id(0)
    tx = xs_ref.shape[0]
    tile = wt_ref[w]
    first = (w == 0) | (wt_ref[jnp.maximum(w - 1, 0)] != tile)

    @pl.when(w < nw_ref[0])
    def _():
        e = we_ref[w]
        rows = tile * tx + lax.broadcasted_iota(jnp.int32, (tx, 1), 0)
        member = (rows >= lo_ref[e]) & (rows < hi_ref[e])
        x = xs_ref[...].astype(BF16)
        gate = _dot(x, wg_ref[0].astype(BF16))
        up = _dot(x, wu_ref[0].astype(BF16))
        hid = gate * (1.0 / (1.0 + jnp.exp(-gate))) * up
        y = _dot(jnp.where(member, hid, 0.0).astype(BF16), wd_ref[0].astype(BF16))

        @pl.when(first)
        def _():
            ys_ref[...] = y

        @pl.when(jnp.logical_not(first))
        def _():
            ys_ref[...] += y


def _experts_call(work_tile, work_expert, n_work, seg_lo, seg_hi, xs, w_gate, w_up, w_down, *, tx):
    p, d = xs.shape
    de = w_gate.shape[2]
    assert p % tx == 0
    n_items = work_tile.shape[0]
    tile = lambda w, wt, we, nw, lo, hi: (wt[w], 0)
    expert = lambda w, wt, we, nw, lo, hi: (we[w], 0, 0)
    return pl.pallas_call(
        _experts_kernel,
        grid_spec=pltpu.PrefetchScalarGridSpec(
            num_scalar_prefetch=5,
            grid=(n_items,),
            in_specs=[
                pl.BlockSpec((tx, d), tile),
                pl.BlockSpec((1, d, de), expert),
                pl.BlockSpec((1, d, de), expert),
                pl.BlockSpec((1, de, d), expert),
            ],
            out_specs=pl.BlockSpec((tx, d), tile),
        ),
        out_shape=jax.ShapeDtypeStruct((p, d), F32),
        compiler_params=_cparams(("arbitrary",)),
        name="experts",
    )(work_tile, work_expert, n_work, seg_lo, seg_hi, xs, w_gate, w_up, w_down)


def _combine_kernel(pos_hbm, x1_ref, rf_ref, g_ref, ys_hbm, o_ref, pos_smem, y0_ref, y1_ref,
                    sem_idx, sem, *, tc):
    i = pl.program_id(0)
    idx_copy = pltpu.make_async_copy(pos_hbm.at[i], pos_smem, sem_idx)
    idx_copy.start()
    idx_copy.wait()

    def row_copy(t, k):
        dst = y0_ref if k == 0 else y1_ref
        return pltpu.make_async_copy(ys_hbm.at[pl.ds(pos_smem[0, k * tc + t], 1)],
                                     dst.at[pl.ds(t, 1)], sem)

    def issue(t, c):
        row_copy(t, 0).start()
        row_copy(t, 1).start()
        return c

    lax.fori_loop(0, tc, issue, 0, unroll=DMA_UNROLL)

    def drain(t, c):
        row_copy(t, 0).wait()
        row_copy(t, 1).wait()
        return c

    lax.fori_loop(0, tc, drain, 0, unroll=DMA_UNROLL)

    x = x1_ref[...] + rf_ref[:, 0:1] * y0_ref[...] + rf_ref[:, 1:2] * y1_ref[...]
    ms = jnp.mean(x * x, axis=-1, keepdims=True)
    o_ref[...] = x * lax.rsqrt(ms + NORM_EPS) * g_ref[...]


def _combine_call(pos3, x1, rf, gain, ys):
    t, d = x1.shape
    tc = pos3.shape[2] // 2
    assert pos3.shape[0] * tc == t
    row = lambda i: (i, 0)
    return pl.pallas_call(
        functools.partial(_combine_kernel, tc=tc),
        grid=(t // tc,),
        in_specs=[
            pl.BlockSpec(memory_space=pl.ANY),
            pl.BlockSpec((tc, d), row),
            pl.BlockSpec((tc, LANES), row),
            pl.BlockSpec((1, d), lambda i: (0, 0)),
            pl.BlockSpec(memory_space=pl.ANY),
        ],
        out_specs=pl.BlockSpec((tc, d), row),
        out_shape=jax.ShapeDtypeStruct((t, d), F32),
        scratch_shapes=[pltpu.SMEM((1, 2 * tc), jnp.int32), pltpu.VMEM((tc, d), F32),
                        pltpu.VMEM((tc, d), F32), pltpu.SemaphoreType.DMA,
                        pltpu.SemaphoreType.DMA],
        compiler_params=_cparams(("arbitrary",)),
        name="combine",
    )(pos3, x1, rf, gain, ys)


def _pick(n, pref):
    while n % pref:
        pref //= 2
    return pref


def _layer(x2d, b, s, layer, attn_norm_gain, w_in, sb_norm_gain, lq1, lk1, lq2, lk2, subln_gain,
           w_out, ffn_norm_gain, w_gr, b_gr, w_er, b_er, w_gate, w_up, w_down):
    t, d = x2d.shape
    tb = _pick(s, 256)
    tm_proj = _pick(s, 1024)
    tm_out = _pick(t, 256)
    tx = 256

    half = HEAD_DIM // 2
    inv_freq = 1.0 / (ROPE_THETA ** (np.arange(half, dtype=np.float64) / half))
    ang = np.arange(s, dtype=np.float64)[:, None] * inv_freq[None, :]
    cos = jnp.asarray(np.concatenate([np.cos(ang), np.cos(ang)], axis=-1), F32)
    sin = jnp.asarray(np.concatenate([-np.sin(ang), np.sin(ang)], axis=-1), F32)

    proj = _proj_call(x2d, attn_norm_gain.reshape(1, d), w_in.astype(BF16), cos, sin, s,
                      tm=tm_proj, tn=512)
    proj3d = proj.reshape(b, s, proj.shape[1])

    r = jnp.arange(tb)
    tri_suffix = (r[:, None] >= r[None, :]).astype(BF16)
    sb_out = _sb_call(proj3d, tri_suffix, sb_norm_gain.reshape(1, HEAD_DIM), tb=tb, nsub=2)

    lam_init = 0.8 - 0.6 * math.exp(-0.3 * layer)
    d_out = _diff_call(proj3d, lq1.reshape(1, -1), lk1.reshape(1, -1), lq2.reshape(1, -1),
                       lk2.reshape(1, -1), subln_gain.reshape(1, -1), lam_init, tq=2 * tb)

    w_router = jnp.zeros((d, LANES), F32)
    w_router = w_router.at[:, :N_GROUPS].set(w_gr).at[:, N_GROUPS:N_GROUPS + N_EXPERTS].set(w_er)
    b_router = jnp.zeros((1, LANES), F32)
    b_router = b_router.at[0, :N_GROUPS].set(b_gr).at[0, N_GROUPS:N_GROUPS + N_EXPERTS].set(b_er)
    rr = jnp.arange(tm_out)
    tri_before = (rr[None, :] < rr[:, None]).astype(BF16)
    x1, h2, ri, rf, cnt = _outproj_call(
        x2d, sb_out.reshape(t, -1), d_out.reshape(t, -1), w_out.astype(BF16),
        ffn_norm_gain.reshape(1, d), w_router.astype(BF16), b_router, tri_before, tm=tm_out)

    counts = cnt[0, :N_EXPERTS].astype(jnp.int32)
    seg_hi = jnp.cumsum(counts)
    seg_lo = seg_hi - counts
    experts = jnp.arange(N_EXPERTS, dtype=jnp.int32)

    def sorted_position(slot):
        hit = ri[:, slot][:, None] == experts[None, :]
        return jnp.sum(jnp.where(hit, seg_lo[None, :], 0), axis=1) + ri[:, 2 + slot]

    pos = jnp.stack([sorted_position(0).reshape(-1, tm_out), sorted_position(1).reshape(-1, tm_out)],
                    axis=1).reshape(-1, 1, 2 * tm_out)
    first_tile = seg_lo // tx
    items = jnp.where(counts > 0, (seg_hi - 1) // tx - first_tile + 1, 0)
    item_hi = jnp.cumsum(items)
    n_work = item_hi[-1]
    w = jnp.minimum(jnp.arange(2 * t // tx + N_EXPERTS - 1, dtype=jnp.int32), n_work - 1)
    work_expert = jnp.sum((item_hi[None, :] <= w[:, None]).astype(jnp.int32), axis=1)
    work_tile = (first_tile[work_expert] + w - (item_hi - items)[work_expert]).astype(jnp.int32)

    xs = _dispatch_call(pos, h2, 2 * t)
    ys = _experts_call(work_tile, work_expert, n_work.reshape(1).astype(jnp.int32), seg_lo, seg_hi,
                       xs, w_gate, w_up, w_down, tx=tx)
    return x1, pos, rf, ys


def kernel(x, attn_norm_gain, w_in, sb_norm_gain, diff_lambda_q1, diff_lambda_k1, diff_lambda_q2,
           diff_lambda_k2, diff_subln_gain, w_out, ffn_norm_gain, w_group_router, b_group_router,
           w_expert_router, b_expert_router, w_gate, w_up, w_down, final_norm_gain):
    b, s, d = x.shape
    assert w_in.shape[0] == 1, "the combine stage fuses the final norm: single-layer stacks only"
    layer = 0
    x2d = x.reshape(b * s, d)
    x1, pos, rf, ys = _layer(
        x2d, b, s, layer, attn_norm_gain[layer], w_in[layer], sb_norm_gain[layer],
        diff_lambda_q1[layer], diff_lambda_k1[layer], diff_lambda_q2[layer],
        diff_lambda_k2[layer], diff_subln_gain[layer], w_out[layer], ffn_norm_gain[layer],
        w_group_router[layer], b_group_router[layer], w_expert_router[layer],
        b_expert_router[layer], w_gate[layer], w_up[layer], w_down[layer])
    out = _combine_call(pos, x1, rf, final_norm_gain.reshape(1, d), ys)
    return out.reshape(b, s, d)
```

```python
import functools
import math

import jax
import jax.numpy as jnp
import numpy as np
from jax import lax
from jax.experimental import pallas as pl
from jax.experimental.pallas import tpu as pltpu

F32 = jnp.float32
BF16 = jnp.bfloat16

HEAD_DIM = 128
N_SB_HEADS = 8
N_DIFF_HEADS = 4
SECTION = 1024
CHUNK = 64
ROPE_THETA = 10000.0
N_GROUPS = 4
EXPERTS_PER_GROUP = 8
N_EXPERTS = N_GROUPS * EXPERTS_PER_GROUP
NORM_EPS = 1e-6
NEG_INF = -1e30
LANES = 128
ROW_CHUNK = 32
UNDERFLOW_LOG2 = -200.0
DMA_UNROLL = 8
SCALE_LOG2E = math.log2(math.e) / math.sqrt(HEAD_DIM)

VMEM_LIMIT = 56 * 1024 * 1024


def _cparams(sem):
    return pltpu.CompilerParams(dimension_semantics=sem, vmem_limit_bytes=VMEM_LIMIT)


def _dot(a, b):
    return jnp.dot(a, b, preferred_element_type=F32)


def _dot_nt(a, b):
    return lax.dot_general(a, b, (((1,), (1,)), ((), ())), preferred_element_type=F32)


def _proj_kernel(x_ref, g_ref, w_ref, cos_ref, sin_ref, o_ref, h_ref, *, tn, rows):
    j = pl.program_id(1)
    tm = x_ref.shape[0]

    @pl.when(j == 0)
    def _():
        for r in range(0, tm, rows):
            x = x_ref[r:r + rows, :]
            ms = jnp.mean(x * x, axis=-1, keepdims=True)
            h_ref[r:r + rows, :] = (x * lax.rsqrt(ms + NORM_EPS) * g_ref[...]).astype(BF16)

    acc = _dot(h_ref[...], w_ref[...])
    sec = j // (SECTION // tn)

    def rotary(scale):
        for c in range(tn // HEAD_DIM):
            a = acc[:, c * HEAD_DIM:(c + 1) * HEAD_DIM]
            rot = a * cos_ref[...] + pltpu.roll(a, HEAD_DIM // 2, 1) * sin_ref[...]
            if scale != 1.0:
                rot = rot * scale
            o_ref[:, c * HEAD_DIM:(c + 1) * HEAD_DIM] = rot.astype(BF16)

    @pl.when(sec == 0)
    def _():
        o_ref[...] = (acc * (-SCALE_LOG2E)).astype(BF16)

    @pl.when((sec == 1) | (sec == 2) | (sec == 5))
    def _():
        o_ref[...] = acc.astype(BF16)

    @pl.when(sec == 3)
    def _():
        rotary(SCALE_LOG2E)

    @pl.when(sec == 4)
    def _():
        rotary(1.0)


def _proj_call(x2d, gain, w_in, cos, sin, seq, *, tm, tn):
    t, d = x2d.shape
    n = w_in.shape[1]
    assert t % tm == 0 and seq % tm == 0 and n % tn == 0 and SECTION % tn == 0
    rows = min(tm, 256)
    nseq = seq // tm
    return pl.pallas_call(
        functools.partial(_proj_kernel, tn=tn, rows=rows),
        grid=(t // tm, n // tn),
        in_specs=[
            pl.BlockSpec((tm, d), lambda i, j: (i, 0)),
            pl.BlockSpec((1, d), lambda i, j: (0, 0)),
            pl.BlockSpec((d, tn), lambda i, j: (0, j)),
            pl.BlockSpec((tm, HEAD_DIM), lambda i, j: (i % nseq, 0)),
            pl.BlockSpec((tm, HEAD_DIM), lambda i, j: (i % nseq, 0)),
        ],
        out_specs=pl.BlockSpec((tm, tn), lambda i, j: (i, j)),
        out_shape=jax.ShapeDtypeStruct((t, n), BF16),
        scratch_shapes=[pltpu.VMEM((tm, d), BF16)],
        compiler_params=_cparams(("arbitrary", "arbitrary")),
        name="proj",
    )(x2d, gain, w_in, cos, sin)


def _sb_kernel(q_ref, k_ref, v_ref, tri_ref, g_ref, o_ref, acc_ref, *, tb, nsub):
    qi = pl.program_id(2)
    tq = nsub * tb
    lo, hi = slice(0, tb), slice(tb, tq)
    row = lax.broadcasted_iota(jnp.int32, (tb, tb), 0)
    col = lax.broadcasted_iota(jnp.int32, (tb, tb), 1)
    strict = col < row

    def scores(rows, kj):
        start = pl.multiple_of(kj * tb, tb)
        return _dot_nt(q_ref[0, rows, :], k_ref[0, pl.ds(start, tb), :])

    def keep_logs(n, mask):
        lks = []
        for r in range(0, n.shape[0], ROW_CHUNK):
            nc = n[r:r + ROW_CHUNK]
            lk = jnp.minimum(nc, 0.0) - jnp.log2(1.0 + jnp.exp2(-jnp.abs(nc)))
            if mask is not None:
                lk = jnp.where(mask[r:r + ROW_CHUNK], lk, 0.0)
            lks.append(lk.astype(BF16))
        return jnp.concatenate(lks, axis=0)

    def accumulate(rows, kj, n, lk, carry, mask):
        start = pl.multiple_of(kj * tb, tb)
        cum = _dot(lk, tri_ref[...])
        parts = []
        for r in range(0, n.shape[0], ROW_CHUNK):
            chunk = slice(r, r + ROW_CHUNK)
            a = jnp.exp2(cum[chunk] + carry[chunk] - n[chunk])
            if mask is not None:
                a = jnp.where(mask[chunk], a, 0.0)
            parts.append(a.astype(BF16))
        acc_ref[rows, :] += _dot(jnp.concatenate(parts, axis=0), v_ref[0, pl.ds(start, tb), :])
        return carry + cum[:, 0:1]

    def several(rows, kjs, carry, mask=None):
        staged = []
        for kj in kjs:
            n = scores(rows, kj)
            staged.append((kj, n, keep_logs(n, mask)))
        for kj, n, lk in staged:
            carry = accumulate(rows, kj, n, lk, carry, mask)
        return carry

    acc_ref[...] = jnp.zeros_like(acc_ref)
    base = qi * nsub
    zero = jnp.zeros((tb, 1), F32)
    prev = jnp.maximum(base - 1, 0)
    has_prev = jnp.broadcast_to(base > 0, (tb, tb))
    chains = [(1, base + 1, strict), (0, base, strict), (1, base, None), (0, prev, has_prev)]
    halves = [lo, hi]
    staged = []
    for half, kj, mask in chains:
        n = scores(halves[half], kj)
        staged.append((n, keep_logs(n, mask)))
    carry = [zero, zero]
    for (half, kj, mask), (n, lk) in zip(chains, staged):
        carry[half] = accumulate(halves[half], kj, n, lk, carry[half], mask)

    def walk_back(rows, first, carry):
        def live(state):
            kj, c = state
            return (kj >= 0) & (jnp.max(c) > UNDERFLOW_LOG2)

        def one_block(state):
            kj, c = state
            return kj - 1, several(rows, [kj], c)

        lax.while_loop(live, one_block, (first, carry))

    walk_back(hi, base - 1, carry[1])
    walk_back(lo, base - 2, carry[0])

    o = acc_ref[...]
    ms = jnp.mean(o * o, axis=-1, keepdims=True)
    o_ref[0] = (o * lax.rsqrt(ms + NORM_EPS) * g_ref[...]).astype(BF16)


def _sb_call(proj3d, tri, gain, *, tb, nsub):
    b, s, _ = proj3d.shape
    tq = nsub * tb
    assert s % tq == 0 and nsub == 2 and tb % ROW_CHUNK == 0
    hq, hk, hv = 0, SECTION // HEAD_DIM, 2 * SECTION // HEAD_DIM
    return pl.pallas_call(
        functools.partial(_sb_kernel, tb=tb, nsub=nsub),
        grid=(b, N_SB_HEADS, s // tq),
        in_specs=[
            pl.BlockSpec((1, tq, HEAD_DIM), lambda bi, h, i: (bi, i, hq + h)),
            pl.BlockSpec((1, s, HEAD_DIM), lambda bi, h, i: (bi, 0, hk + h)),
            pl.BlockSpec((1, s, HEAD_DIM), lambda bi, h, i: (bi, 0, hv + h)),
            pl.BlockSpec((tb, tb), lambda bi, h, i: (0, 0)),
            pl.BlockSpec((1, HEAD_DIM), lambda bi, h, i: (0, 0)),
        ],
        out_specs=pl.BlockSpec((1, tq, HEAD_DIM), lambda bi, h, i: (bi, i, h)),
        out_shape=jax.ShapeDtypeStruct((b, s, N_SB_HEADS * HEAD_DIM), BF16),
        scratch_shapes=[pltpu.VMEM((tq, HEAD_DIM), F32)],
        compiler_params=_cparams(("arbitrary", "arbitrary", "arbitrary")),
        name="sb_attn",
    )(proj3d, proj3d, proj3d, tri, gain)


def _diff_kernel(q_ref, k_ref, v_ref, lq1_ref, lk1_ref, lq2_ref, lk2_ref, g_ref, o_ref,
                 acc1_ref, acc2_ref, *, tq, lam_init):
    qi = pl.program_id(2)
    d = HEAD_DIM
    q1 = q_ref[0, :, :d]
    q2 = q_ref[0, :, d:]
    row = lax.broadcasted_iota(jnp.int32, (tq, tq), 0)
    col = lax.broadcasted_iota(jnp.int32, (tq, tq), 1)
    visible = (col // CHUNK) <= (row // CHUNK)

    def probabilities(s, m, l):
        m_new = jnp.maximum(m, jnp.max(s, axis=-1, keepdims=True))
        alpha = jnp.exp2(m - m_new)
        p = jnp.exp2(s - m_new)
        return p.astype(BF16), alpha, m_new, alpha * l + jnp.sum(p, axis=-1, keepdims=True)

    def block(start, width, carry, masked):
        m1, l1, m2, l2 = carry
        k = k_ref[0, pl.ds(start, width), :]
        v = v_ref[0, pl.ds(start, width), :]
        s1 = _dot_nt(q1, k[:, :d])
        s2 = _dot_nt(q2, k[:, d:])
        if masked:
            s1 = jnp.where(visible, s1, NEG_INF)
            s2 = jnp.where(visible, s2, NEG_INF)
        p1, alpha1, m1, l1 = probabilities(s1, m1, l1)
        p2, alpha2, m2, l2 = probabilities(s2, m2, l2)
        pv = _dot(jnp.concatenate([p1, p2], axis=0), v)
        acc1_ref[...] = alpha1 * acc1_ref[...] + pv[:tq]
        acc2_ref[...] = alpha2 * acc2_ref[...] + pv[tq:]
        return m1, l1, m2, l2

    acc1_ref[...] = jnp.zeros_like(acc1_ref)
    acc2_ref[...] = jnp.zeros_like(acc2_ref)
    neg = jnp.full((tq, 1), NEG_INF, F32)
    zero = jnp.zeros((tq, 1), F32)
    carry = block(pl.multiple_of(qi * tq, tq), tq, (neg, zero, neg, zero), True)
    odd = qi % 2
    carry = lax.cond(odd == 1, lambda c: block(pl.multiple_of((qi - 1) * tq, tq), tq, c, False),
                     lambda c: c, carry)
    _, l1, _, l2 = lax.fori_loop(
        0, qi // 2, lambda i, c: block(pl.multiple_of(2 * i * tq, 2 * tq), 2 * tq, c, False), carry)

    lam = (jnp.exp(jnp.sum(lq1_ref[...] * lk1_ref[...], axis=-1, keepdims=True))
           - jnp.exp(jnp.sum(lq2_ref[...] * lk2_ref[...], axis=-1, keepdims=True)) + lam_init)
    o = acc1_ref[...] / l1 - lam * (acc2_ref[...] / l2)
    ms = jnp.mean(o * o, axis=-1, keepdims=True)
    o_ref[0] = (o * lax.rsqrt(ms + NORM_EPS) * g_ref[...] * (1.0 - lam_init)).astype(BF16)


def _diff_call(proj3d, lq1, lk1, lq2, lk2, gain, lam_init, *, tq):
    b, s, _ = proj3d.shape
    assert s % tq == 0 and tq % CHUNK == 0
    w = 2 * HEAD_DIM
    hq, hk, hv = 3 * SECTION // w, 4 * SECTION // w, 5 * SECTION // w
    vec = pl.BlockSpec((1, HEAD_DIM), lambda bi, h, i: (0, 0))
    return pl.pallas_call(
        functools.partial(_diff_kernel, tq=tq, lam_init=lam_init),
        grid=(b, N_DIFF_HEADS, s // tq),
        in_specs=[
            pl.BlockSpec((1, tq, w), lambda bi, h, i: (bi, i, hq + h)),
            pl.BlockSpec((1, s, w), lambda bi, h, i: (bi, 0, hk + h)),
            pl.BlockSpec((1, s, w), lambda bi, h, i: (bi, 0, hv + h)),
            vec, vec, vec, vec,
            pl.BlockSpec((1, w), lambda bi, h, i: (0, 0)),
        ],
        out_specs=pl.BlockSpec((1, tq, w), lambda bi, h, i: (bi, i, h)),
        out_shape=jax.ShapeDtypeStruct((b, s, N_DIFF_HEADS * w), BF16),
        scratch_shapes=[pltpu.VMEM((tq, w), F32), pltpu.VMEM((tq, w), F32)],
        compiler_params=_cparams(("arbitrary", "arbitrary", "arbitrary")),
        name="diff_attn",
    )(proj3d, proj3d, proj3d, lq1, lk1, lq2, lk2, gain)


def _outproj_kernel(x_ref, sb_ref, df_ref, wo_ref, g_ref, wr_ref, br_ref, tri_ref,
                    x1_ref, h2_ref, ri_ref, rf_ref, cnt_ref):
    i = pl.program_id(0)
    tm = x_ref.shape[0]
    half = sb_ref.shape[1]
    x1 = x_ref[...] + _dot(sb_ref[...], wo_ref[:half, :]) + _dot(df_ref[...], wo_ref[half:, :])
    x1_ref[...] = x1
    ms = jnp.mean(x1 * x1, axis=-1, keepdims=True)
    h2 = x1 * lax.rsqrt(ms + NORM_EPS) * g_ref[...]
    h2_ref[...] = h2

    logits = _dot(h2.astype(BF16), wr_ref[...]) + br_ref[...]
    lane = lax.broadcasted_iota(jnp.int32, (tm, LANES), 1).astype(F32)
    ninf = -jnp.inf

    def first_argmax(vals):
        top = jnp.max(vals, axis=-1, keepdims=True)
        idx = jnp.min(jnp.where(vals == top, lane, float(LANES)), axis=-1, keepdims=True)
        return top, idx

    gl = jnp.where(lane < N_GROUPS, logits, ninf)
    gmax, gidx = first_argmax(gl)
    g_val = 1.0 / jnp.sum(jnp.exp(gl - gmax), axis=-1, keepdims=True)
    lo = N_GROUPS + EXPERTS_PER_GROUP * gidx
    el = jnp.where((lane >= lo) & (lane < lo + EXPERTS_PER_GROUP), logits, ninf)
    l1, i1 = first_argmax(el)
    l2, i2 = first_argmax(jnp.where(lane == i1, ninf, el))
    r = jnp.exp(l2 - l1)
    w1 = g_val / (1.0 + r)
    w2 = g_val * r / (1.0 + r)
    e1 = i1 - N_GROUPS
    e2 = i2 - N_GROUPS

    @pl.when(i == 0)
    def _():
        cnt_ref[...] = jnp.zeros_like(cnt_ref)

    onehot = jnp.where((lane == e1) | (lane == e2), 1.0, 0.0)
    before = _dot(tri_ref[...], onehot.astype(BF16)) + cnt_ref[0:1, :]
    rank1 = jnp.sum(jnp.where(lane == e1, before, 0.0), axis=-1, keepdims=True)
    rank2 = jnp.sum(jnp.where(lane == e2, before, 0.0), axis=-1, keepdims=True)
    cnt_ref[0:1, :] = cnt_ref[0:1, :] + jnp.sum(onehot, axis=0, keepdims=True)

    ri = jnp.where(lane == 0, e1, jnp.where(lane == 1, e2,
                   jnp.where(lane == 2, rank1, jnp.where(lane == 3, rank2, 0.0))))
    ri_ref[...] = ri.astype(jnp.int32)
    rf_ref[...] = jnp.where(lane == 0, w1, jnp.where(lane == 1, w2, 0.0))


def _outproj_call(x2d, sb_out, d_out, w_out, gain, w_router, b_router, tri, *, tm):
    t, d = x2d.shape
    half = sb_out.shape[1]
    assert t % tm == 0
    row = lambda i: (i, 0)
    fixed = lambda i: (0, 0)
    return pl.pallas_call(
        _outproj_kernel,
        grid=(t // tm,),
        in_specs=[
            pl.BlockSpec((tm, d), row),
            pl.BlockSpec((tm, half), row),
            pl.BlockSpec((tm, half), row),
            pl.BlockSpec((2 * half, d), fixed),
            pl.BlockSpec((1, d), fixed),
            pl.BlockSpec((d, LANES), fixed),
            pl.BlockSpec((1, LANES), fixed),
            pl.BlockSpec((tm, tm), fixed),
        ],
        out_specs=[
            pl.BlockSpec((tm, d), row),
            pl.BlockSpec((tm, d), row),
            pl.BlockSpec((tm, LANES), row),
            pl.BlockSpec((tm, LANES), row),
            pl.BlockSpec((8, LANES), fixed),
        ],
        out_shape=[
            jax.ShapeDtypeStruct((t, d), F32),
            jax.ShapeDtypeStruct((t, d), F32),
            jax.ShapeDtypeStruct((t, LANES), jnp.int32),
            jax.ShapeDtypeStruct((t, LANES), F32),
            jax.ShapeDtypeStruct((8, LANES), F32),
        ],
        compiler_params=_cparams(("arbitrary",)),
        name="outproj_router",
    )(x2d, sb_out, d_out, w_out, gain, w_router, b_router, tri)


def _dispatch_kernel(pos_hbm, h_ref, xs_hbm, pos_smem, sem_idx, sem, *, td):
    i = pl.program_id(0)
    idx_copy = pltpu.make_async_copy(pos_hbm.at[i], pos_smem, sem_idx)
    idx_copy.start()
    idx_copy.wait()

    def row_copy(t, k):
        return pltpu.make_async_copy(h_ref.at[pl.ds(t, 1)],
                                     xs_hbm.at[pl.ds(pos_smem[0, k * td + t], 1)], sem)

    def issue(t, c):
        row_copy(t, 0).start()
        row_copy(t, 1).start()
        return c

    lax.fori_loop(0, td, issue, 0, unroll=DMA_UNROLL)

    def drain(t, c):
        row_copy(t, 0).wait()
        row_copy(t, 1).wait()
        return c

    lax.fori_loop(0, td, drain, 0, unroll=DMA_UNROLL)


def _dispatch_call(pos3, h2, n_rows):
    t, d = h2.shape
    td = pos3.shape[2] // 2
    assert pos3.shape[0] * td == t
    return pl.pallas_call(
        functools.partial(_dispatch_kernel, td=td),
        grid=(t // td,),
        in_specs=[pl.BlockSpec(memory_space=pl.ANY), pl.BlockSpec((td, d), lambda i: (i, 0))],
        out_specs=pl.BlockSpec(memory_space=pl.ANY),
        out_shape=jax.ShapeDtypeStruct((n_rows, d), h2.dtype),
        scratch_shapes=[pltpu.SMEM((1, 2 * td), jnp.int32), pltpu.SemaphoreType.DMA,
                        pltpu.SemaphoreType.DMA],
        compiler_params=_cparams(("arbitrary",)),
        name="dispatch",
    )(pos3, h2)


def _experts_kernel(wt_ref, we_ref, nw_ref, lo_ref, hi_ref, xs_ref, wg_ref, wu_ref, wd_ref, ys_ref):
    w = pl.program_id(0)
    tx = xs_ref.shape[0]
    tile = wt_ref[w]
    first = (w == 0) | (wt_ref[jnp.maximum(w - 1, 0)] != tile)

    @pl.when(w < nw_ref[0])
    def _():
        e = we_ref[w]
        rows = tile * tx + lax.broadcasted_iota(jnp.int32, (tx, 1), 0)
        member = (rows >= lo_ref[e]) & (rows < hi_ref[e])
        x = xs_ref[...].astype(BF16)
        gate = _dot(x, wg_ref[0].astype(BF16))
        up = _dot(x, wu_ref[0].astype(BF16))
        hid = gate * (1.0 / (1.0 + jnp.exp(-gate))) * up
        y = _dot(jnp.where(member, hid, 0.0).astype(BF16), wd_ref[0].astype(BF16))

        @pl.when(first)
        def _():
            ys_ref[...] = y

        @pl.when(jnp.logical_not(first))
        def _():
            ys_ref[...] += y


def _experts_call(work_tile, work_expert, n_work, seg_lo, seg_hi, xs, w_gate, w_up, w_down, *, tx):
    p, d = xs.shape
    de = w_gate.shape[2]
    assert p % tx == 0
    n_items = work_tile.shape[0]
    tile = lambda w, wt, we, nw, lo, hi: (wt[w], 0)
    expert = lambda w, wt, we, nw, lo, hi: (we[w], 0, 0)
    return pl.pallas_call(
        _experts_kernel,
        grid_spec=pltpu.PrefetchScalarGridSpec(
            num_scalar_prefetch=5,
            grid=(n_items,),
            in_specs=[
                pl.BlockSpec((tx, d), tile),
                pl.BlockSpec((1, d, de), expert),
                pl.BlockSpec((1, d, de), expert),
                pl.BlockSpec((1, de, d), expert),
            ],
            out_specs=pl.BlockSpec((tx, d), tile),
        ),
        out_shape=jax.ShapeDtypeStruct((p, d), F32),
        compiler_params=_cparams(("arbitrary",)),
        name="experts",
    )(work_tile, work_expert, n_work, seg_lo, seg_hi, xs, w_gate, w_up, w_down)


def _combine_kernel(pos_hbm, x1_ref, rf_ref, g_ref, ys_hbm, o_ref, pos_smem, y0_ref, y1_ref,
                    sem_idx, sem, *, tc):
    i = pl.program_id(0)
    idx_copy = pltpu.make_async_copy(pos_hbm.at[i], pos_smem, sem_idx)
    idx_copy.start()
    idx_copy.wait()

    def row_copy(t, k):
        dst = y0_ref if k == 0 else y1_ref
        return pltpu.make_async_copy(ys_hbm.at[pl.ds(pos_smem[0, k * tc + t], 1)],
                                     dst.at[pl.ds(t, 1)], sem)

    def issue(t, c):
        row_copy(t, 0).start()
        row_copy(t, 1).start()
        return c

    lax.fori_loop(0, tc, issue, 0, unroll=DMA_UNROLL)

    def drain(t, c):
        row_copy(t, 0).wait()
        row_copy(t, 1).wait()
        return c

    lax.fori_loop(0, tc, drain, 0, unroll=DMA_UNROLL)

    x = x1_ref[...] + rf_ref[:, 0:1] * y0_ref[...] + rf_ref[:, 1:2] * y1_ref[...]
    ms = jnp.mean(x * x, axis=-1, keepdims=True)
    o_ref[...] = x * lax.rsqrt(ms + NORM_EPS) * g_ref[...]


def _combine_call(pos3, x1, rf, gain, ys):
    t, d = x1.shape
    tc = pos3.shape[2] // 2
    assert pos3.shape[0] * tc == t
    row = lambda i: (i, 0)
    return pl.pallas_call(
        functools.partial(_combine_kernel, tc=tc),
        grid=(t // tc,),
        in_specs=[
            pl.BlockSpec(memory_space=pl.ANY),
            pl.BlockSpec((tc, d), row),
            pl.BlockSpec((tc, LANES), row),
            pl.BlockSpec((1, d), lambda i: (0, 0)),
            pl.BlockSpec(memory_space=pl.ANY),
        ],
        out_specs=pl.BlockSpec((tc, d), row),
        out_shape=jax.ShapeDtypeStruct((t, d), F32),
        scratch_shapes=[pltpu.SMEM((1, 2 * tc), jnp.int32), pltpu.VMEM((tc, d), F32),
                        pltpu.VMEM((tc, d), F32), pltpu.SemaphoreType.DMA,
                        pltpu.SemaphoreType.DMA],
        compiler_params=_cparams(("arbitrary",)),
        name="combine",
    )(pos3, x1, rf, gain, ys)


def _pick(n, pref):
    while n % pref:
        pref //= 2
    return pref


def _layer(x2d, b, s, layer, attn_norm_gain, w_in, sb_norm_gain, lq1, lk1, lq2, lk2, subln_gain,
           w_out, ffn_norm_gain, w_gr, b_gr, w_er, b_er, w_gate, w_up, w_down):
    t, d = x2d.shape
    tb = _pick(s, 256)
    tm_proj = _pick(s, 1024)
    tm_out = _pick(t, 256)
    tx = 256

    half = HEAD_DIM // 2
    inv_freq = 1.0 / (ROPE_THETA ** (np.arange(half, dtype=np.float64) / half))
    ang = np.arange(s, dtype=np.float64)[:, None] * inv_freq[None, :]
    cos = jnp.asarray(np.concatenate([np.cos(ang), np.cos(ang)], axis=-1), F32)
    sin = jnp.asarray(np.concatenate([-np.sin(ang), np.sin(ang)], axis=-1), F32)

    proj = _proj_call(x2d, attn_norm_gain.reshape(1, d), w_in.astype(BF16), cos, sin, s,
                      tm=tm_proj, tn=512)
    proj3d = proj.reshape(b, s, proj.shape[1])

    r = jnp.arange(tb)
    tri_suffix = (r[:, None] >= r[None, :]).astype(BF16)
    sb_out = _sb_call(proj3d, tri_suffix, sb_norm_gain.reshape(1, HEAD_DIM), tb=tb, nsub=2)

    lam_init = 0.8 - 0.6 * math.exp(-0.3 * layer)
    d_out = _diff_call(proj3d, lq1.reshape(1, -1), lk1.reshape(1, -1), lq2.reshape(1, -1),
                       lk2.reshape(1, -1), subln_gain.reshape(1, -1), lam_init, tq=2 * tb)

    w_router = jnp.zeros((d, LANES), F32)
    w_router = w_router.at[:, :N_GROUPS].set(w_gr).at[:, N_GROUPS:N_GROUPS + N_EXPERTS].set(w_er)
    b_router = jnp.zeros((1, LANES), F32)
    b_router = b_router.at[0, :N_GROUPS].set(b_gr).at[0, N_GROUPS:N_GROUPS + N_EXPERTS].set(b_er)
    rr = jnp.arange(tm_out)
    tri_before = (rr[None, :] < rr[:, None]).astype(BF16)
    x1, h2, ri, rf, cnt = _outproj_call(
        x2d, sb_out.reshape(t, -1), d_out.reshape(t, -1), w_out.astype(BF16),
        ffn_norm_gain.reshape(1, d), w_router.astype(BF16), b_router, tri_before, tm=tm_out)

    counts = cnt[0, :N_EXPERTS].astype(jnp.int32)
    seg_hi = jnp.cumsum(counts)
    seg_lo = seg_hi - counts
    experts = jnp.arange(N_EXPERTS, dtype=jnp.int32)

    def sorted_position(slot):
        hit = ri[:, slot][:, None] == experts[None, :]
        return jnp.sum(jnp.where(hit, seg_lo[None, :], 0), axis=1) + ri[:, 2 + slot]

    pos = jnp.stack([sorted_position(0).reshape(-1, tm_out), sorted_position(1).reshape(-1, tm_out)],
                    axis=1).reshape(-1, 1, 2 * tm_out)
    first_tile = seg_lo // tx
    items = jnp.where(counts > 0, (seg_hi - 1) // tx - first_tile + 1, 0)
    item_hi = jnp.cumsum(items)
    n_work = item_hi[-1]
    w = jnp.minimum(jnp.arange(2 * t // tx + N_EXPERTS - 1, dtype=jnp.int32), n_work - 1)
    work_expert = jnp.sum((item_hi[None, :] <= w[:, None]).astype(jnp.int32), axis=1)
    work_tile = (first_tile[work_expert] + w - (item_hi - items)[work_expert]).astype(jnp.int32)

    xs = _dispatch_call(pos, h2, 2 * t)
    ys = _experts_call(work_tile, work_expert, n_work.reshape(1).astype(jnp.int32), seg_lo, seg_hi,
                       xs, w_gate, w_up, w_down, tx=tx)
    return x1, pos, rf, ys


def kernel(x, attn_norm_gain, w_in, sb_norm_gain, diff_lambda_q1, diff_lambda_k1, diff_lambda_q2,
           diff_lambda_k2, diff_subln_gain, w_out, ffn_norm_gain, w_group_router, b_group_router,
           w_expert_router, b_expert_router, w_gate, w_up, w_down, final_norm_gain):
    b, s, d = x.shape
    assert w_in.shape[0] == 1, "the combine stage fuses the final norm: single-layer stacks only"
    layer = 0
    x2d = x.reshape(b * s, d)
    x1, pos, rf, ys = _layer(
        x2d, b, s, layer, attn_norm_gain[layer], w_in[layer], sb_norm_gain[layer],
        diff_lambda_q1[layer], diff_lambda_k1[layer], diff_lambda_q2[layer],
        diff_lambda_k2[layer], diff_subln_gain[layer], w_out[layer], ffn_norm_gain[layer],
        w_group_router[layer], b_group_router[layer], w_expert_router[layer],
        b_expert_router[layer], w_gate[layer], w_up[layer], w_down[layer])
    out = _combine_call(pos, x1, rf, final_norm_gain.reshape(1, d), ys)
    return out.reshape(b, s, d)
```

```python
import functools
import math

import jax
import jax.numpy as jnp
import numpy as np
from jax import lax
from jax.experimental import pallas as pl
from jax.experimental.pallas import tpu as pltpu

F32 = jnp.float32
BF16 = jnp.bfloat16

HEAD_DIM = 128
N_SB_HEADS = 8
N_DIFF_HEADS = 4
SECTION = 1024
CHUNK = 64
ROPE_THETA = 10000.0
N_GROUPS = 4
EXPERTS_PER_GROUP = 8
N_EXPERTS = N_GROUPS * EXPERTS_PER_GROUP
NORM_EPS = 1e-6
NEG_INF = -1e30
LANES = 128
ROW_CHUNK = 32
UNDERFLOW_LOG2 = -200.0
DMA_UNROLL = 8
SCALE_LOG2E = math.log2(math.e) / math.sqrt(HEAD_DIM)

VMEM_LIMIT = 56 * 1024 * 1024


def _cparams(sem):
    return pltpu.CompilerParams(dimension_semantics=sem, vmem_limit_bytes=VMEM_LIMIT)


def _dot(a, b):
    return jnp.dot(a, b, preferred_element_type=F32)


def _dot_nt(a, b):
    return lax.dot_general(a, b, (((1,), (1,)), ((), ())), preferred_element_type=F32)


def _proj_kernel(x_ref, g_ref, w_ref, cos_ref, sin_ref, o_ref, h_ref, *, tn, rows):
    j = pl.program_id(1)
    tm = x_ref.shape[0]

    @pl.when(j == 0)
    def _():
        for r in range(0, tm, rows):
            x = x_ref[r:r + rows, :]
            ms = jnp.mean(x * x, axis=-1, keepdims=True)
            h_ref[r:r + rows, :] = (x * lax.rsqrt(ms + NORM_EPS) * g_ref[...]).astype(BF16)

    acc = _dot(h_ref[...], w_ref[...])
    sec = j // (SECTION // tn)

    def rotary(scale):
        for c in range(tn // HEAD_DIM):
            a = acc[:, c * HEAD_DIM:(c + 1) * HEAD_DIM]
            rot = a * cos_ref[...] + pltpu.roll(a, HEAD_DIM // 2, 1) * sin_ref[...]
            if scale != 1.0:
                rot = rot * scale
            o_ref[:, c * HEAD_DIM:(c + 1) * HEAD_DIM] = rot.astype(BF16)

    @pl.when(sec == 0)
    def _():
        o_ref[...] = (acc * (-SCALE_LOG2E)).astype(BF16)

    @pl.when((sec == 1) | (sec == 2) | (sec == 5))
    def _():
        o_ref[...] = acc.astype(BF16)

    @pl.when(sec == 3)
    def _():
        rotary(SCALE_LOG2E)

    @pl.when(sec == 4)
    def _():
        rotary(1.0)


def _proj_call(x2d, gain, w_in, cos, sin, seq, *, tm, tn):
    t, d = x2d.shape
    n = w_in.shape[1]
    assert t % tm == 0 and seq % tm == 0 and n % tn == 0 and SECTION % tn == 0
    rows = min(tm, 256)
    nseq = seq // tm
    return pl.pallas_call(
        functools.partial(_proj_kernel, tn=tn, rows=rows),
        grid=(t // tm, n // tn),
        in_specs=[
            pl.BlockSpec((tm, d), lambda i, j: (i, 0)),
            pl.BlockSpec((1, d), lambda i, j: (0, 0)),
            pl.BlockSpec((d, tn), lambda i, j: (0, j)),
            pl.BlockSpec((tm, HEAD_DIM), lambda i, j: (i % nseq, 0)),
            pl.BlockSpec((tm, HEAD_DIM), lambda i, j: (i % nseq, 0)),
        ],
        out_specs=pl.BlockSpec((tm, tn), lambda i, j: (i, j)),
        out_shape=jax.ShapeDtypeStruct((t, n), BF16),
        scratch_shapes=[pltpu.VMEM((tm, d), BF16)],
        compiler_params=_cparams(("arbitrary", "arbitrary")),
        name="proj",
    )(x2d, gain, w_in, cos, sin)


def _sb_kernel(q_ref, k_ref, v_ref, tri_ref, g_ref, o_ref, acc_ref, *, tb, nsub):
    qi = pl.program_id(2)
    tq = nsub * tb
    lo, hi = slice(0, tb), slice(tb, tq)
    row = lax.broadcasted_iota(jnp.int32, (tb, tb), 0)
    col = lax.broadcasted_iota(jnp.int32, (tb, tb), 1)
    strict = col < row

    def scores(rows, kj):
        start = pl.multiple_of(kj * tb, tb)
        return _dot_nt(q_ref[0, rows, :], k_ref[0, pl.ds(start, tb), :])

    def keep_logs(n, mask):
        lks = []
        for r in range(0, n.shape[0], ROW_CHUNK):
            nc = n[r:r + ROW_CHUNK]
            lk = jnp.minimum(nc, 0.0) - jnp.log2(1.0 + jnp.exp2(-jnp.abs(nc)))
            if mask is not None:
                lk = jnp.where(mask[r:r + ROW_CHUNK], lk, 0.0)
            lks.append(lk.astype(BF16))
        return jnp.concatenate(lks, axis=0)

    def accumulate(rows, kj, n, lk, carry, mask):
        start = pl.multiple_of(kj * tb, tb)
        cum = _dot(lk, tri_ref[...])
        parts = []
        for r in range(0, n.shape[0], ROW_CHUNK):
            chunk = slice(r, r + ROW_CHUNK)
            a = jnp.exp2(cum[chunk] + carry[chunk] - n[chunk])
            if mask is not None:
                a = jnp.where(mask[chunk], a, 0.0)
            parts.append(a.astype(BF16))
        acc_ref[rows, :] += _dot(jnp.concatenate(parts, axis=0), v_ref[0, pl.ds(start, tb), :])
        return carry + cum[:, 0:1]

    def several(rows, kjs, carry, mask=None):
        staged = []
        for kj in kjs:
            n = scores(rows, kj)
            staged.append((kj, n, keep_logs(n, mask)))
        for kj, n, lk in staged:
            carry = accumulate(rows, kj, n, lk, carry, mask)
        return carry

    acc_ref[...] = jnp.zeros_like(acc_ref)
    base = qi * nsub
    zero = jnp.zeros((tb, 1), F32)
    prev = jnp.maximum(base - 1, 0)
    has_prev = jnp.broadcast_to(base > 0, (tb, tb))
    chains = [(1, base + 1, strict), (0, base, strict), (1, base, None), (0, prev, has_prev)]
    halves = [lo, hi]
    staged = []
    for half, kj, mask in chains:
        n = scores(halves[half], kj)
        staged.append((n, keep_logs(n, mask)))
    carry = [zero, zero]
    for (half, kj, mask), (n, lk) in zip(chains, staged):
        carry[half] = accumulate(halves[half], kj, n, lk, carry[half], mask)

    def walk_back(rows, first, carry):
        def live(state):
            kj, c = state
            return (kj >= 0) & (jnp.max(c) > UNDERFLOW_LOG2)

        def one_block(state):
            kj, c = state
            return kj - 1, several(rows, [kj], c)

        lax.while_loop(live, one_block, (first, carry))

    walk_back(hi, base - 1, carry[1])
    walk_back(lo, base - 2, carry[0])

    o = acc_ref[...]
    ms = jnp.mean(o * o, axis=-1, keepdims=True)
    o_ref[0] = (o * lax.rsqrt(ms + NORM_EPS) * g_ref[...]).astype(BF16)


def _sb_call(proj3d, tri, gain, *, tb, nsub):
    b, s, _ = proj3d.shape
    tq = nsub * tb
    assert s % tq == 0 and nsub == 2 and tb % ROW_CHUNK == 0
    hq, hk, hv = 0, SECTION // HEAD_DIM, 2 * SECTION // HEAD_DIM
    return pl.pallas_call(
        functools.partial(_sb_kernel, tb=tb, nsub=nsub),
        grid=(b, N_SB_HEADS, s // tq),
        in_specs=[
            pl.BlockSpec((1, tq, HEAD_DIM), lambda bi, h, i: (bi, i, hq + h)),
            pl.BlockSpec((1, s, HEAD_DIM), lambda bi, h, i: (bi, 0, hk + h)),
            pl.BlockSpec((1, s, HEAD_DIM), lambda bi, h, i: (bi, 0, hv + h)),
            pl.BlockSpec((tb, tb), lambda bi, h, i: (0, 0)),
            pl.BlockSpec((1, HEAD_DIM), lambda bi, h, i: (0, 0)),
        ],
        out_specs=pl.BlockSpec((1, tq, HEAD_DIM), lambda bi, h, i: (bi, i, h)),
        out_shape=jax.ShapeDtypeStruct((b, s, N_SB_HEADS * HEAD_DIM), BF16),
        scratch_shapes=[pltpu.VMEM((tq, HEAD_DIM), F32)],
        compiler_params=_cparams(("arbitrary", "arbitrary", "arbitrary")),
        name="sb_attn",
    )(proj3d, proj3d, proj3d, tri, gain)


def _diff_kernel(q_ref, k_ref, v_ref, lq1_ref, lk1_ref, lq2_ref, lk2_ref, g_ref, o_ref,
                 acc1_ref, acc2_ref, *, tq, lam_init):
    qi = pl.program_id(2)
    d = HEAD_DIM
    q1 = q_ref[0, :, :d]
    q2 = q_ref[0, :, d:]
    row = lax.broadcasted_iota(jnp.int32, (tq, tq), 0)
    col = lax.broadcasted_iota(jnp.int32, (tq, tq), 1)
    visible = (col // CHUNK) <= (row // CHUNK)

    def probabilities(s, m, l):
        m_new = jnp.maximum(m, jnp.max(s, axis=-1, keepdims=True))
        alpha = jnp.exp2(m - m_new)
        p = jnp.exp2(s - m_new)
        return p.astype(BF16), alpha, m_new, alpha * l + jnp.sum(p, axis=-1, keepdims=True)

    def block(start, width, carry, masked):
        m1, l1, m2, l2 = carry
        k = k_ref[0, pl.ds(start, width), :]
        v = v_ref[0, pl.ds(start, width), :]
        s1 = _dot_nt(q1, k[:, :d])
        s2 = _dot_nt(q2, k[:, d:])
        if masked:
            s1 = jnp.where(visible, s1, NEG_INF)
            s2 = jnp.where(visible, s2, NEG_INF)
        p1, alpha1, m1, l1 = probabilities(s1, m1, l1)
        p2, alpha2, m2, l2 = probabilities(s2, m2, l2)
        pv = _dot(jnp.concatenate([p1, p2], axis=0), v)
        acc1_ref[...] = alpha1 * acc1_ref[...] + pv[:tq]
        acc2_ref[...] = alpha2 * acc2_ref[...] + pv[tq:]
        return m1, l1, m2, l2

    acc1_ref[...] = jnp.zeros_like(acc1_ref)
    acc2_ref[...] = jnp.zeros_like(acc2_ref)
    neg = jnp.full((tq, 1), NEG_INF, F32)
    zero = jnp.zeros((tq, 1), F32)
    carry = block(pl.multiple_of(qi * tq, tq), tq, (neg, zero, neg, zero), True)
    odd = qi % 2
    carry = lax.cond(odd == 1, lambda c: block(pl.multiple_of((qi - 1) * tq, tq), tq, c, False),
                     lambda c: c, carry)
    _, l1, _, l2 = lax.fori_loop(
        0, qi // 2, lambda i, c: block(pl.multiple_of(2 * i * tq, 2 * tq), 2 * tq, c, False), carry)

    lam = (jnp.exp(jnp.sum(lq1_ref[...] * lk1_ref[...], axis=-1, keepdims=True))
           - jnp.exp(jnp.sum(lq2_ref[...] * lk2_ref[...], axis=-1, keepdims=True)) + lam_init)
    o = acc1_ref[...] / l1 - lam * (acc2_ref[...] / l2)
    ms = jnp.mean(o * o, axis=-1, keepdims=True)
    o_ref[0] = (o * lax.rsqrt(ms + NORM_EPS) * g_ref[...] * (1.0 - lam_init)).astype(BF16)


def _diff_call(proj3d, lq1, lk1, lq2, lk2, gain, lam_init, *, tq):
    b, s, _ = proj3d.shape
    assert s % tq == 0 and tq % CHUNK == 0
    w = 2 * HEAD_DIM
    hq, hk, hv = 3 * SECTION // w, 4 * SECTION // w, 5 * SECTION // w
    vec = pl.BlockSpec((1, HEAD_DIM), lambda bi, h, i: (0, 0))
    return pl.pallas_call(
        functools.partial(_diff_kernel, tq=tq, lam_init=lam_init),
        grid=(b, N_DIFF_HEADS, s // tq),
        in_specs=[
            pl.BlockSpec((1, tq, w), lambda bi, h, i: (bi, i, hq + h)),
            pl.BlockSpec((1, s, w), lambda bi, h, i: (bi, 0, hk + h)),
            pl.BlockSpec((1, s, w), lambda bi, h, i: (bi, 0, hv + h)),
            vec, vec, vec, vec,
            pl.BlockSpec((1, w), lambda bi, h, i: (0, 0)),
        ],
        out_specs=pl.BlockSpec((1, tq, w), lambda bi, h, i: (bi, i, h)),
        out_shape=jax.ShapeDtypeStruct((b, s, N_DIFF_HEADS * w), BF16),
        scratch_shapes=[pltpu.VMEM((tq, w), F32), pltpu.VMEM((tq, w), F32)],
        compiler_params=_cparams(("arbitrary", "arbitrary", "arbitrary")),
        name="diff_attn",
    )(proj3d, proj3d, proj3d, lq1, lk1, lq2, lk2, gain)


def _outproj_kernel(x_ref, sb_ref, df_ref, wo_ref, g_ref, wr_ref, br_ref, tri_ref,
                    x1_ref, h2_ref, ri_ref, rf_ref, cnt_ref):
    i = pl.program_id(0)
    tm = x_ref.shape[0]
    half = sb_ref.shape[1]
    x1 = x_ref[...] + _dot(sb_ref[...], wo_ref[:half, :]) + _dot(df_ref[...], wo_ref[half:, :])
    x1_ref[...] = x1
    ms = jnp.mean(x1 * x1, axis=-1, keepdims=True)
    h2 = x1 * lax.rsqrt(ms + NORM_EPS) * g_ref[...]
    h2_ref[...] = h2

    logits = _dot(h2.astype(BF16), wr_ref[...]) + br_ref[...]
    lane = lax.broadcasted_iota(jnp.int32, (tm, LANES), 1).astype(F32)
    ninf = -jnp.inf

    def first_argmax(vals):
        top = jnp.max(vals, axis=-1, keepdims=True)
        idx = jnp.min(jnp.where(vals == top, lane, float(LANES)), axis=-1, keepdims=True)
        return top, idx

    gl = jnp.where(lane < N_GROUPS, logits, ninf)
    gmax, gidx = first_argmax(gl)
    g_val = 1.0 / jnp.sum(jnp.exp(gl - gmax), axis=-1, keepdims=True)
    lo = N_GROUPS + EXPERTS_PER_GROUP * gidx
    el = jnp.where((lane >= lo) & (lane < lo + EXPERTS_PER_GROUP), logits, ninf)
    l1, i1 = first_argmax(el)
    l2, i2 = first_argmax(jnp.where(lane == i1, ninf, el))
    r = jnp.exp(l2 - l1)
    w1 = g_val / (1.0 + r)
    w2 = g_val * r / (1.0 + r)
    e1 = i1 - N_GROUPS
    e2 = i2 - N_GROUPS

    @pl.when(i == 0)
    def _():
        cnt_ref[...] = jnp.zeros_like(cnt_ref)

    onehot = jnp.where((lane == e1) | (lane == e2), 1.0, 0.0)
    before = _dot(tri_ref[...], onehot.astype(BF16)) + cnt_ref[0:1, :]
    rank1 = jnp.sum(jnp.where(lane == e1, before, 0.0), axis=-1, keepdims=True)
    rank2 = jnp.sum(jnp.where(lane == e2, before, 0.0), axis=-1, keepdims=True)
    cnt_ref[0:1, :] = cnt_ref[0:1, :] + jnp.sum(onehot, axis=0, keepdims=True)

    ri = jnp.where(lane == 0, e1, jnp.where(lane == 1, e2,
                   jnp.where(lane == 2, rank1, jnp.where(lane == 3, rank2, 0.0))))
    ri_ref[...] = ri.astype(jnp.int32)
    rf_ref[...] = jnp.where(lane == 0, w1, jnp.where(lane == 1, w2, 0.0))


def _outproj_call(x2d, sb_out, d_out, w_out, gain, w_router, b_router, tri, *, tm):
    t, d = x2d.shape
    half = sb_out.shape[1]
    assert t % tm == 0
    row = lambda i: (i, 0)
    fixed = lambda i: (0, 0)
    return pl.pallas_call(
        _outproj_kernel,
        grid=(t // tm,),
        in_specs=[
            pl.BlockSpec((tm, d), row),
            pl.BlockSpec((tm, half), row),
            pl.BlockSpec((tm, half), row),
            pl.BlockSpec((2 * half, d), fixed),
            pl.BlockSpec((1, d), fixed),
            pl.BlockSpec((d, LANES), fixed),
            pl.BlockSpec((1, LANES), fixed),
            pl.BlockSpec((tm, tm), fixed),
        ],
        out_specs=[
            pl.BlockSpec((tm, d), row),
            pl.BlockSpec((tm, d), row),
            pl.BlockSpec((tm, LANES), row),
            pl.BlockSpec((tm, LANES), row),
            pl.BlockSpec((8, LANES), fixed),
        ],
        out_shape=[
            jax.ShapeDtypeStruct((t, d), F32),
            jax.ShapeDtypeStruct((t, d), F32),
            jax.ShapeDtypeStruct((t, LANES), jnp.int32),
            jax.ShapeDtypeStruct((t, LANES), F32),
            jax.ShapeDtypeStruct((8, LANES), F32),
        ],
        compiler_params=_cparams(("arbitrary",)),
        name="outproj_router",
    )(x2d, sb_out, d_out, w_out, gain, w_router, b_router, tri)


def _dispatch_kernel(pos_hbm, h_ref, xs_hbm, pos_smem, sem_idx, sem, *, td):
    i = pl.program_id(0)
    idx_copy = pltpu.make_async_copy(pos_hbm.at[i], pos_smem, sem_idx)
    idx_copy.start()
    idx_copy.wait()

    def row_copy(t, k):
        return pltpu.make_async_copy(h_ref.at[pl.ds(t, 1)],
                                     xs_hbm.at[pl.ds(pos_smem[0, k * td + t], 1)], sem)

    def issue(t, c):
        row_copy(t, 0).start()
        row_copy(t, 1).start()
        return c

    lax.fori_loop(0, td, issue, 0, unroll=DMA_UNROLL)

    def drain(t, c):
        row_copy(t, 0).wait()
        row_copy(t, 1).wait()
        return c

    lax.fori_loop(0, td, drain, 0, unroll=DMA_UNROLL)


def _dispatch_call(pos3, h2, n_rows):
    t, d = h2.shape
    td = pos3.shape[2] // 2
    assert pos3.shape[0] * td == t
    return pl.pallas_call(
        functools.partial(_dispatch_kernel, td=td),
        grid=(t // td,),
        in_specs=[pl.BlockSpec(memory_space=pl.ANY), pl.BlockSpec((td, d), lambda i: (i, 0))],
        out_specs=pl.BlockSpec(memory_space=pl.ANY),
        out_shape=jax.ShapeDtypeStruct((n_rows, d), h2.dtype),
        scratch_shapes=[pltpu.SMEM((1, 2 * td), jnp.int32), pltpu.SemaphoreType.DMA,
                        pltpu.SemaphoreType.DMA],
        compiler_params=_cparams(("arbitrary",)),
        name="dispatch",
    )(pos3, h2)


def _experts_kernel(wt_ref, we_ref, nw_ref, lo_ref, hi_ref, xs_ref, wg_ref, wu_ref, wd_ref, ys_ref):
    w = pl.program_id(0)
    tx = xs_ref.shape[0]
    tile = wt_ref[w]
    first = (w == 0) | (wt_ref[jnp.maximum(w - 1, 0)] != tile)

    @pl.when(w < nw_ref[0])
    def _():
        e = we_ref[w]
        rows = tile * tx + lax.broadcasted_iota(jnp.int32, (tx, 1), 0)
        member = (rows >= lo_ref[e]) & (rows < hi_ref[e])
        x = xs_ref[...].astype(BF16)
        gate = _dot(x, wg_ref[0].astype(BF16))
        up = _dot(x, wu_ref[0].astype(BF16))
        hid = gate * (1.0 / (1.0 + jnp.exp(-gate))) * up
        y = _dot(jnp.where(member, hid, 0.0).astype(BF16), wd_ref[0].astype(BF16))

        @pl.when(first)
        def _():
            ys_ref[...] = y

        @pl.when(jnp.logical_not(first))
        def _():
            ys_ref[...] += y


def _experts_call(work_tile, work_expert, n_work, seg_lo, seg_hi, xs, w_gate, w_up, w_down, *, tx):
    p, d = xs.shape
    de = w_gate.shape[2]
    assert p % tx == 0
    n_items = work_tile.shape[0]
    tile = lambda w, wt, we, nw, lo, hi: (wt[w], 0)
    expert = lambda w, wt, we, nw, lo, hi: (we[w], 0, 0)
    return pl.pallas_call(
        _experts_kernel,
        grid_spec=pltpu.PrefetchScalarGridSpec(
            num_scalar_prefetch=5,
            grid=(n_items,),
            in_specs=[
                pl.BlockSpec((tx, d), tile),
                pl.BlockSpec((1, d, de), expert),
                pl.BlockSpec((1, d, de), expert),
                pl.BlockSpec((1, de, d), expert),
            ],
            out_specs=pl.BlockSpec((tx, d), tile),
        ),
        out_shape=jax.ShapeDtypeStruct((p, d), F32),
        compiler_params=_cparams(("arbitrary",)),
        name="experts",
    )(work_tile, work_expert, n_work, seg_lo, seg_hi, xs, w_gate, w_up, w_down)


def _combine_kernel(pos_hbm, x1_ref, rf_ref, g_ref, ys_hbm, o_ref, pos_smem, y_ref, sem_idx, sem, *, tc):
    i = pl.program_id(0)
    last = pl.num_programs(0) - 1

    def row_copy(slot, t, k):
        return pltpu.make_async_copy(ys_hbm.at[pl.ds(pos_smem[slot, k * tc + t], 1)],
                                     y_ref.at[slot, k, pl.ds(t, 1)], sem.at[slot])

    def gather(tile, slot):
        idx_copy = pltpu.make_async_copy(pos_hbm.at[tile, 0], pos_smem.at[slot], sem_idx)
        idx_copy.start()
        idx_copy.wait()

        def issue(t, c):
            row_copy(slot, t, 0).start()
            row_copy(slot, t, 1).start()
            return c

        lax.fori_loop(0, tc, issue, 0, unroll=DMA_UNROLL)

    slot = i % 2

    @pl.when(i == 0)
    def _():
        gather(0, 0)

    @pl.when(i < last)
    def _():
        gather(i + 1, 1 - slot)

    def drain(t, c):
        row_copy(slot, t, 0).wait()
        row_copy(slot, t, 1).wait()
        return c

    lax.fori_loop(0, tc, drain, 0, unroll=DMA_UNROLL)

    x = x1_ref[...] + rf_ref[:, 0:1] * y_ref[slot, 0] + rf_ref[:, 1:2] * y_ref[slot, 1]
    ms = jnp.mean(x * x, axis=-1, keepdims=True)
    o_ref[...] = x * lax.rsqrt(ms + NORM_EPS) * g_ref[...]


def _combine_call(pos3, x1, rf, gain, ys):
    t, d = x1.shape
    tc = pos3.shape[2] // 2
    assert pos3.shape[0] * tc == t
    row = lambda i: (i, 0)
    return pl.pallas_call(
        functools.partial(_combine_kernel, tc=tc),
        grid=(t // tc,),
        in_specs=[
            pl.BlockSpec(memory_space=pl.ANY),
            pl.BlockSpec((tc, d), row),
            pl.BlockSpec((tc, LANES), row),
            pl.BlockSpec((1, d), lambda i: (0, 0)),
            pl.BlockSpec(memory_space=pl.ANY),
        ],
        out_specs=pl.BlockSpec((tc, d), row),
        out_shape=jax.ShapeDtypeStruct((t, d), F32),
        scratch_shapes=[pltpu.SMEM((2, 2 * tc), jnp.int32), pltpu.VMEM((2, 2, tc, d), F32),
                        pltpu.SemaphoreType.DMA, pltpu.SemaphoreType.DMA((2,))],
        compiler_params=_cparams(("arbitrary",)),
        name="combine",
    )(pos3, x1, rf, gain, ys)


def _pick(n, pref):
    while n % pref:
        pref //= 2
    return pref


def _layer(x2d, b, s, layer, attn_norm_gain, w_in, sb_norm_gain, lq1, lk1, lq2, lk2, subln_gain,
           w_out, ffn_norm_gain, w_gr, b_gr, w_er, b_er, w_gate, w_up, w_down):
    t, d = x2d.shape
    tb = _pick(s, 256)
    tm_proj = _pick(s, 1024)
    tm_out = _pick(t, 256)
    tx = 256

    half = HEAD_DIM // 2
    inv_freq = 1.0 / (ROPE_THETA ** (np.arange(half, dtype=np.float64) / half))
    ang = np.arange(s, dtype=np.float64)[:, None] * inv_freq[None, :]
    cos = jnp.asarray(np.concatenate([np.cos(ang), np.cos(ang)], axis=-1), F32)
    sin = jnp.asarray(np.concatenate([-np.sin(ang), np.sin(ang)], axis=-1), F32)

    proj = _proj_call(x2d, attn_norm_gain.reshape(1, d), w_in.astype(BF16), cos, sin, s,
                      tm=tm_proj, tn=512)
    proj3d = proj.reshape(b, s, proj.shape[1])

    r = jnp.arange(tb)
    tri_suffix = (r[:, None] >= r[None, :]).astype(BF16)
    sb_out = _sb_call(proj3d, tri_suffix, sb_norm_gain.reshape(1, HEAD_DIM), tb=tb, nsub=2)

    lam_init = 0.8 - 0.6 * math.exp(-0.3 * layer)
    d_out = _diff_call(proj3d, lq1.reshape(1, -1), lk1.reshape(1, -1), lq2.reshape(1, -1),
                       lk2.reshape(1, -1), subln_gain.reshape(1, -1), lam_init, tq=2 * tb)

    w_router = jnp.zeros((d, LANES), F32)
    w_router = w_router.at[:, :N_GROUPS].set(w_gr).at[:, N_GROUPS:N_GROUPS + N_EXPERTS].set(w_er)
    b_router = jnp.zeros((1, LANES), F32)
    b_router = b_router.at[0, :N_GROUPS].set(b_gr).at[0, N_GROUPS:N_GROUPS + N_EXPERTS].set(b_er)
    rr = jnp.arange(tm_out)
    tri_before = (rr[None, :] < rr[:, None]).astype(BF16)
    x1, h2, ri, rf, cnt = _outproj_call(
        x2d, sb_out.reshape(t, -1), d_out.reshape(t, -1), w_out.astype(BF16),
        ffn_norm_gain.reshape(1, d), w_router.astype(BF16), b_router, tri_before, tm=tm_out)

    counts = cnt[0, :N_EXPERTS].astype(jnp.int32)
    seg_hi = jnp.cumsum(counts)
    seg_lo = seg_hi - counts
    experts = jnp.arange(N_EXPERTS, dtype=jnp.int32)

    def sorted_position(slot):
        hit = ri[:, slot][:, None] == experts[None, :]
        return jnp.sum(jnp.where(hit, seg_lo[None, :], 0), axis=1) + ri[:, 2 + slot]

    pos = jnp.stack([sorted_position(0).reshape(-1, tm_out), sorted_position(1).reshape(-1, tm_out)],
                    axis=1).reshape(-1, 1, 2 * tm_out)
    first_tile = seg_lo // tx
    items = jnp.where(counts > 0, (seg_hi - 1) // tx - first_tile + 1, 0)
    item_hi = jnp.cumsum(items)
    n_work = item_hi[-1]
    w = jnp.minimum(jnp.arange(2 * t // tx + N_EXPERTS - 1, dtype=jnp.int32), n_work - 1)
    work_expert = jnp.sum((item_hi[None, :] <= w[:, None]).astype(jnp.int32), axis=1)
    work_tile = (first_tile[work_expert] + w - (item_hi - items)[work_expert]).astype(jnp.int32)

    xs = _dispatch_call(pos, h2, 2 * t)
    ys = _experts_call(work_tile, work_expert, n_work.reshape(1).astype(jnp.int32), seg_lo, seg_hi,
                       xs, w_gate, w_up, w_down, tx=tx)
    return x1, pos, rf, ys


def kernel(x, attn_norm_gain, w_in, sb_norm_gain, diff_lambda_q1, diff_lambda_k1, diff_lambda_q2,
           diff_lambda_k2, diff_subln_gain, w_out, ffn_norm_gain, w_group_router, b_group_router,
           w_expert_router, b_expert_router, w_gate, w_up, w_down, final_norm_gain):
    b, s, d = x.shape
    assert w_in.shape[0] == 1, "the combine stage fuses the final norm: single-layer stacks only"
    layer = 0
    x2d = x.reshape(b * s, d)
    x1, pos, rf, ys = _layer(
        x2d, b, s, layer, attn_norm_gain[layer], w_in[layer], sb_norm_gain[layer],
        diff_lambda_q1[layer], diff_lambda_k1[layer], diff_lambda_q2[layer],
        diff_lambda_k2[layer], diff_subln_gain[layer], w_out[layer], ffn_norm_gain[layer],
        w_group_router[layer], b_group_router[layer], w_expert_router[layer],
        b_expert_router[layer], w_gate[layer], w_up[layer], w_down[layer])
    out = _combine_call(pos, x1, rf, final_norm_gain.reshape(1, d), ys)
    return out.reshape(b, s, d)
```

```python
import functools
import math

import jax
import jax.numpy as jnp
import numpy as np
from jax import lax
from jax.experimental import pallas as pl
from jax.experimental.pallas import tpu as pltpu

F32 = jnp.float32
BF16 = jnp.bfloat16

HEAD_DIM = 128
N_SB_HEADS = 8
N_DIFF_HEADS = 4
SECTION = 1024
CHUNK = 64
ROPE_THETA = 10000.0
N_GROUPS = 4
EXPERTS_PER_GROUP = 8
N_EXPERTS = N_GROUPS * EXPERTS_PER_GROUP
NORM_EPS = 1e-6
NEG_INF = -1e30
LANES = 128
ROW_CHUNK = 32
UNDERFLOW_LOG2 = -200.0
DMA_UNROLL = 8
SCALE_LOG2E = math.log2(math.e) / math.sqrt(HEAD_DIM)

VMEM_LIMIT = 56 * 1024 * 1024


def _cparams(sem):
    return pltpu.CompilerParams(dimension_semantics=sem, vmem_limit_bytes=VMEM_LIMIT)


def _dot(a, b):
    return jnp.dot(a, b, preferred_element_type=F32)


def _dot_nt(a, b):
    return lax.dot_general(a, b, (((1,), (1,)), ((), ())), preferred_element_type=F32)


def _proj_kernel(x_ref, g_ref, w_ref, cos_ref, sin_ref, o_ref, h_ref, *, tn, rows):
    j = pl.program_id(1)
    tm = x_ref.shape[0]

    @pl.when(j == 0)
    def _():
        for r in range(0, tm, rows):
            x = x_ref[r:r + rows, :]
            ms = jnp.mean(x * x, axis=-1, keepdims=True)
            h_ref[r:r + rows, :] = (x * lax.rsqrt(ms + NORM_EPS) * g_ref[...]).astype(BF16)

    acc = _dot(h_ref[...], w_ref[...].astype(BF16))
    sec = j // (SECTION // tn)

    def rotary(scale):
        for c in range(tn // HEAD_DIM):
            a = acc[:, c * HEAD_DIM:(c + 1) * HEAD_DIM]
            rot = a * cos_ref[...] + pltpu.roll(a, HEAD_DIM // 2, 1) * sin_ref[...]
            if scale != 1.0:
                rot = rot * scale
            o_ref[:, c * HEAD_DIM:(c + 1) * HEAD_DIM] = rot.astype(BF16)

    @pl.when(sec == 0)
    def _():
        o_ref[...] = (acc * (-SCALE_LOG2E)).astype(BF16)

    @pl.when((sec == 1) | (sec == 2) | (sec == 5))
    def _():
        o_ref[...] = acc.astype(BF16)

    @pl.when(sec == 3)
    def _():
        rotary(SCALE_LOG2E)

    @pl.when(sec == 4)
    def _():
        rotary(1.0)


def _proj_call(x2d, gain, w_in, cos, sin, seq, *, tm, tn):
    t, d = x2d.shape
    n = w_in.shape[1]
    assert t % tm == 0 and seq % tm == 0 and n % tn == 0 and SECTION % tn == 0
    rows = min(tm, 256)
    nseq = seq // tm
    return pl.pallas_call(
        functools.partial(_proj_kernel, tn=tn, rows=rows),
        grid=(t // tm, n // tn),
        in_specs=[
            pl.BlockSpec((tm, d), lambda i, j: (i, 0)),
            pl.BlockSpec((1, d), lambda i, j: (0, 0)),
            pl.BlockSpec((d, tn), lambda i, j: (0, j)),
            pl.BlockSpec((tm, HEAD_DIM), lambda i, j: (i % nseq, 0)),
            pl.BlockSpec((tm, HEAD_DIM), lambda i, j: (i % nseq, 0)),
        ],
        out_specs=pl.BlockSpec((tm, tn), lambda i, j: (i, j)),
        out_shape=jax.ShapeDtypeStruct((t, n), BF16),
        scratch_shapes=[pltpu.VMEM((tm, d), BF16)],
        compiler_params=_cparams(("arbitrary", "arbitrary")),
        name="proj",
    )(x2d, gain, w_in, cos, sin)


def _sb_kernel(q_ref, k_ref, v_ref, tri_ref, g_ref, o_ref, acc_ref, *, tb, nsub):
    qi = pl.program_id(2)
    tq = nsub * tb
    lo, hi = slice(0, tb), slice(tb, tq)
    row = lax.broadcasted_iota(jnp.int32, (tb, tb), 0)
    col = lax.broadcasted_iota(jnp.int32, (tb, tb), 1)
    strict = col < row

    def scores(rows, kj):
        start = pl.multiple_of(kj * tb, tb)
        return _dot_nt(q_ref[0, rows, :], k_ref[0, pl.ds(start, tb), :])

    def keep_logs(n, mask):
        lks = []
        for r in range(0, n.shape[0], ROW_CHUNK):
            nc = n[r:r + ROW_CHUNK]
            lk = jnp.minimum(nc, 0.0) - jnp.log2(1.0 + jnp.exp2(-jnp.abs(nc)))
            if mask is not None:
                lk = jnp.where(mask[r:r + ROW_CHUNK], lk, 0.0)
            lks.append(lk.astype(BF16))
        return jnp.concatenate(lks, axis=0)

    def accumulate(rows, kj, n, lk, carry, mask):
        start = pl.multiple_of(kj * tb, tb)
        cum = _dot(lk, tri_ref[...])
        parts = []
        for r in range(0, n.shape[0], ROW_CHUNK):
            chunk = slice(r, r + ROW_CHUNK)
            a = jnp.exp2(cum[chunk] + carry[chunk] - n[chunk])
            if mask is not None:
                a = jnp.where(mask[chunk], a, 0.0)
            parts.append(a.astype(BF16))
        acc_ref[rows, :] += _dot(jnp.concatenate(parts, axis=0), v_ref[0, pl.ds(start, tb), :])
        return carry + cum[:, 0:1]

    def several(rows, kjs, carry, mask=None):
        staged = []
        for kj in kjs:
            n = scores(rows, kj)
            staged.append((kj, n, keep_logs(n, mask)))
        for kj, n, lk in staged:
            carry = accumulate(rows, kj, n, lk, carry, mask)
        return carry

    acc_ref[...] = jnp.zeros_like(acc_ref)
    base = qi * nsub
    zero = jnp.zeros((tb, 1), F32)
    prev = jnp.maximum(base - 1, 0)
    has_prev = jnp.broadcast_to(base > 0, (tb, tb))
    chains = [(1, base + 1, strict), (0, base, strict), (1, base, None), (0, prev, has_prev)]
    halves = [lo, hi]
    staged = []
    for half, kj, mask in chains:
        n = scores(halves[half], kj)
        staged.append((n, keep_logs(n, mask)))
    carry = [zero, zero]
    for (half, kj, mask), (n, lk) in zip(chains, staged):
        carry[half] = accumulate(halves[half], kj, n, lk, carry[half], mask)

    def walk_back(rows, first, carry):
        def still_live(kj, c):
            return (kj >= 0) & (jnp.max(c) > UNDERFLOW_LOG2)

        def one_block(state):
            kj, c, _ = state
            c = several(rows, [kj], c)
            return kj - 1, c, still_live(kj - 1, c)

        lax.while_loop(lambda state: state[2], one_block, (first, carry, still_live(first, carry)))

    walk_back(hi, base - 1, carry[1])
    walk_back(lo, base - 2, carry[0])

    o = acc_ref[...]
    ms = jnp.mean(o * o, axis=-1, keepdims=True)
    o_ref[0] = (o * lax.rsqrt(ms + NORM_EPS) * g_ref[...]).astype(BF16)


def _sb_call(proj3d, tri, gain, *, tb, nsub):
    b, s, _ = proj3d.shape
    tq = nsub * tb
    assert s % tq == 0 and nsub == 2 and tb % ROW_CHUNK == 0
    hq, hk, hv = 0, SECTION // HEAD_DIM, 2 * SECTION // HEAD_DIM
    return pl.pallas_call(
        functools.partial(_sb_kernel, tb=tb, nsub=nsub),
        grid=(b, N_SB_HEADS, s // tq),
        in_specs=[
            pl.BlockSpec((1, tq, HEAD_DIM), lambda bi, h, i: (bi, i, hq + h)),
            pl.BlockSpec((1, s, HEAD_DIM), lambda bi, h, i: (bi, 0, hk + h)),
            pl.BlockSpec((1, s, HEAD_DIM), lambda bi, h, i: (bi, 0, hv + h)),
            pl.BlockSpec((tb, tb), lambda bi, h, i: (0, 0)),
            pl.BlockSpec((1, HEAD_DIM), lambda bi, h, i: (0, 0)),
        ],
        out_specs=pl.BlockSpec((1, tq, HEAD_DIM), lambda bi, h, i: (bi, i, h)),
        out_shape=jax.ShapeDtypeStruct((b, s, N_SB_HEADS * HEAD_DIM), BF16),
        scratch_shapes=[pltpu.VMEM((tq, HEAD_DIM), F32)],
        compiler_params=_cparams(("arbitrary", "arbitrary", "arbitrary")),
        name="sb_attn",
    )(proj3d, proj3d, proj3d, tri, gain)


def _diff_kernel(q_ref, k_ref, v_ref, lq1_ref, lk1_ref, lq2_ref, lk2_ref, g_ref, o_ref,
                 acc1_ref, acc2_ref, *, tq, lam_init):
    qi = pl.program_id(2)
    d = HEAD_DIM
    q1 = q_ref[0, :, :d]
    q2 = q_ref[0, :, d:]
    row = lax.broadcasted_iota(jnp.int32, (tq, tq), 0)
    col = lax.broadcasted_iota(jnp.int32, (tq, tq), 1)
    visible = (col // CHUNK) <= (row // CHUNK)

    def probabilities(s, m, l):
        m_new = jnp.maximum(m, jnp.max(s, axis=-1, keepdims=True))
        alpha = jnp.exp2(m - m_new)
        p = jnp.exp2(s - m_new)
        return p.astype(BF16), alpha, m_new, alpha * l + jnp.sum(p, axis=-1, keepdims=True)

    def block(start, width, carry, masked):
        m1, l1, m2, l2 = carry
        k = k_ref[0, pl.ds(start, width), :]
        v = v_ref[0, pl.ds(start, width), :]
        s1 = _dot_nt(q1, k[:, :d])
        s2 = _dot_nt(q2, k[:, d:])
        if masked:
            s1 = jnp.where(visible, s1, NEG_INF)
            s2 = jnp.where(visible, s2, NEG_INF)
        p1, alpha1, m1, l1 = probabilities(s1, m1, l1)
        p2, alpha2, m2, l2 = probabilities(s2, m2, l2)
        pv = _dot(jnp.concatenate([p1, p2], axis=0), v)
        acc1_ref[...] = alpha1 * acc1_ref[...] + pv[:tq]
        acc2_ref[...] = alpha2 * acc2_ref[...] + pv[tq:]
        return m1, l1, m2, l2

    acc1_ref[...] = jnp.zeros_like(acc1_ref)
    acc2_ref[...] = jnp.zeros_like(acc2_ref)
    neg = jnp.full((tq, 1), NEG_INF, F32)
    zero = jnp.zeros((tq, 1), F32)
    carry = block(pl.multiple_of(qi * tq, tq), tq, (neg, zero, neg, zero), True)
    odd = qi % 2
    carry = lax.cond(odd == 1, lambda c: block(pl.multiple_of((qi - 1) * tq, tq), tq, c, False),
                     lambda c: c, carry)
    _, l1, _, l2 = lax.fori_loop(
        0, qi // 2, lambda i, c: block(pl.multiple_of(2 * i * tq, 2 * tq), 2 * tq, c, False), carry)

    lam = (jnp.exp(jnp.sum(lq1_ref[...] * lk1_ref[...], axis=-1, keepdims=True))
           - jnp.exp(jnp.sum(lq2_ref[...] * lk2_ref[...], axis=-1, keepdims=True)) + lam_init)
    o = acc1_ref[...] / l1 - lam * (acc2_ref[...] / l2)
    ms = jnp.mean(o * o, axis=-1, keepdims=True)
    o_ref[0] = (o * lax.rsqrt(ms + NORM_EPS) * g_ref[...] * (1.0 - lam_init)).astype(BF16)


def _diff_call(proj3d, lq1, lk1, lq2, lk2, gain, lam_init, *, tq):
    b, s, _ = proj3d.shape
    assert s % tq == 0 and tq % CHUNK == 0
    w = 2 * HEAD_DIM
    hq, hk, hv = 3 * SECTION // w, 4 * SECTION // w, 5 * SECTION // w
    vec = pl.BlockSpec((1, HEAD_DIM), lambda bi, h, i: (0, 0))
    return pl.pallas_call(
        functools.partial(_diff_kernel, tq=tq, lam_init=lam_init),
        grid=(b, N_DIFF_HEADS, s // tq),
        in_specs=[
            pl.BlockSpec((1, tq, w), lambda bi, h, i: (bi, i, hq + h)),
            pl.BlockSpec((1, s, w), lambda bi, h, i: (bi, 0, hk + h)),
            pl.BlockSpec((1, s, w), lambda bi, h, i: (bi, 0, hv + h)),
            vec, vec, vec, vec,
            pl.BlockSpec((1, w), lambda bi, h, i: (0, 0)),
        ],
        out_specs=pl.BlockSpec((1, tq, w), lambda bi, h, i: (bi, i, h)),
        out_shape=jax.ShapeDtypeStruct((b, s, N_DIFF_HEADS * w), BF16),
        scratch_shapes=[pltpu.VMEM((tq, w), F32), pltpu.VMEM((tq, w), F32)],
        compiler_params=_cparams(("arbitrary", "arbitrary", "arbitrary")),
        name="diff_attn",
    )(proj3d, proj3d, proj3d, lq1, lk1, lq2, lk2, gain)


def _outproj_kernel(x_ref, sb_ref, df_ref, wo_ref, g_ref, wr_ref, br_ref, tri_ref,
                    x1_ref, h2_ref, ri_ref, rf_ref, cnt_ref):
    i = pl.program_id(0)
    tm = x_ref.shape[0]
    half = sb_ref.shape[1]
    x1 = x_ref[...] + _dot(sb_ref[...], wo_ref[:half, :]) + _dot(df_ref[...], wo_ref[half:, :])
    x1_ref[...] = x1
    ms = jnp.mean(x1 * x1, axis=-1, keepdims=True)
    h2 = x1 * lax.rsqrt(ms + NORM_EPS) * g_ref[...]
    h2_ref[...] = h2

    logits = _dot(h2.astype(BF16), wr_ref[...]) + br_ref[...]
    lane = lax.broadcasted_iota(jnp.int32, (tm, LANES), 1).astype(F32)
    ninf = -jnp.inf

    def first_argmax(vals):
        top = jnp.max(vals, axis=-1, keepdims=True)
        idx = jnp.min(jnp.where(vals == top, lane, float(LANES)), axis=-1, keepdims=True)
        return top, idx

    gl = jnp.where(lane < N_GROUPS, logits, ninf)
    gmax, gidx = first_argmax(gl)
    g_val = 1.0 / jnp.sum(jnp.exp(gl - gmax), axis=-1, keepdims=True)
    lo = N_GROUPS + EXPERTS_PER_GROUP * gidx
    el = jnp.where((lane >= lo) & (lane < lo + EXPERTS_PER_GROUP), logits, ninf)
    l1, i1 = first_argmax(el)
    l2, i2 = first_argmax(jnp.where(lane == i1, ninf, el))
    r = jnp.exp(l2 - l1)
    w1 = g_val / (1.0 + r)
    w2 = g_val * r / (1.0 + r)
    e1 = i1 - N_GROUPS
    e2 = i2 - N_GROUPS

    @pl.when(i == 0)
    def _():
        cnt_ref[...] = jnp.zeros_like(cnt_ref)

    onehot = jnp.where((lane == e1) | (lane == e2), 1.0, 0.0)
    before = _dot(tri_ref[...], onehot.astype(BF16)) + cnt_ref[0:1, :]
    rank1 = jnp.sum(jnp.where(lane == e1, before, 0.0), axis=-1, keepdims=True)
    rank2 = jnp.sum(jnp.where(lane == e2, before, 0.0), axis=-1, keepdims=True)
    cnt_ref[0:1, :] = cnt_ref[0:1, :] + jnp.sum(onehot, axis=0, keepdims=True)

    ri = jnp.where(lane == 0, e1, jnp.where(lane == 1, e2,
                   jnp.where(lane == 2, rank1, jnp.where(lane == 3, rank2, 0.0))))
    ri_ref[...] = ri.astype(jnp.int32)
    rf_ref[...] = jnp.where(lane == 0, w1, jnp.where(lane == 1, w2, 0.0))


def _outproj_call(x2d, sb_out, d_out, w_out, gain, w_router, b_router, tri, *, tm):
    t, d = x2d.shape
    half = sb_out.shape[1]
    assert t % tm == 0
    row = lambda i: (i, 0)
    fixed = lambda i: (0, 0)
    return pl.pallas_call(
        _outproj_kernel,
        grid=(t // tm,),
        in_specs=[
            pl.BlockSpec((tm, d), row),
            pl.BlockSpec((tm, half), row),
            pl.BlockSpec((tm, half), row),
            pl.BlockSpec((2 * half, d), fixed),
            pl.BlockSpec((1, d), fixed),
            pl.BlockSpec((d, LANES), fixed),
            pl.BlockSpec((1, LANES), fixed),
            pl.BlockSpec((tm, tm), fixed),
        ],
        out_specs=[
            pl.BlockSpec((tm, d), row),
            pl.BlockSpec((tm, d), row),
            pl.BlockSpec((tm, LANES), row),
            pl.BlockSpec((tm, LANES), row),
            pl.BlockSpec((8, LANES), fixed),
        ],
        out_shape=[
            jax.ShapeDtypeStruct((t, d), F32),
            jax.ShapeDtypeStruct((t, d), F32),
            jax.ShapeDtypeStruct((t, LANES), jnp.int32),
            jax.ShapeDtypeStruct((t, LANES), F32),
            jax.ShapeDtypeStruct((8, LANES), F32),
        ],
        compiler_params=_cparams(("arbitrary",)),
        name="outproj_router",
    )(x2d, sb_out, d_out, w_out, gain, w_router, b_router, tri)


def _dispatch_kernel(pos_hbm, h_ref, xs_hbm, pos_smem, sem_idx, sem, *, td):
    i = pl.program_id(0)
    idx_copy = pltpu.make_async_copy(pos_hbm.at[i], pos_smem, sem_idx)
    idx_copy.start()
    idx_copy.wait()

    def row_copy(t, k):
        return pltpu.make_async_copy(h_ref.at[pl.ds(t, 1)],
                                     xs_hbm.at[pl.ds(pos_smem[0, k * td + t], 1)], sem)

    def issue(t, c):
        row_copy(t, 0).start()
        row_copy(t, 1).start()
        return c

    lax.fori_loop(0, td, issue, 0, unroll=DMA_UNROLL)

    def drain(t, c):
        row_copy(t, 0).wait()
        row_copy(t, 1).wait()
        return c

    lax.fori_loop(0, td, drain, 0, unroll=DMA_UNROLL)


def _dispatch_call(pos3, h2, n_rows):
    t, d = h2.shape
    td = pos3.shape[2] // 2
    assert pos3.shape[0] * td == t
    return pl.pallas_call(
        functools.partial(_dispatch_kernel, td=td),
        grid=(t // td,),
        in_specs=[pl.BlockSpec(memory_space=pl.ANY), pl.BlockSpec((td, d), lambda i: (i, 0))],
        out_specs=pl.BlockSpec(memory_space=pl.ANY),
        out_shape=jax.ShapeDtypeStruct((n_rows, d), h2.dtype),
        scratch_shapes=[pltpu.SMEM((1, 2 * td), jnp.int32), pltpu.SemaphoreType.DMA,
                        pltpu.SemaphoreType.DMA],
        compiler_params=_cparams(("arbitrary",)),
        name="dispatch",
    )(pos3, h2)


def _experts_kernel(wt_ref, we_ref, nw_ref, lo_ref, hi_ref, xs_ref, wg_ref, wu_ref, wd_ref, ys_ref):
    w = pl.program_id(0)
    tx = xs_ref.shape[0]
    tile = wt_ref[w]
    first = (w == 0) | (wt_ref[jnp.maximum(w - 1, 0)] != tile)

    @pl.when(w < nw_ref[0])
    def _():
        e = we_ref[w]
        rows = tile * tx + lax.broadcasted_iota(jnp.int32, (tx, 1), 0)
        member = (rows >= lo_ref[e]) & (rows < hi_ref[e])
        x = xs_ref[...].astype(BF16)
        gate = _dot(x, wg_ref[0].astype(BF16))
        up = _dot(x, wu_ref[0].astype(BF16))
        hid = gate * (1.0 / (1.0 + jnp.exp(-gate))) * up
        y = _dot(jnp.where(member, hid, 0.0).astype(BF16), wd_ref[0].astype(BF16))

        @pl.when(first)
        def _():
            ys_ref[...] = y

        @pl.when(jnp.logical_not(first))
        def _():
            ys_ref[...] += y


def _experts_call(work_tile, work_expert, n_work, seg_lo, seg_hi, xs, w_gate, w_up, w_down, *, tx):
    p, d = xs.shape
    de = w_gate.shape[2]
    assert p % tx == 0
    n_items = work_tile.shape[0]
    tile = lambda w, wt, we, nw, lo, hi: (wt[w], 0)
    expert = lambda w, wt, we, nw, lo, hi: (we[w], 0, 0)
    return pl.pallas_call(
        _experts_kernel,
        grid_spec=pltpu.PrefetchScalarGridSpec(
            num_scalar_prefetch=5,
            grid=(n_items,),
            in_specs=[
                pl.BlockSpec((tx, d), tile),
                pl.BlockSpec((1, d, de), expert),
                pl.BlockSpec((1, d, de), expert),
                pl.BlockSpec((1, de, d), expert),
            ],
            out_specs=pl.BlockSpec((tx, d), tile),
        ),
        out_shape=jax.ShapeDtypeStruct((p, d), F32),
        compiler_params=_cparams(("arbitrary",)),
        name="experts",
    )(work_tile, work_expert, n_work, seg_lo, seg_hi, xs, w_gate, w_up, w_down)


def _combine_kernel(pos_hbm, x1_ref, rf_ref, g_ref, ys_hbm, o_ref, pos_smem, y0_ref, y1_ref,
                    sem_idx, sem, *, tc):
    i = pl.program_id(0)
    idx_copy = pltpu.make_async_copy(pos_hbm.at[i], pos_smem, sem_idx)
    idx_copy.start()
    idx_copy.wait()

    def row_copy(t, k):
        dst = y0_ref if k == 0 else y1_ref
        return pltpu.make_async_copy(ys_hbm.at[pl.ds(pos_smem[0, k * tc + t], 1)],
                                     dst.at[pl.ds(t, 1)], sem)

    def issue(t, c):
        row_copy(t, 0).start()
        row_copy(t, 1).start()
        return c

    lax.fori_loop(0, tc, issue, 0, unroll=DMA_UNROLL)

    def drain(t, c):
        row_copy(t, 0).wait()
        row_copy(t, 1).wait()
        return c

    lax.fori_loop(0, tc, drain, 0, unroll=DMA_UNROLL)

    x = x1_ref[...] + rf_ref[:, 0:1] * y0_ref[...] + rf_ref[:, 1:2] * y1_ref[...]
    ms = jnp.mean(x * x, axis=-1, keepdims=True)
    o_ref[...] = x * lax.rsqrt(ms + NORM_EPS) * g_ref[...]


def _combine_call(pos3, x1, rf, gain, ys):
    t, d = x1.shape
    tc = pos3.shape[2] // 2
    assert pos3.shape[0] * tc == t
    row = lambda i: (i, 0)
    return pl.pallas_call(
        functools.partial(_combine_kernel, tc=tc),
        grid=(t // tc,),
        in_specs=[
            pl.BlockSpec(memory_space=pl.ANY),
            pl.BlockSpec((tc, d), row),
            pl.BlockSpec((tc, LANES), row),
            pl.BlockSpec((1, d), lambda i: (0, 0)),
            pl.BlockSpec(memory_space=pl.ANY),
        ],
        out_specs=pl.BlockSpec((tc, d), row),
        out_shape=jax.ShapeDtypeStruct((t, d), F32),
        scratch_shapes=[pltpu.SMEM((1, 2 * tc), jnp.int32), pltpu.VMEM((tc, d), F32),
                        pltpu.VMEM((tc, d), F32), pltpu.SemaphoreType.DMA,
                        pltpu.SemaphoreType.DMA],
        compiler_params=_cparams(("arbitrary",)),
        name="combine",
    )(pos3, x1, rf, gain, ys)


def _pick(n, pref):
    while n % pref:
        pref //= 2
    return pref


def _layer(x2d, b, s, layer, attn_norm_gain, w_in, sb_norm_gain, lq1, lk1, lq2, lk2, subln_gain,
           w_out, ffn_norm_gain, w_gr, b_gr, w_er, b_er, w_gate, w_up, w_down):
    t, d = x2d.shape
    tb = _pick(s, 256)
    tm_proj = _pick(s, 1024)
    tm_out = _pick(t, 256)
    tx = 256

    half = HEAD_DIM // 2
    inv_freq = 1.0 / (ROPE_THETA ** (np.arange(half, dtype=np.float64) / half))
    ang = np.arange(s, dtype=np.float64)[:, None] * inv_freq[None, :]
    cos = jnp.asarray(np.concatenate([np.cos(ang), np.cos(ang)], axis=-1), F32)
    sin = jnp.asarray(np.concatenate([-np.sin(ang), np.sin(ang)], axis=-1), F32)

    proj = _proj_call(x2d, attn_norm_gain.reshape(1, d), w_in, cos, sin, s,
                      tm=tm_proj, tn=512)
    proj3d = proj.reshape(b, s, proj.shape[1])

    r = jnp.arange(tb)
    tri_suffix = (r[:, None] >= r[None, :]).astype(BF16)
    sb_out = _sb_call(proj3d, tri_suffix, sb_norm_gain.reshape(1, HEAD_DIM), tb=tb, nsub=2)

    lam_init = 0.8 - 0.6 * math.exp(-0.3 * layer)
    d_out = _diff_call(proj3d, lq1.reshape(1, -1), lk1.reshape(1, -1), lq2.reshape(1, -1),
                       lk2.reshape(1, -1), subln_gain.reshape(1, -1), lam_init, tq=2 * tb)

    w_router = jnp.zeros((d, LANES), F32)
    w_router = w_router.at[:, :N_GROUPS].set(w_gr).at[:, N_GROUPS:N_GROUPS + N_EXPERTS].set(w_er)
    b_router = jnp.zeros((1, LANES), F32)
    b_router = b_router.at[0, :N_GROUPS].set(b_gr).at[0, N_GROUPS:N_GROUPS + N_EXPERTS].set(b_er)
    rr = jnp.arange(tm_out)
    tri_before = (rr[None, :] < rr[:, None]).astype(BF16)
    x1, h2, ri, rf, cnt = _outproj_call(
        x2d, sb_out.reshape(t, -1), d_out.reshape(t, -1), w_out.astype(BF16),
        ffn_norm_gain.reshape(1, d), w_router.astype(BF16), b_router, tri_before, tm=tm_out)

    counts = cnt[0, :N_EXPERTS].astype(jnp.int32)
    seg_hi = jnp.cumsum(counts)
    seg_lo = seg_hi - counts
    experts = jnp.arange(N_EXPERTS, dtype=jnp.int32)

    def sorted_position(slot):
        hit = ri[:, slot][:, None] == experts[None, :]
        return jnp.sum(jnp.where(hit, seg_lo[None, :], 0), axis=1) + ri[:, 2 + slot]

    tm_moe = _pick(t, 512)
    pos = jnp.stack([sorted_position(0).reshape(-1, tm_moe), sorted_position(1).reshape(-1, tm_moe)],
                    axis=1).reshape(-1, 1, 2 * tm_moe)
    first_tile = seg_lo // tx
    items = jnp.where(counts > 0, (seg_hi - 1) // tx - first_tile + 1, 0)
    item_hi = jnp.cumsum(items)
    n_work = item_hi[-1]
    w = jnp.minimum(jnp.arange(2 * t // tx + N_EXPERTS - 1, dtype=jnp.int32), n_work - 1)
    work_expert = jnp.sum((item_hi[None, :] <= w[:, None]).astype(jnp.int32), axis=1)
    work_tile = (first_tile[work_expert] + w - (item_hi - items)[work_expert]).astype(jnp.int32)

    xs = _dispatch_call(pos, h2, 2 * t)
    ys = _experts_call(work_tile, work_expert, n_work.reshape(1).astype(jnp.int32), seg_lo, seg_hi,
                       xs, w_gate, w_up, w_down, tx=tx)
    return x1, pos, rf, ys


def kernel(x, attn_norm_gain, w_in, sb_norm_gain, diff_lambda_q1, diff_lambda_k1, diff_lambda_q2,
           diff_lambda_k2, diff_subln_gain, w_out, ffn_norm_gain, w_group_router, b_group_router,
           w_expert_router, b_expert_router, w_gate, w_up, w_down, final_norm_gain):
    b, s, d = x.shape
    assert w_in.shape[0] == 1, "the combine stage fuses the final norm: single-layer stacks only"
    layer = 0
    x2d = x.reshape(b * s, d)
    x1, pos, rf, ys = _layer(
        x2d, b, s, layer, attn_norm_gain[layer], w_in[layer], sb_norm_gain[layer],
        diff_lambda_q1[layer], diff_lambda_k1[layer], diff_lambda_q2[layer],
        diff_lambda_k2[layer], diff_subln_gain[layer], w_out[layer], ffn_norm_gain[layer],
        w_group_router[layer], b_group_router[layer], w_expert_router[layer],
        b_expert_router[layer], w_gate[layer], w_up[layer], w_down[layer])
    out = _combine_call(pos, x1, rf, final_norm_gain.reshape(1, d), ys)
    return out.reshape(b, s, d)
```

```python
import functools
import math

import jax
import jax.numpy as jnp
import numpy as np
from jax import lax
from jax.experimental import pallas as pl
from jax.experimental.pallas import tpu as pltpu

F32 = jnp.float32
BF16 = jnp.bfloat16

HEAD_DIM = 128
N_SB_HEADS = 8
N_DIFF_HEADS = 4
SECTION = 1024
CHUNK = 64
ROPE_THETA = 10000.0
N_GROUPS = 4
EXPERTS_PER_GROUP = 8
N_EXPERTS = N_GROUPS * EXPERTS_PER_GROUP
NORM_EPS = 1e-6
NEG_INF = -1e30
LANES = 128
ROW_CHUNK = 32
UNDERFLOW_LOG2 = -200.0
DMA_UNROLL = 8
SCALE_LOG2E = math.log2(math.e) / math.sqrt(HEAD_DIM)

VMEM_LIMIT = 56 * 1024 * 1024


def _cparams(sem):
    return pltpu.CompilerParams(dimension_semantics=sem, vmem_limit_bytes=VMEM_LIMIT)


def _dot(a, b):
    return jnp.dot(a, b, preferred_element_type=F32)


def _dot_nt(a, b):
    return lax.dot_general(a, b, (((1,), (1,)), ((), ())), preferred_element_type=F32)


def _proj_kernel(x_ref, g_ref, w_ref, cos_ref, sin_ref, o_ref, h_ref, *, tn, rows):
    j = pl.program_id(1)
    tm = x_ref.shape[0]

    @pl.when(j == 0)
    def _():
        for r in range(0, tm, rows):
            x = x_ref[r:r + rows, :]
            ms = jnp.mean(x * x, axis=-1, keepdims=True)
            h_ref[r:r + rows, :] = (x * lax.rsqrt(ms + NORM_EPS) * g_ref[...]).astype(BF16)

    acc = _dot(h_ref[...], w_ref[...].astype(BF16))
    sec = j // (SECTION // tn)

    def rotary(scale):
        for c in range(tn // HEAD_DIM):
            a = acc[:, c * HEAD_DIM:(c + 1) * HEAD_DIM]
            rot = a * cos_ref[...] + pltpu.roll(a, HEAD_DIM // 2, 1) * sin_ref[...]
            if scale != 1.0:
                rot = rot * scale
            o_ref[:, c * HEAD_DIM:(c + 1) * HEAD_DIM] = rot.astype(BF16)

    @pl.when(sec == 0)
    def _():
        o_ref[...] = (acc * (-SCALE_LOG2E)).astype(BF16)

    @pl.when((sec == 1) | (sec == 2) | (sec == 5))
    def _():
        o_ref[...] = acc.astype(BF16)

    @pl.when(sec == 3)
    def _():
        rotary(SCALE_LOG2E)

    @pl.when(sec == 4)
    def _():
        rotary(1.0)


def _proj_call(x2d, gain, w_in, cos, sin, seq, *, tm, tn):
    t, d = x2d.shape
    n = w_in.shape[1]
    assert t % tm == 0 and seq % tm == 0 and n % tn == 0 and SECTION % tn == 0
    rows = min(tm, 256)
    nseq = seq // tm
    return pl.pallas_call(
        functools.partial(_proj_kernel, tn=tn, rows=rows),
        grid=(t // tm, n // tn),
        in_specs=[
            pl.BlockSpec((tm, d), lambda i, j: (i, 0)),
            pl.BlockSpec((1, d), lambda i, j: (0, 0)),
            pl.BlockSpec((d, tn), lambda i, j: (0, j)),
            pl.BlockSpec((tm, HEAD_DIM), lambda i, j: (i % nseq, 0)),
            pl.BlockSpec((tm, HEAD_DIM), lambda i, j: (i % nseq, 0)),
        ],
        out_specs=pl.BlockSpec((tm, tn), lambda i, j: (i, j)),
        out_shape=jax.ShapeDtypeStruct((t, n), BF16),
        scratch_shapes=[pltpu.VMEM((tm, d), BF16)],
        compiler_params=_cparams(("arbitrary", "arbitrary")),
        name="proj",
    )(x2d, gain, w_in, cos, sin)


def _sb_kernel(q_ref, k_ref, v_ref, tri_ref, g_ref, o_ref, acc_ref, *, tb, nsub):
    qi = pl.program_id(2)
    tq = nsub * tb
    lo, hi = slice(0, tb), slice(tb, tq)
    row = lax.broadcasted_iota(jnp.int32, (tb, tb), 0)
    col = lax.broadcasted_iota(jnp.int32, (tb, tb), 1)
    strict = col < row

    def scores(rows, kj):
        start = pl.multiple_of(kj * tb, tb)
        return _dot_nt(q_ref[0, rows, :], k_ref[0, pl.ds(start, tb), :])

    def keep_logs(n, mask):
        lks = []
        for r in range(0, n.shape[0], ROW_CHUNK):
            nc = n[r:r + ROW_CHUNK]
            lk = jnp.minimum(nc, 0.0) - jnp.log2(1.0 + jnp.exp2(-jnp.abs(nc)))
            if mask is not None:
                lk = jnp.where(mask[r:r + ROW_CHUNK], lk, 0.0)
            lks.append(lk.astype(BF16))
        return jnp.concatenate(lks, axis=0)

    def accumulate(rows, kj, n, lk, carry, mask):
        start = pl.multiple_of(kj * tb, tb)
        cum = _dot(lk, tri_ref[...])
        parts = []
        for r in range(0, n.shape[0], ROW_CHUNK):
            chunk = slice(r, r + ROW_CHUNK)
            a = jnp.exp2(cum[chunk] + carry[chunk] - n[chunk])
            if mask is not None:
                a = jnp.where(mask[chunk], a, 0.0)
            parts.append(a.astype(BF16))
        acc_ref[rows, :] += _dot(jnp.concatenate(parts, axis=0), v_ref[0, pl.ds(start, tb), :])
        return carry + cum[:, 0:1]

    def several(rows, kjs, carry, mask=None):
        staged = []
        for kj in kjs:
            n = scores(rows, kj)
            staged.append((kj, n, keep_logs(n, mask)))
        for kj, n, lk in staged:
            carry = accumulate(rows, kj, n, lk, carry, mask)
        return carry

    acc_ref[...] = jnp.zeros_like(acc_ref)
    base = qi * nsub
    zero = jnp.zeros((tb, 1), F32)
    prev = jnp.maximum(base - 1, 0)
    has_prev = jnp.broadcast_to(base > 0, (tb, tb))
    chains = [(1, base + 1, strict), (0, base, strict), (1, base, None), (0, prev, has_prev)]
    halves = [lo, hi]
    staged = []
    for half, kj, mask in chains:
        n = scores(halves[half], kj)
        staged.append((n, keep_logs(n, mask)))
    carry = [zero, zero]
    for (half, kj, mask), (n, lk) in zip(chains, staged):
        carry[half] = accumulate(halves[half], kj, n, lk, carry[half], mask)

    def walk_back(rows, first, carry):
        def still_live(kj, c):
            return (kj >= 0) & (jnp.max(c) > UNDERFLOW_LOG2)

        def one_block(state):
            kj, c, _ = state
            c = several(rows, [kj], c)
            return kj - 1, c, still_live(kj - 1, c)

        lax.while_loop(lambda state: state[2], one_block, (first, carry, still_live(first, carry)))

    walk_back(hi, base - 1, carry[1])
    walk_back(lo, base - 2, carry[0])

    o = acc_ref[...]
    ms = jnp.mean(o * o, axis=-1, keepdims=True)
    o_ref[0] = (o * lax.rsqrt(ms + NORM_EPS) * g_ref[...]).astype(BF16)


def _sb_call(proj3d, tri, gain, *, tb, nsub):
    b, s, _ = proj3d.shape
    tq = nsub * tb
    assert s % tq == 0 and nsub == 2 and tb % ROW_CHUNK == 0
    hq, hk, hv = 0, SECTION // HEAD_DIM, 2 * SECTION // HEAD_DIM
    return pl.pallas_call(
        functools.partial(_sb_kernel, tb=tb, nsub=nsub),
        grid=(b, N_SB_HEADS, s // tq),
        in_specs=[
            pl.BlockSpec((1, tq, HEAD_DIM), lambda bi, h, i: (bi, i, hq + h)),
            pl.BlockSpec((1, s, HEAD_DIM), lambda bi, h, i: (bi, 0, hk + h)),
            pl.BlockSpec((1, s, HEAD_DIM), lambda bi, h, i: (bi, 0, hv + h)),
            pl.BlockSpec((tb, tb), lambda bi, h, i: (0, 0)),
            pl.BlockSpec((1, HEAD_DIM), lambda bi, h, i: (0, 0)),
        ],
        out_specs=pl.BlockSpec((1, tq, HEAD_DIM), lambda bi, h, i: (bi, i, h)),
        out_shape=jax.ShapeDtypeStruct((b, s, N_SB_HEADS * HEAD_DIM), BF16),
        scratch_shapes=[pltpu.VMEM((tq, HEAD_DIM), F32)],
        compiler_params=_cparams(("arbitrary", "arbitrary", "arbitrary")),
        name="sb_attn",
    )(proj3d, proj3d, proj3d, tri, gain)


def _diff_kernel(q_ref, k_ref, v_ref, lq1_ref, lk1_ref, lq2_ref, lk2_ref, g_ref, o_ref,
                 acc1_ref, acc2_ref, *, tq, lam_init):
    qi = pl.program_id(2)
    d = HEAD_DIM
    q1 = q_ref[0, :, :d]
    q2 = q_ref[0, :, d:]
    row = lax.broadcasted_iota(jnp.int32, (tq, tq), 0)
    col = lax.broadcasted_iota(jnp.int32, (tq, tq), 1)
    visible = (col // CHUNK) <= (row // CHUNK)

    def probabilities(s, m, l):
        m_new = jnp.maximum(m, jnp.max(s, axis=-1, keepdims=True))
        alpha = jnp.exp2(m - m_new)
        p = jnp.exp2(s - m_new)
        return p.astype(BF16), alpha, m_new, alpha * l + jnp.sum(p, axis=-1, keepdims=True)

    def block(start, width, carry, masked):
        m1, l1, m2, l2 = carry
        k = k_ref[0, pl.ds(start, width), :]
        v = v_ref[0, pl.ds(start, width), :]
        s1 = _dot_nt(q1, k[:, :d])
        s2 = _dot_nt(q2, k[:, d:])
        if masked:
            s1 = jnp.where(visible, s1, NEG_INF)
            s2 = jnp.where(visible, s2, NEG_INF)
        p1, alpha1, m1, l1 = probabilities(s1, m1, l1)
        p2, alpha2, m2, l2 = probabilities(s2, m2, l2)
        pv = _dot(jnp.concatenate([p1, p2], axis=0), v)
        acc1_ref[...] = alpha1 * acc1_ref[...] + pv[:tq]
        acc2_ref[...] = alpha2 * acc2_ref[...] + pv[tq:]
        return m1, l1, m2, l2

    acc1_ref[...] = jnp.zeros_like(acc1_ref)
    acc2_ref[...] = jnp.zeros_like(acc2_ref)
    neg = jnp.full((tq, 1), NEG_INF, F32)
    zero = jnp.zeros((tq, 1), F32)
    carry = block(pl.multiple_of(qi * tq, tq), tq, (neg, zero, neg, zero), True)
    odd = qi % 2
    carry = lax.cond(odd == 1, lambda c: block(pl.multiple_of((qi - 1) * tq, tq), tq, c, False),
                     lambda c: c, carry)
    _, l1, _, l2 = lax.fori_loop(
        0, qi // 2, lambda i, c: block(pl.multiple_of(2 * i * tq, 2 * tq), 2 * tq, c, False), carry)

    lam = (jnp.exp(jnp.sum(lq1_ref[...] * lk1_ref[...], axis=-1, keepdims=True))
           - jnp.exp(jnp.sum(lq2_ref[...] * lk2_ref[...], axis=-1, keepdims=True)) + lam_init)
    o = acc1_ref[...] / l1 - lam * (acc2_ref[...] / l2)
    ms = jnp.mean(o * o, axis=-1, keepdims=True)
    o_ref[0] = (o * lax.rsqrt(ms + NORM_EPS) * g_ref[...] * (1.0 - lam_init)).astype(BF16)


def _diff_call(proj3d, lq1, lk1, lq2, lk2, gain, lam_init, *, tq):
    b, s, _ = proj3d.shape
    assert s % tq == 0 and tq % CHUNK == 0
    w = 2 * HEAD_DIM
    hq, hk, hv = 3 * SECTION // w, 4 * SECTION // w, 5 * SECTION // w
    vec = pl.BlockSpec((1, HEAD_DIM), lambda bi, h, i: (0, 0))
    return pl.pallas_call(
        functools.partial(_diff_kernel, tq=tq, lam_init=lam_init),
        grid=(b, N_DIFF_HEADS, s // tq),
        in_specs=[
            pl.BlockSpec((1, tq, w), lambda bi, h, i: (bi, i, hq + h)),
            pl.BlockSpec((1, s, w), lambda bi, h, i: (bi, 0, hk + h)),
            pl.BlockSpec((1, s, w), lambda bi, h, i: (bi, 0, hv + h)),
            vec, vec, vec, vec,
            pl.BlockSpec((1, w), lambda bi, h, i: (0, 0)),
        ],
        out_specs=pl.BlockSpec((1, tq, w), lambda bi, h, i: (bi, i, h)),
        out_shape=jax.ShapeDtypeStruct((b, s, N_DIFF_HEADS * w), BF16),
        scratch_shapes=[pltpu.VMEM((tq, w), F32), pltpu.VMEM((tq, w), F32)],
        compiler_params=_cparams(("arbitrary", "arbitrary", "arbitrary")),
        name="diff_attn",
    )(proj3d, proj3d, proj3d, lq1, lk1, lq2, lk2, gain)


def _outproj_kernel(x_ref, sb_ref, df_ref, wo_ref, g_ref, wr_ref, br_ref, tri_ref,
                    x1_ref, h2_ref, ri_ref, rf_ref, cnt_ref, wo_bf16_ref):
    i = pl.program_id(0)
    tm = x_ref.shape[0]
    half = sb_ref.shape[1]

    @pl.when(i == 0)
    def _():
        wo_bf16_ref[...] = wo_ref[...].astype(BF16)

    x1 = (x_ref[...] + _dot(sb_ref[...], wo_bf16_ref[:half, :])
          + _dot(df_ref[...], wo_bf16_ref[half:, :]))
    x1_ref[...] = x1
    ms = jnp.mean(x1 * x1, axis=-1, keepdims=True)
    h2 = x1 * lax.rsqrt(ms + NORM_EPS) * g_ref[...]
    h2_ref[...] = h2

    logits = _dot(h2.astype(BF16), wr_ref[...]) + br_ref[...]
    lane = lax.broadcasted_iota(jnp.int32, (tm, LANES), 1).astype(F32)
    ninf = -jnp.inf

    def first_argmax(vals):
        top = jnp.max(vals, axis=-1, keepdims=True)
        idx = jnp.min(jnp.where(vals == top, lane, float(LANES)), axis=-1, keepdims=True)
        return top, idx

    gl = jnp.where(lane < N_GROUPS, logits, ninf)
    gmax, gidx = first_argmax(gl)
    g_val = 1.0 / jnp.sum(jnp.exp(gl - gmax), axis=-1, keepdims=True)
    lo = N_GROUPS + EXPERTS_PER_GROUP * gidx
    el = jnp.where((lane >= lo) & (lane < lo + EXPERTS_PER_GROUP), logits, ninf)
    l1, i1 = first_argmax(el)
    l2, i2 = first_argmax(jnp.where(lane == i1, ninf, el))
    r = jnp.exp(l2 - l1)
    w1 = g_val / (1.0 + r)
    w2 = g_val * r / (1.0 + r)
    e1 = i1 - N_GROUPS
    e2 = i2 - N_GROUPS

    @pl.when(i == 0)
    def _():
        cnt_ref[...] = jnp.zeros_like(cnt_ref)

    onehot = jnp.where((lane == e1) | (lane == e2), 1.0, 0.0)
    before = _dot(tri_ref[...], onehot.astype(BF16)) + cnt_ref[0:1, :]
    rank1 = jnp.sum(jnp.where(lane == e1, before, 0.0), axis=-1, keepdims=True)
    rank2 = jnp.sum(jnp.where(lane == e2, before, 0.0), axis=-1, keepdims=True)
    cnt_ref[0:1, :] = cnt_ref[0:1, :] + jnp.sum(onehot, axis=0, keepdims=True)

    ri = jnp.where(lane == 0, e1, jnp.where(lane == 1, e2,
                   jnp.where(lane == 2, rank1, jnp.where(lane == 3, rank2, 0.0))))
    ri_ref[0] = jnp.transpose(ri)[:8].astype(jnp.int32)
    rf_ref[...] = jnp.where(lane == 0, w1, jnp.where(lane == 1, w2, 0.0))


def _outproj_call(x2d, sb_out, d_out, w_out, gain, w_router, b_router, tri, *, tm):
    t, d = x2d.shape
    half = sb_out.shape[1]
    assert t % tm == 0
    row = lambda i: (i, 0)
    fixed = lambda i: (0, 0)
    return pl.pallas_call(
        _outproj_kernel,
        grid=(t // tm,),
        in_specs=[
            pl.BlockSpec((tm, d), row),
            pl.BlockSpec((tm, half), row),
            pl.BlockSpec((tm, half), row),
            pl.BlockSpec((2 * half, d), fixed, pipeline_mode=pl.Buffered(1)),
            pl.BlockSpec((1, d), fixed),
            pl.BlockSpec((d, LANES), fixed),
            pl.BlockSpec((1, LANES), fixed),
            pl.BlockSpec((tm, tm), fixed),
        ],
        out_specs=[
            pl.BlockSpec((tm, d), row),
            pl.BlockSpec((tm, d), row),
            pl.BlockSpec((1, 8, tm), lambda i: (i, 0, 0)),
            pl.BlockSpec((tm, LANES), row),
            pl.BlockSpec((8, LANES), fixed),
        ],
        out_shape=[
            jax.ShapeDtypeStruct((t, d), F32),
            jax.ShapeDtypeStruct((t, d), F32),
            jax.ShapeDtypeStruct((t // tm, 8, tm), jnp.int32),
            jax.ShapeDtypeStruct((t, LANES), F32),
            jax.ShapeDtypeStruct((8, LANES), F32),
        ],
        scratch_shapes=[pltpu.VMEM((2 * half, d), BF16)],
        compiler_params=_cparams(("arbitrary",)),
        name="outproj_router",
    )(x2d, sb_out, d_out, w_out, gain, w_router, b_router, tri)


def _dispatch_kernel(pos_hbm, h_ref, xs_hbm, pos_smem, sem_idx, sem, *, td):
    i = pl.program_id(0)
    idx_copy = pltpu.make_async_copy(pos_hbm.at[i], pos_smem, sem_idx)
    idx_copy.start()
    idx_copy.wait()

    def row_copy(t, k):
        return pltpu.make_async_copy(h_ref.at[pl.ds(t, 1)],
                                     xs_hbm.at[pl.ds(pos_smem[0, k * td + t], 1)], sem)

    def issue(t, c):
        row_copy(t, 0).start()
        row_copy(t, 1).start()
        return c

    lax.fori_loop(0, td, issue, 0, unroll=DMA_UNROLL)

    def drain(t, c):
        row_copy(t, 0).wait()
        row_copy(t, 1).wait()
        return c

    lax.fori_loop(0, td, drain, 0, unroll=DMA_UNROLL)


def _dispatch_call(pos3, h2, n_rows):
    t, d = h2.shape
    td = pos3.shape[2] // 2
    assert pos3.shape[0] * td == t
    return pl.pallas_call(
        functools.partial(_dispatch_kernel, td=td),
        grid=(t // td,),
        in_specs=[pl.BlockSpec(memory_space=pl.ANY), pl.BlockSpec((td, d), lambda i: (i, 0))],
        out_specs=pl.BlockSpec(memory_space=pl.ANY),
        out_shape=jax.ShapeDtypeStruct((n_rows, d), h2.dtype),
        scratch_shapes=[pltpu.SMEM((1, 2 * td), jnp.int32), pltpu.SemaphoreType.DMA,
                        pltpu.SemaphoreType.DMA],
        compiler_params=_cparams(("arbitrary",)),
        name="dispatch",
    )(pos3, h2)


def _experts_kernel(wt_ref, we_ref, nw_ref, lo_ref, hi_ref, xs_ref, wg_ref, wu_ref, wd_ref, ys_ref):
    w = pl.program_id(0)
    tx = xs_ref.shape[0]
    tile = wt_ref[w]
    first = (w == 0) | (wt_ref[jnp.maximum(w - 1, 0)] != tile)

    @pl.when(w < nw_ref[0])
    def _():
        e = we_ref[w]
        rows = tile * tx + lax.broadcasted_iota(jnp.int32, (tx, 1), 0)
        member = (rows >= lo_ref[e]) & (rows < hi_ref[e])
        x = xs_ref[...].astype(BF16)
        gate = _dot(x, wg_ref[0].astype(BF16))
        up = _dot(x, wu_ref[0].astype(BF16))
        hid = gate * (1.0 / (1.0 + jnp.exp(-gate))) * up
        y = _dot(jnp.where(member, hid, 0.0).astype(BF16), wd_ref[0].astype(BF16))

        @pl.when(first)
        def _():
            ys_ref[...] = y

        @pl.when(jnp.logical_not(first))
        def _():
            ys_ref[...] += y


def _experts_call(work_tile, work_expert, n_work, seg_lo, seg_hi, xs, w_gate, w_up, w_down, *, tx):
    p, d = xs.shape
    de = w_gate.shape[2]
    assert p % tx == 0
    n_items = work_tile.shape[0]
    tile = lambda w, wt, we, nw, lo, hi: (wt[w], 0)
    expert = lambda w, wt, we, nw, lo, hi: (we[w], 0, 0)
    return pl.pallas_call(
        _experts_kernel,
        grid_spec=pltpu.PrefetchScalarGridSpec(
            num_scalar_prefetch=5,
            grid=(n_items,),
            in_specs=[
                pl.BlockSpec((tx, d), tile),
                pl.BlockSpec((1, d, de), expert),
                pl.BlockSpec((1, d, de), expert),
                pl.BlockSpec((1, de, d), expert),
            ],
            out_specs=pl.BlockSpec((tx, d), tile),
        ),
        out_shape=jax.ShapeDtypeStruct((p, d), F32),
        compiler_params=_cparams(("arbitrary",)),
        name="experts",
    )(work_tile, work_expert, n_work, seg_lo, seg_hi, xs, w_gate, w_up, w_down)


def _combine_kernel(pos_hbm, x1_ref, rf_ref, g_ref, ys_hbm, o_ref, pos_smem, y0_ref, y1_ref,
                    sem_idx, sem, *, tc):
    i = pl.program_id(0)
    idx_copy = pltpu.make_async_copy(pos_hbm.at[i], pos_smem, sem_idx)
    idx_copy.start()
    idx_copy.wait()

    def row_copy(t, k):
        dst = y0_ref if k == 0 else y1_ref
        return pltpu.make_async_copy(ys_hbm.at[pl.ds(pos_smem[0, k * tc + t], 1)],
                                     dst.at[pl.ds(t, 1)], sem)

    def issue(t, c):
        row_copy(t, 0).start()
        row_copy(t, 1).start()
        return c

    lax.fori_loop(0, tc, issue, 0, unroll=DMA_UNROLL)

    def drain(t, c):
        row_copy(t, 0).wait()
        row_copy(t, 1).wait()
        return c

    lax.fori_loop(0, tc, drain, 0, unroll=DMA_UNROLL)

    x = x1_ref[...] + rf_ref[:, 0:1] * y0_ref[...] + rf_ref[:, 1:2] * y1_ref[...]
    ms = jnp.mean(x * x, axis=-1, keepdims=True)
    o_ref[...] = x * lax.rsqrt(ms + NORM_EPS) * g_ref[...]


def _combine_call(pos3, x1, rf, gain, ys):
    t, d = x1.shape
    tc = pos3.shape[2] // 2
    assert pos3.shape[0] * tc == t
    row = lambda i: (i, 0)
    return pl.pallas_call(
        functools.partial(_combine_kernel, tc=tc),
        grid=(t // tc,),
        in_specs=[
            pl.BlockSpec(memory_space=pl.ANY),
            pl.BlockSpec((tc, d), row),
            pl.BlockSpec((tc, LANES), row),
            pl.BlockSpec((1, d), lambda i: (0, 0)),
            pl.BlockSpec(memory_space=pl.ANY),
        ],
        out_specs=pl.BlockSpec((tc, d), row),
        out_shape=jax.ShapeDtypeStruct((t, d), F32),
        scratch_shapes=[pltpu.SMEM((1, 2 * tc), jnp.int32), pltpu.VMEM((tc, d), F32),
                        pltpu.VMEM((tc, d), F32), pltpu.SemaphoreType.DMA,
                        pltpu.SemaphoreType.DMA],
        compiler_params=_cparams(("arbitrary",)),
        name="combine",
    )(pos3, x1, rf, gain, ys)


def _pick(n, pref):
    while n % pref:
        pref //= 2
    return pref


def _layer(x2d, b, s, layer, attn_norm_gain, w_in, sb_norm_gain, lq1, lk1, lq2, lk2, subln_gain,
           w_out, ffn_norm_gain, w_gr, b_gr, w_er, b_er, w_gate, w_up, w_down):
    t, d = x2d.shape
    tb = _pick(s, 256)
    tm_proj = _pick(s, 1024)
    tm_out = _pick(t, 256)
    tx = 256

    half = HEAD_DIM // 2
    inv_freq = 1.0 / (ROPE_THETA ** (np.arange(half, dtype=np.float64) / half))
    ang = np.arange(s, dtype=np.float64)[:, None] * inv_freq[None, :]
    cos = jnp.asarray(np.concatenate([np.cos(ang), np.cos(ang)], axis=-1), F32)
    sin = jnp.asarray(np.concatenate([-np.sin(ang), np.sin(ang)], axis=-1), F32)

    proj = _proj_call(x2d, attn_norm_gain.reshape(1, d), w_in, cos, sin, s,
                      tm=tm_proj, tn=512)
    proj3d = proj.reshape(b, s, proj.shape[1])

    r = jnp.arange(tb)
    tri_suffix = (r[:, None] >= r[None, :]).astype(BF16)
    sb_out = _sb_call(proj3d, tri_suffix, sb_norm_gain.reshape(1, HEAD_DIM), tb=tb, nsub=2)

    lam_init = 0.8 - 0.6 * math.exp(-0.3 * layer)
    d_out = _diff_call(proj3d, lq1.reshape(1, -1), lk1.reshape(1, -1), lq2.reshape(1, -1),
                       lk2.reshape(1, -1), subln_gain.reshape(1, -1), lam_init, tq=2 * tb)

    w_router = jnp.zeros((d, LANES), F32)
    w_router = w_router.at[:, :N_GROUPS].set(w_gr).at[:, N_GROUPS:N_GROUPS + N_EXPERTS].set(w_er)
    b_router = jnp.zeros((1, LANES), F32)
    b_router = b_router.at[0, :N_GROUPS].set(b_gr).at[0, N_GROUPS:N_GROUPS + N_EXPERTS].set(b_er)
    rr = jnp.arange(tm_out)
    tri_before = (rr[None, :] < rr[:, None]).astype(BF16)
    x1, h2, ri, rf, cnt = _outproj_call(
        x2d, sb_out.reshape(t, -1), d_out.reshape(t, -1), w_out,
        ffn_norm_gain.reshape(1, d), w_router.astype(BF16), b_router, tri_before, tm=tm_out)

    counts = cnt[0, :N_EXPERTS].astype(jnp.int32)
    seg_hi = jnp.cumsum(counts)
    seg_lo = seg_hi - counts
    experts = jnp.arange(N_EXPERTS, dtype=jnp.int32)
    ri = ri.transpose(1, 0, 2).reshape(8, t)

    def sorted_position(slot):
        hit = ri[slot][None, :] == experts[:, None]
        return jnp.sum(jnp.where(hit, seg_lo[:, None], 0), axis=0) + ri[2 + slot]

    def tiled_positions(tile):
        return jnp.stack([sorted_position(0).reshape(-1, tile), sorted_position(1).reshape(-1, tile)],
                         axis=1).reshape(-1, 1, 2 * tile)

    first_tile = seg_lo // tx
    items = jnp.where(counts > 0, (seg_hi - 1) // tx - first_tile + 1, 0)
    item_hi = jnp.cumsum(items)
    n_work = item_hi[-1]
    w = jnp.minimum(jnp.arange(2 * t // tx + N_EXPERTS - 1, dtype=jnp.int32), n_work - 1)
    work_expert = jnp.sum((item_hi[None, :] <= w[:, None]).astype(jnp.int32), axis=1)
    work_tile = (first_tile[work_expert] + w - (item_hi - items)[work_expert]).astype(jnp.int32)

    xs = _dispatch_call(tiled_positions(_pick(t, 1024)), h2, 2 * t)
    ys = _experts_call(work_tile, work_expert, n_work.reshape(1).astype(jnp.int32), seg_lo, seg_hi,
                       xs, w_gate, w_up, w_down, tx=tx)
    return x1, tiled_positions(_pick(t, 512)), rf, ys


def kernel(x, attn_norm_gain, w_in, sb_norm_gain, diff_lambda_q1, diff_lambda_k1, diff_lambda_q2,
           diff_lambda_k2, diff_subln_gain, w_out, ffn_norm_gain, w_group_router, b_group_router,
           w_expert_router, b_expert_router, w_gate, w_up, w_down, final_norm_gain):
    b, s, d = x.shape
    assert w_in.shape[0] == 1, "the combine stage fuses the final norm: single-layer stacks only"
    layer = 0
    x2d = x.reshape(b * s, d)
    x1, pos, rf, ys = _layer(
        x2d, b, s, layer, attn_norm_gain[layer], w_in[layer], sb_norm_gain[layer],
        diff_lambda_q1[layer], diff_lambda_k1[layer], diff_lambda_q2[layer],
        diff_lambda_k2[layer], diff_subln_gain[layer], w_out[layer], ffn_norm_gain[layer],
        w_group_router[layer], b_group_router[layer], w_expert_router[layer],
        b_expert_router[layer], w_gate[layer], w_up[layer], w_down[layer])
    out = _combine_call(pos, x1, rf, final_norm_gain.reshape(1, d), ys)
    return out.reshape(b, s, d)
```

```python
import functools
import math

import jax
import jax.numpy as jnp
import numpy as np
from jax import lax
from jax.experimental import pallas as pl
from jax.experimental.pallas import tpu as pltpu

F32 = jnp.float32
BF16 = jnp.bfloat16

HEAD_DIM = 128
N_SB_HEADS = 8
N_DIFF_HEADS = 4
SECTION = 1024
CHUNK = 64
ROPE_THETA = 10000.0
N_GROUPS = 4
EXPERTS_PER_GROUP = 8
N_EXPERTS = N_GROUPS * EXPERTS_PER_GROUP
NORM_EPS = 1e-6
NEG_INF = -1e30
LANES = 128
ROW_CHUNK = 32
UNDERFLOW_LOG2 = -200.0
DMA_UNROLL = 8
SCALE_LOG2E = math.log2(math.e) / math.sqrt(HEAD_DIM)

VMEM_LIMIT = 56 * 1024 * 1024


def _cparams(sem):
    return pltpu.CompilerParams(dimension_semantics=sem, vmem_limit_bytes=VMEM_LIMIT)


def _dot(a, b):
    return jnp.dot(a, b, preferred_element_type=F32)


def _dot_nt(a, b):
    return lax.dot_general(a, b, (((1,), (1,)), ((), ())), preferred_element_type=F32)


def _proj_kernel(x_ref, g_ref, w_ref, cos_ref, sin_ref, o_ref, h_ref, *, tn, rows):
    j = pl.program_id(1)
    tm = x_ref.shape[0]

    @pl.when(j == 0)
    def _():
        for r in range(0, tm, rows):
            x = x_ref[r:r + rows, :]
            ms = jnp.mean(x * x, axis=-1, keepdims=True)
            h_ref[r:r + rows, :] = (x * lax.rsqrt(ms + NORM_EPS) * g_ref[...]).astype(BF16)

    sec = j // (SECTION // tn)

    def chunks(epilogue):
        for r in range(0, tm, rows):
            epilogue(slice(r, r + rows), _dot(h_ref[r:r + rows, :], w_ref[...].astype(BF16)))

    def rotary(scale):
        def epilogue(rs, acc):
            for c in range(tn // HEAD_DIM):
                cs = slice(c * HEAD_DIM, (c + 1) * HEAD_DIM)
                a = acc[:, cs]
                rot = a * cos_ref[rs, :] + pltpu.roll(a, HEAD_DIM // 2, 1) * sin_ref[rs, :]
                if scale != 1.0:
                    rot = rot * scale
                o_ref[rs, cs] = rot.astype(BF16)
        chunks(epilogue)

    def scaled(scale):
        def epilogue(rs, acc):
            o_ref[rs, :] = (acc if scale == 1.0 else acc * scale).astype(BF16)
        chunks(epilogue)

    @pl.when(sec == 0)
    def _():
        scaled(-SCALE_LOG2E)

    @pl.when((sec == 1) | (sec == 2) | (sec == 5))
    def _():
        scaled(1.0)

    @pl.when(sec == 3)
    def _():
        rotary(SCALE_LOG2E)

    @pl.when(sec == 4)
    def _():
        rotary(1.0)


def _proj_call(x2d, gain, w_in, cos, sin, seq, *, tm, tn):
    t, d = x2d.shape
    n = w_in.shape[1]
    assert t % tm == 0 and seq % tm == 0 and n % tn == 0 and SECTION % tn == 0
    rows = min(tm, 256)
    nseq = seq // tm
    return pl.pallas_call(
        functools.partial(_proj_kernel, tn=tn, rows=rows),
        grid=(t // tm, n // tn),
        in_specs=[
            pl.BlockSpec((tm, d), lambda i, j: (i, 0)),
            pl.BlockSpec((1, d), lambda i, j: (0, 0)),
            pl.BlockSpec((d, tn), lambda i, j: (0, j)),
            pl.BlockSpec((tm, HEAD_DIM), lambda i, j: (i % nseq, 0)),
            pl.BlockSpec((tm, HEAD_DIM), lambda i, j: (i % nseq, 0)),
        ],
        out_specs=pl.BlockSpec((tm, tn), lambda i, j: (i, j)),
        out_shape=jax.ShapeDtypeStruct((t, n), BF16),
        scratch_shapes=[pltpu.VMEM((tm, d), BF16)],
        compiler_params=_cparams(("arbitrary", "arbitrary")),
        name="proj",
    )(x2d, gain, w_in, cos, sin)


def _sb_kernel(q_ref, k_ref, v_ref, tri_ref, g_ref, o_ref, acc_ref, *, tb, nsub):
    qi = pl.program_id(2)
    tq = nsub * tb
    lo, hi = slice(0, tb), slice(tb, tq)
    row = lax.broadcasted_iota(jnp.int32, (tb, tb), 0)
    col = lax.broadcasted_iota(jnp.int32, (tb, tb), 1)
    strict = col < row

    def scores(rows, kj):
        start = pl.multiple_of(kj * tb, tb)
        return _dot_nt(q_ref[0, rows, :], k_ref[0, pl.ds(start, tb), :])

    def keep_logs(n, mask):
        lks = []
        for r in range(0, n.shape[0], ROW_CHUNK):
            nc = n[r:r + ROW_CHUNK]
            lk = jnp.minimum(nc, 0.0) - jnp.log2(1.0 + jnp.exp2(-jnp.abs(nc)))
            if mask is not None:
                lk = jnp.where(mask[r:r + ROW_CHUNK], lk, 0.0)
            lks.append(lk.astype(BF16))
        return jnp.concatenate(lks, axis=0)

    def accumulate(rows, kj, n, lk, carry, mask):
        start = pl.multiple_of(kj * tb, tb)
        cum = _dot(lk, tri_ref[...])
        parts = []
        for r in range(0, n.shape[0], ROW_CHUNK):
            chunk = slice(r, r + ROW_CHUNK)
            a = jnp.exp2(cum[chunk] + carry[chunk] - n[chunk])
            if mask is not None:
                a = jnp.where(mask[chunk], a, 0.0)
            parts.append(a.astype(BF16))
        acc_ref[rows, :] += _dot(jnp.concatenate(parts, axis=0), v_ref[0, pl.ds(start, tb), :])
        return carry + cum[:, 0:1]

    def several(rows, kjs, carry, mask=None):
        staged = []
        for kj in kjs:
            n = scores(rows, kj)
            staged.append((kj, n, keep_logs(n, mask)))
        for kj, n, lk in staged:
            carry = accumulate(rows, kj, n, lk, carry, mask)
        return carry

    acc_ref[...] = jnp.zeros_like(acc_ref)
    base = qi * nsub
    zero = jnp.zeros((tb, 1), F32)
    prev = jnp.maximum(base - 1, 0)
    has_prev = jnp.broadcast_to(base > 0, (tb, tb))
    chains = [(1, base + 1, strict), (0, base, strict), (1, base, None), (0, prev, has_prev)]
    halves = [lo, hi]
    staged = []
    for half, kj, mask in chains:
        n = scores(halves[half], kj)
        staged.append((n, keep_logs(n, mask)))
    carry = [zero, zero]
    for (half, kj, mask), (n, lk) in zip(chains, staged):
        carry[half] = accumulate(halves[half], kj, n, lk, carry[half], mask)

    def walk_back(rows, first, carry):
        def still_live(kj, c):
            return (kj >= 0) & (jnp.max(c) > UNDERFLOW_LOG2)

        def one_block(state):
            kj, c, _ = state
            c = several(rows, [kj], c)
            return kj - 1, c, still_live(kj - 1, c)

        lax.while_loop(lambda state: state[2], one_block, (first, carry, still_live(first, carry)))

    walk_back(hi, base - 1, carry[1])
    walk_back(lo, base - 2, carry[0])

    o = acc_ref[...]
    ms = jnp.mean(o * o, axis=-1, keepdims=True)
    o_ref[0] = (o * lax.rsqrt(ms + NORM_EPS) * g_ref[...]).astype(BF16)


def _sb_call(proj3d, tri, gain, *, tb, nsub):
    b, s, _ = proj3d.shape
    tq = nsub * tb
    assert s % tq == 0 and nsub == 2 and tb % ROW_CHUNK == 0
    hq, hk, hv = 0, SECTION // HEAD_DIM, 2 * SECTION // HEAD_DIM
    return pl.pallas_call(
        functools.partial(_sb_kernel, tb=tb, nsub=nsub),
        grid=(b, N_SB_HEADS, s // tq),
        in_specs=[
            pl.BlockSpec((1, tq, HEAD_DIM), lambda bi, h, i: (bi, i, hq + h)),
            pl.BlockSpec((1, s, HEAD_DIM), lambda bi, h, i: (bi, 0, hk + h)),
            pl.BlockSpec((1, s, HEAD_DIM), lambda bi, h, i: (bi, 0, hv + h)),
            pl.BlockSpec((tb, tb), lambda bi, h, i: (0, 0)),
            pl.BlockSpec((1, HEAD_DIM), lambda bi, h, i: (0, 0)),
        ],
        out_specs=pl.BlockSpec((1, tq, HEAD_DIM), lambda bi, h, i: (bi, i, h)),
        out_shape=jax.ShapeDtypeStruct((b, s, N_SB_HEADS * HEAD_DIM), BF16),
        scratch_shapes=[pltpu.VMEM((tq, HEAD_DIM), F32)],
        compiler_params=_cparams(("arbitrary", "arbitrary", "arbitrary")),
        name="sb_attn",
    )(proj3d, proj3d, proj3d, tri, gain)


def _diff_kernel(q_ref, k_ref, v_ref, lq1_ref, lk1_ref, lq2_ref, lk2_ref, g_ref, o_ref,
                 acc1_ref, acc2_ref, *, tq, lam_init):
    qi = pl.program_id(2)
    d = HEAD_DIM
    q1 = q_ref[0, :, :d]
    q2 = q_ref[0, :, d:]
    row = lax.broadcasted_iota(jnp.int32, (tq, tq), 0)
    col = lax.broadcasted_iota(jnp.int32, (tq, tq), 1)
    visible = (col // CHUNK) <= (row // CHUNK)

    def probabilities(s, m, l):
        m_new = jnp.maximum(m, jnp.max(s, axis=-1, keepdims=True))
        alpha = jnp.exp2(m - m_new)
        p = jnp.exp2(s - m_new)
        return p.astype(BF16), alpha, m_new, alpha * l + jnp.sum(p, axis=-1, keepdims=True)

    def block(start, width, carry, masked):
        m1, l1, m2, l2 = carry
        k = k_ref[0, pl.ds(start, width), :]
        v = v_ref[0, pl.ds(start, width), :]
        s1 = _dot_nt(q1, k[:, :d])
        s2 = _dot_nt(q2, k[:, d:])
        if masked:
            s1 = jnp.where(visible, s1, NEG_INF)
            s2 = jnp.where(visible, s2, NEG_INF)
        p1, alpha1, m1, l1 = probabilities(s1, m1, l1)
        p2, alpha2, m2, l2 = probabilities(s2, m2, l2)
        pv = _dot(jnp.concatenate([p1, p2], axis=0), v)
        acc1_ref[...] = alpha1 * acc1_ref[...] + pv[:tq]
        acc2_ref[...] = alpha2 * acc2_ref[...] + pv[tq:]
        return m1, l1, m2, l2

    acc1_ref[...] = jnp.zeros_like(acc1_ref)
    acc2_ref[...] = jnp.zeros_like(acc2_ref)
    neg = jnp.full((tq, 1), NEG_INF, F32)
    zero = jnp.zeros((tq, 1), F32)
    carry = block(pl.multiple_of(qi * tq, tq), tq, (neg, zero, neg, zero), True)
    odd = qi % 2
    carry = lax.cond(odd == 1, lambda c: block(pl.multiple_of((qi - 1) * tq, tq), tq, c, False),
                     lambda c: c, carry)
    _, l1, _, l2 = lax.fori_loop(
        0, qi // 2, lambda i, c: block(pl.multiple_of(2 * i * tq, 2 * tq), 2 * tq, c, False), carry)

    lam = (jnp.exp(jnp.sum(lq1_ref[...] * lk1_ref[...], axis=-1, keepdims=True))
           - jnp.exp(jnp.sum(lq2_ref[...] * lk2_ref[...], axis=-1, keepdims=True)) + lam_init)
    o = acc1_ref[...] / l1 - lam * (acc2_ref[...] / l2)
    ms = jnp.mean(o * o, axis=-1, keepdims=True)
    o_ref[0] = (o * lax.rsqrt(ms + NORM_EPS) * g_ref[...] * (1.0 - lam_init)).astype(BF16)


def _diff_call(proj3d, lq1, lk1, lq2, lk2, gain, lam_init, *, tq):
    b, s, _ = proj3d.shape
    assert s % tq == 0 and tq % CHUNK == 0
    w = 2 * HEAD_DIM
    hq, hk, hv = 3 * SECTION // w, 4 * SECTION // w, 5 * SECTION // w
    vec = pl.BlockSpec((1, HEAD_DIM), lambda bi, h, i: (0, 0))
    return pl.pallas_call(
        functools.partial(_diff_kernel, tq=tq, lam_init=lam_init),
        grid=(b, N_DIFF_HEADS, s // tq),
        in_specs=[
            pl.BlockSpec((1, tq, w), lambda bi, h, i: (bi, i, hq + h)),
            pl.BlockSpec((1, s, w), lambda bi, h, i: (bi, 0, hk + h)),
            pl.BlockSpec((1, s, w), lambda bi, h, i: (bi, 0, hv + h)),
            vec, vec, vec, vec,
            pl.BlockSpec((1, w), lambda bi, h, i: (0, 0)),
        ],
        out_specs=pl.BlockSpec((1, tq, w), lambda bi, h, i: (bi, i, h)),
        out_shape=jax.ShapeDtypeStruct((b, s, N_DIFF_HEADS * w), BF16),
        scratch_shapes=[pltpu.VMEM((tq, w), F32), pltpu.VMEM((tq, w), F32)],
        compiler_params=_cparams(("arbitrary", "arbitrary", "arbitrary")),
        name="diff_attn",
    )(proj3d, proj3d, proj3d, lq1, lk1, lq2, lk2, gain)


def _outproj_kernel(x_ref, sb_ref, df_ref, wo_ref, g_ref, wr_ref, br_ref, tri_ref,
                    x1_ref, h2_ref, ri_ref, rf_ref, cnt_ref, wo_bf16_ref):
    i = pl.program_id(0)
    tm = x_ref.shape[0]
    half = sb_ref.shape[1]

    @pl.when(i == 0)
    def _():
        wo_bf16_ref[...] = wo_ref[...].astype(BF16)

    x1 = (x_ref[...] + _dot(sb_ref[...], wo_bf16_ref[:half, :])
          + _dot(df_ref[...], wo_bf16_ref[half:, :]))
    x1_ref[...] = x1
    ms = jnp.mean(x1 * x1, axis=-1, keepdims=True)
    h2 = x1 * lax.rsqrt(ms + NORM_EPS) * g_ref[...]
    h2_ref[...] = h2

    logits = _dot(h2.astype(BF16), wr_ref[...]) + br_ref[...]
    lane = lax.broadcasted_iota(jnp.int32, (tm, LANES), 1).astype(F32)
    ninf = -jnp.inf

    def first_argmax(vals):
        top = jnp.max(vals, axis=-1, keepdims=True)
        idx = jnp.min(jnp.where(vals == top, lane, float(LANES)), axis=-1, keepdims=True)
        return top, idx

    gl = jnp.where(lane < N_GROUPS, logits, ninf)
    gmax, gidx = first_argmax(gl)
    g_val = 1.0 / jnp.sum(jnp.exp(gl - gmax), axis=-1, keepdims=True)
    lo = N_GROUPS + EXPERTS_PER_GROUP * gidx
    el = jnp.where((lane >= lo) & (lane < lo + EXPERTS_PER_GROUP), logits, ninf)
    l1, i1 = first_argmax(el)
    l2, i2 = first_argmax(jnp.where(lane == i1, ninf, el))
    r = jnp.exp(l2 - l1)
    w1 = g_val / (1.0 + r)
    w2 = g_val * r / (1.0 + r)
    e1 = i1 - N_GROUPS
    e2 = i2 - N_GROUPS

    @pl.when(i == 0)
    def _():
        cnt_ref[...] = jnp.zeros_like(cnt_ref)

    onehot = jnp.where((lane == e1) | (lane == e2), 1.0, 0.0)
    before = _dot(tri_ref[...], onehot.astype(BF16)) + cnt_ref[0:1, :]
    rank1 = jnp.sum(jnp.where(lane == e1, before, 0.0), axis=-1, keepdims=True)
    rank2 = jnp.sum(jnp.where(lane == e2, before, 0.0), axis=-1, keepdims=True)
    cnt_ref[0:1, :] = cnt_ref[0:1, :] + jnp.sum(onehot, axis=0, keepdims=True)

    ri = jnp.where(lane == 0, e1, jnp.where(lane == 1, e2,
                   jnp.where(lane == 2, rank1, jnp.where(lane == 3, rank2, 0.0))))
    ri_ref[0] = jnp.transpose(ri)[:8].astype(jnp.int32)
    rf_ref[...] = jnp.where(lane == 0, w1, jnp.where(lane == 1, w2, 0.0))


def _outproj_call(x2d, sb_out, d_out, w_out, gain, w_router, b_router, tri, *, tm):
    t, d = x2d.shape
    half = sb_out.shape[1]
    assert t % tm == 0
    row = lambda i: (i, 0)
    fixed = lambda i: (0, 0)
    return pl.pallas_call(
        _outproj_kernel,
        grid=(t // tm,),
        in_specs=[
            pl.BlockSpec((tm, d), row),
            pl.BlockSpec((tm, half), row),
            pl.BlockSpec((tm, half), row),
            pl.BlockSpec((2 * half, d), fixed, pipeline_mode=pl.Buffered(1)),
            pl.BlockSpec((1, d), fixed),
            pl.BlockSpec((d, LANES), fixed),
            pl.BlockSpec((1, LANES), fixed),
            pl.BlockSpec((tm, tm), fixed),
        ],
        out_specs=[
            pl.BlockSpec((tm, d), row),
            pl.BlockSpec((tm, d), row),
            pl.BlockSpec((1, 8, tm), lambda i: (i, 0, 0)),
            pl.BlockSpec((tm, LANES), row),
            pl.BlockSpec((8, LANES), fixed),
        ],
        out_shape=[
            jax.ShapeDtypeStruct((t, d), F32),
            jax.ShapeDtypeStruct((t, d), F32),
            jax.ShapeDtypeStruct((t // tm, 8, tm), jnp.int32),
            jax.ShapeDtypeStruct((t, LANES), F32),
            jax.ShapeDtypeStruct((8, LANES), F32),
        ],
        scratch_shapes=[pltpu.VMEM((2 * half, d), BF16)],
        compiler_params=_cparams(("arbitrary",)),
        name="outproj_router",
    )(x2d, sb_out, d_out, w_out, gain, w_router, b_router, tri)


def _dispatch_kernel(pos_hbm, h_ref, xs_hbm, pos_smem, sem_idx, sem, *, td):
    i = pl.program_id(0)
    idx_copy = pltpu.make_async_copy(pos_hbm.at[i], pos_smem, sem_idx)
    idx_copy.start()
    idx_copy.wait()

    def row_copy(t, k):
        return pltpu.make_async_copy(h_ref.at[pl.ds(t, 1)],
                                     xs_hbm.at[pl.ds(pos_smem[0, k * td + t], 1)], sem)

    def issue(t, c):
        row_copy(t, 0).start()
        row_copy(t, 1).start()
        return c

    lax.fori_loop(0, td, issue, 0, unroll=DMA_UNROLL)

    def drain(t, c):
        row_copy(t, 0).wait()
        row_copy(t, 1).wait()
        return c

    lax.fori_loop(0, td, drain, 0, unroll=DMA_UNROLL)


def _dispatch_call(pos3, h2, n_rows):
    t, d = h2.shape
    td = pos3.shape[2] // 2
    assert pos3.shape[0] * td == t
    return pl.pallas_call(
        functools.partial(_dispatch_kernel, td=td),
        grid=(t // td,),
        in_specs=[pl.BlockSpec(memory_space=pl.ANY), pl.BlockSpec((td, d), lambda i: (i, 0))],
        out_specs=pl.BlockSpec(memory_space=pl.ANY),
        out_shape=jax.ShapeDtypeStruct((n_rows, d), h2.dtype),
        scratch_shapes=[pltpu.SMEM((1, 2 * td), jnp.int32), pltpu.SemaphoreType.DMA,
                        pltpu.SemaphoreType.DMA],
        compiler_params=_cparams(("arbitrary",)),
        name="dispatch",
    )(pos3, h2)


def _experts_kernel(wt_ref, we_ref, nw_ref, lo_ref, hi_ref, xs_ref, wg_ref, wu_ref, wd_ref, ys_ref):
    w = pl.program_id(0)
    tx = xs_ref.shape[0]
    tile = wt_ref[w]
    first = (w == 0) | (wt_ref[jnp.maximum(w - 1, 0)] != tile)

    @pl.when(w < nw_ref[0])
    def _():
        e = we_ref[w]
        rows = tile * tx + lax.broadcasted_iota(jnp.int32, (tx, 1), 0)
        member = (rows >= lo_ref[e]) & (rows < hi_ref[e])
        x = xs_ref[...].astype(BF16)
        gate = _dot(x, wg_ref[0].astype(BF16))
        up = _dot(x, wu_ref[0].astype(BF16))
        hid = gate * (1.0 / (1.0 + jnp.exp(-gate))) * up
        y = _dot(jnp.where(member, hid, 0.0).astype(BF16), wd_ref[0].astype(BF16))

        @pl.when(first)
        def _():
            ys_ref[...] = y

        @pl.when(jnp.logical_not(first))
        def _():
            ys_ref[...] += y


def _experts_call(work_tile, work_expert, n_work, seg_lo, seg_hi, xs, w_gate, w_up, w_down, *, tx):
    p, d = xs.shape
    de = w_gate.shape[2]
    assert p % tx == 0
    n_items = work_tile.shape[0]
    tile = lambda w, wt, we, nw, lo, hi: (wt[w], 0)
    expert = lambda w, wt, we, nw, lo, hi: (we[w], 0, 0)
    return pl.pallas_call(
        _experts_kernel,
        grid_spec=pltpu.PrefetchScalarGridSpec(
            num_scalar_prefetch=5,
            grid=(n_items,),
            in_specs=[
                pl.BlockSpec((tx, d), tile),
                pl.BlockSpec((1, d, de), expert),
                pl.BlockSpec((1, d, de), expert),
                pl.BlockSpec((1, de, d), expert),
            ],
            out_specs=pl.BlockSpec((tx, d), tile),
        ),
        out_shape=jax.ShapeDtypeStruct((p, d), F32),
        compiler_params=_cparams(("arbitrary",)),
        name="experts",
    )(work_tile, work_expert, n_work, seg_lo, seg_hi, xs, w_gate, w_up, w_down)


def _combine_kernel(pos_hbm, x1_ref, rf_ref, g_ref, ys_hbm, o_ref, pos_smem, y0_ref, y1_ref,
                    sem_idx, sem, *, tc):
    i = pl.program_id(0)
    idx_copy = pltpu.make_async_copy(pos_hbm.at[i], pos_smem, sem_idx)
    idx_copy.start()
    idx_copy.wait()

    def row_copy(t, k):
        dst = y0_ref if k == 0 else y1_ref
        return pltpu.make_async_copy(ys_hbm.at[pl.ds(pos_smem[0, k * tc + t], 1)],
                                     dst.at[pl.ds(t, 1)], sem)

    def issue(t, c):
        row_copy(t, 0).start()
        row_copy(t, 1).start()
        return c

    lax.fori_loop(0, tc, issue, 0, unroll=DMA_UNROLL)

    def drain(t, c):
        row_copy(t, 0).wait()
        row_copy(t, 1).wait()
        return c

    lax.fori_loop(0, tc, drain, 0, unroll=DMA_UNROLL)

    x = x1_ref[...] + rf_ref[:, 0:1] * y0_ref[...] + rf_ref[:, 1:2] * y1_ref[...]
    ms = jnp.mean(x * x, axis=-1, keepdims=True)
    o_ref[...] = x * lax.rsqrt(ms + NORM_EPS) * g_ref[...]


def _combine_call(pos3, x1, rf, gain, ys):
    t, d = x1.shape
    tc = pos3.shape[2] // 2
    assert pos3.shape[0] * tc == t
    row = lambda i: (i, 0)
    return pl.pallas_call(
        functools.partial(_combine_kernel, tc=tc),
        grid=(t // tc,),
        in_specs=[
            pl.BlockSpec(memory_space=pl.ANY),
            pl.BlockSpec((tc, d), row),
            pl.BlockSpec((tc, LANES), row),
            pl.BlockSpec((1, d), lambda i: (0, 0)),
            pl.BlockSpec(memory_space=pl.ANY),
        ],
        out_specs=pl.BlockSpec((tc, d), row),
        out_shape=jax.ShapeDtypeStruct((t, d), F32),
        scratch_shapes=[pltpu.SMEM((1, 2 * tc), jnp.int32), pltpu.VMEM((tc, d), F32),
                        pltpu.VMEM((tc, d), F32), pltpu.SemaphoreType.DMA,
                        pltpu.SemaphoreType.DMA],
        compiler_params=_cparams(("arbitrary",)),
        name="combine",
    )(pos3, x1, rf, gain, ys)


def _pick(n, pref):
    while n % pref:
        pref //= 2
    return pref


def _layer(x2d, b, s, layer, attn_norm_gain, w_in, sb_norm_gain, lq1, lk1, lq2, lk2, subln_gain,
           w_out, ffn_norm_gain, w_gr, b_gr, w_er, b_er, w_gate, w_up, w_down):
    t, d = x2d.shape
    tb = _pick(s, 256)
    tm_proj = _pick(s, 1024)
    tm_out = _pick(t, 256)
    tx = 256

    half = HEAD_DIM // 2
    inv_freq = 1.0 / (ROPE_THETA ** (np.arange(half, dtype=np.float64) / half))
    ang = np.arange(s, dtype=np.float64)[:, None] * inv_freq[None, :]
    cos = jnp.asarray(np.concatenate([np.cos(ang), np.cos(ang)], axis=-1), F32)
    sin = jnp.asarray(np.concatenate([-np.sin(ang), np.sin(ang)], axis=-1), F32)

    proj = _proj_call(x2d, attn_norm_gain.reshape(1, d), w_in, cos, sin, s,
                      tm=tm_proj, tn=512)
    proj3d = proj.reshape(b, s, proj.shape[1])

    r = jnp.arange(tb)
    tri_suffix = (r[:, None] >= r[None, :]).astype(BF16)
    sb_out = _sb_call(proj3d, tri_suffix, sb_norm_gain.reshape(1, HEAD_DIM), tb=tb, nsub=2)

    lam_init = 0.8 - 0.6 * math.exp(-0.3 * layer)
    d_out = _diff_call(proj3d, lq1.reshape(1, -1), lk1.reshape(1, -1), lq2.reshape(1, -1),
                       lk2.reshape(1, -1), subln_gain.reshape(1, -1), lam_init, tq=2 * tb)

    w_router = jnp.zeros((d, LANES), F32)
    w_router = w_router.at[:, :N_GROUPS].set(w_gr).at[:, N_GROUPS:N_GROUPS + N_EXPERTS].set(w_er)
    b_router = jnp.zeros((1, LANES), F32)
    b_router = b_router.at[0, :N_GROUPS].set(b_gr).at[0, N_GROUPS:N_GROUPS + N_EXPERTS].set(b_er)
    rr = jnp.arange(tm_out)
    tri_before = (rr[None, :] < rr[:, None]).astype(BF16)
    x1, h2, ri, rf, cnt = _outproj_call(
        x2d, sb_out.reshape(t, -1), d_out.reshape(t, -1), w_out,
        ffn_norm_gain.reshape(1, d), w_router.astype(BF16), b_router, tri_before, tm=tm_out)

    counts = cnt[0, :N_EXPERTS].astype(jnp.int32)
    seg_hi = jnp.cumsum(counts)
    seg_lo = seg_hi - counts
    experts = jnp.arange(N_EXPERTS, dtype=jnp.int32)
    ri = ri.transpose(1, 0, 2).reshape(8, t)

    def sorted_position(slot):
        hit = ri[slot][None, :] == experts[:, None]
        return jnp.sum(jnp.where(hit, seg_lo[:, None], 0), axis=0) + ri[2 + slot]

    def tiled_positions(tile):
        return jnp.stack([sorted_position(0).reshape(-1, tile), sorted_position(1).reshape(-1, tile)],
                         axis=1).reshape(-1, 1, 2 * tile)

    first_tile = seg_lo // tx
    items = jnp.where(counts > 0, (seg_hi - 1) // tx - first_tile + 1, 0)
    item_hi = jnp.cumsum(items)
    n_work = item_hi[-1]
    w = jnp.minimum(jnp.arange(2 * t // tx + N_EXPERTS - 1, dtype=jnp.int32), n_work - 1)
    work_expert = jnp.sum((item_hi[None, :] <= w[:, None]).astype(jnp.int32), axis=1)
    work_tile = (first_tile[work_expert] + w - (item_hi - items)[work_expert]).astype(jnp.int32)

    xs = _dispatch_call(tiled_positions(_pick(t, 1024)), h2, 2 * t)
    ys = _experts_call(work_tile, work_expert, n_work.reshape(1).astype(jnp.int32), seg_lo, seg_hi,
                       xs, w_gate, w_up, w_down, tx=tx)
    return x1, tiled_positions(_pick(t, 512)), rf, ys


def kernel(x, attn_norm_gain, w_in, sb_norm_gain, diff_lambda_q1, diff_lambda_k1, diff_lambda_q2,
           diff_lambda_k2, diff_subln_gain, w_out, ffn_norm_gain, w_group_router, b_group_router,
           w_expert_router, b_expert_router, w_gate, w_up, w_down, final_norm_gain):
    b, s, d = x.shape
    assert w_in.shape[0] == 1, "the combine stage fuses the final norm: single-layer stacks only"
    layer = 0
    x2d = x.reshape(b * s, d)
    x1, pos, rf, ys = _layer(
        x2d, b, s, layer, attn_norm_gain[layer], w_in[layer], sb_norm_gain[layer],
        diff_lambda_q1[layer], diff_lambda_k1[layer], diff_lambda_q2[layer],
        diff_lambda_k2[layer], diff_subln_gain[layer], w_out[layer], ffn_norm_gain[layer],
        w_group_router[layer], b_group_router[layer], w_expert_router[layer],
        b_expert_router[layer], w_gate[layer], w_up[layer], w_down[layer])
    out = _combine_call(pos, x1, rf, final_norm_gain.reshape(1, d), ys)
    return out.reshape(b, s, d)
```

```python
import functools
import math

import jax
import jax.numpy as jnp
import numpy as np
from jax import lax
from jax.experimental import pallas as pl
from jax.experimental.pallas import tpu as pltpu

F32 = jnp.float32
BF16 = jnp.bfloat16

HEAD_DIM = 128
N_SB_HEADS = 8
N_DIFF_HEADS = 4
SECTION = 1024
CHUNK = 64
ROPE_THETA = 10000.0
N_GROUPS = 4
EXPERTS_PER_GROUP = 8
N_EXPERTS = N_GROUPS * EXPERTS_PER_GROUP
NORM_EPS = 1e-6
NEG_INF = -1e30
LANES = 128
ROW_CHUNK = 32
UNDERFLOW_LOG2 = -200.0
DMA_UNROLL = 8
SCALE_LOG2E = math.log2(math.e) / math.sqrt(HEAD_DIM)

VMEM_LIMIT = 56 * 1024 * 1024


def _cparams(sem):
    return pltpu.CompilerParams(dimension_semantics=sem, vmem_limit_bytes=VMEM_LIMIT)


def _dot(a, b):
    return jnp.dot(a, b, preferred_element_type=F32)


def _dot_nt(a, b):
    return lax.dot_general(a, b, (((1,), (1,)), ((), ())), preferred_element_type=F32)


def _proj_kernel(x_ref, g_ref, w_ref, cos_ref, sin_ref, o_ref, h_ref, *, tn, rows):
    j = pl.program_id(1)
    tm = x_ref.shape[0]
    sec = j // (SECTION // tn)

    def normalize(rs):
        x = x_ref[rs, :]
        ms = jnp.mean(x * x, axis=-1, keepdims=True)
        h_ref[rs, :] = (x * lax.rsqrt(ms + NORM_EPS) * g_ref[...]).astype(BF16)

    def chunks(epilogue, first_visit=False):
        for r in range(0, tm, rows):
            rs = slice(r, r + rows)
            if first_visit:
                normalize(rs)
            epilogue(rs, _dot(h_ref[rs, :], w_ref[...].astype(BF16)))

    def rotary(scale):
        def epilogue(rs, acc):
            for c in range(tn // HEAD_DIM):
                cs = slice(c * HEAD_DIM, (c + 1) * HEAD_DIM)
                a = acc[:, cs]
                rot = a * cos_ref[rs, :] + pltpu.roll(a, HEAD_DIM // 2, 1) * sin_ref[rs, :]
                if scale != 1.0:
                    rot = rot * scale
                o_ref[rs, cs] = rot.astype(BF16)
        chunks(epilogue)

    def scaled(scale, first_visit=False):
        def epilogue(rs, acc):
            o_ref[rs, :] = (acc if scale == 1.0 else acc * scale).astype(BF16)
        chunks(epilogue, first_visit)

    @pl.when(j == 0)
    def _():
        scaled(-SCALE_LOG2E, first_visit=True)

    @pl.when((sec == 0) & (j > 0))
    def _():
        scaled(-SCALE_LOG2E)

    @pl.when((sec == 1) | (sec == 2) | (sec == 5))
    def _():
        scaled(1.0)

    @pl.when(sec == 3)
    def _():
        rotary(SCALE_LOG2E)

    @pl.when(sec == 4)
    def _():
        rotary(1.0)


def _proj_call(x2d, gain, w_in, cos, sin, seq, *, tm, tn):
    t, d = x2d.shape
    n = w_in.shape[1]
    assert t % tm == 0 and seq % tm == 0 and n % tn == 0 and SECTION % tn == 0
    rows = min(tm, 256)
    nseq = seq // tm
    return pl.pallas_call(
        functools.partial(_proj_kernel, tn=tn, rows=rows),
        grid=(t // tm, n // tn),
        in_specs=[
            pl.BlockSpec((tm, d), lambda i, j: (i, 0)),
            pl.BlockSpec((1, d), lambda i, j: (0, 0)),
            pl.BlockSpec((d, tn), lambda i, j: (0, j)),
            pl.BlockSpec((tm, HEAD_DIM), lambda i, j: (i % nseq, 0)),
            pl.BlockSpec((tm, HEAD_DIM), lambda i, j: (i % nseq, 0)),
        ],
        out_specs=pl.BlockSpec((tm, tn), lambda i, j: (i, j)),
        out_shape=jax.ShapeDtypeStruct((t, n), BF16),
        scratch_shapes=[pltpu.VMEM((tm, d), BF16)],
        compiler_params=_cparams(("arbitrary", "arbitrary")),
        name="proj",
    )(x2d, gain, w_in, cos, sin)


def _sb_kernel(q_ref, k_ref, v_ref, tri_ref, g_ref, o_ref, acc_ref, *, tb, nsub):
    qi = pl.program_id(2)
    tq = nsub * tb
    row = lax.broadcasted_iota(jnp.int32, (tb, tb), 0)
    col = lax.broadcasted_iota(jnp.int32, (tb, tb), 1)
    strict = col < row

    def scores(rows, kj):
        start = pl.multiple_of(kj * tb, tb)
        return _dot_nt(q_ref[0, rows, :], k_ref[0, pl.ds(start, tb), :])

    def keep_logs(n, mask):
        lks = []
        for r in range(0, n.shape[0], ROW_CHUNK):
            nc = n[r:r + ROW_CHUNK]
            lk = jnp.minimum(nc, 0.0) - jnp.log2(1.0 + jnp.exp2(-jnp.abs(nc)))
            if mask is not None:
                lk = jnp.where(mask[r:r + ROW_CHUNK], lk, 0.0)
            lks.append(lk.astype(BF16))
        return jnp.concatenate(lks, axis=0)

    def accumulate(rows, kj, n, lk, carry, mask):
        start = pl.multiple_of(kj * tb, tb)
        cum = _dot(lk, tri_ref[...])
        parts = []
        for r in range(0, n.shape[0], ROW_CHUNK):
            chunk = slice(r, r + ROW_CHUNK)
            a = jnp.exp2(cum[chunk] + carry[chunk] - n[chunk])
            if mask is not None:
                a = jnp.where(mask[chunk], a, 0.0)
            parts.append(a.astype(BF16))
        acc_ref[rows, :] += _dot(jnp.concatenate(parts, axis=0), v_ref[0, pl.ds(start, tb), :])
        return carry + cum[:, 0:1]

    def several(rows, kjs, carry, mask=None):
        staged = []
        for kj in kjs:
            n = scores(rows, kj)
            staged.append((kj, n, keep_logs(n, mask)))
        for kj, n, lk in staged:
            carry = accumulate(rows, kj, n, lk, carry, mask)
        return carry

    acc_ref[...] = jnp.zeros_like(acc_ref)
    base = qi * nsub
    blocks = [slice(r * tb, (r + 1) * tb) for r in range(nsub)]
    chains = []
    for r in range(nsub):
        chains.append((r, base + r, strict))
        chains.append((r, jnp.maximum(base + r - 1, 0),
                       jnp.broadcast_to(base > 0, (tb, tb)) if r == 0 else None))
    staged = []
    for r, kj, mask in chains:
        n = scores(blocks[r], kj)
        staged.append((n, keep_logs(n, mask)))
    carry = [jnp.zeros((tb, 1), F32)] * nsub
    for (r, kj, mask), (n, lk) in zip(chains, staged):
        carry[r] = accumulate(blocks[r], kj, n, lk, carry[r], mask)

    def walk_back(rows, first, carry):
        def still_live(kj, c):
            return (kj >= 0) & (jnp.max(c) > UNDERFLOW_LOG2)

        def one_block(state):
            kj, c, _ = state
            c = several(rows, [kj], c)
            return kj - 1, c, still_live(kj - 1, c)

        lax.while_loop(lambda state: state[2], one_block, (first, carry, still_live(first, carry)))

    for r in range(nsub):
        walk_back(blocks[r], base + r - 2, carry[r])

    o = acc_ref[...]
    ms = jnp.mean(o * o, axis=-1, keepdims=True)
    o_ref[0] = (o * lax.rsqrt(ms + NORM_EPS) * g_ref[...]).astype(BF16)


def _sb_call(proj3d, tri, gain, *, tb, nsub):
    b, s, _ = proj3d.shape
    tq = nsub * tb
    assert s % tq == 0 and tb % ROW_CHUNK == 0
    hq, hk, hv = 0, SECTION // HEAD_DIM, 2 * SECTION // HEAD_DIM
    return pl.pallas_call(
        functools.partial(_sb_kernel, tb=tb, nsub=nsub),
        grid=(b, N_SB_HEADS, s // tq),
        in_specs=[
            pl.BlockSpec((1, tq, HEAD_DIM), lambda bi, h, i: (bi, i, hq + h)),
            pl.BlockSpec((1, s, HEAD_DIM), lambda bi, h, i: (bi, 0, hk + h)),
            pl.BlockSpec((1, s, HEAD_DIM), lambda bi, h, i: (bi, 0, hv + h)),
            pl.BlockSpec((tb, tb), lambda bi, h, i: (0, 0)),
            pl.BlockSpec((1, HEAD_DIM), lambda bi, h, i: (0, 0)),
        ],
        out_specs=pl.BlockSpec((1, tq, HEAD_DIM), lambda bi, h, i: (bi, i, h)),
        out_shape=jax.ShapeDtypeStruct((b, s, N_SB_HEADS * HEAD_DIM), BF16),
        scratch_shapes=[pltpu.VMEM((tq, HEAD_DIM), F32)],
        compiler_params=_cparams(("arbitrary", "arbitrary", "arbitrary")),
        name="sb_attn",
    )(proj3d, proj3d, proj3d, tri, gain)


def _diff_kernel(q_ref, k_ref, v_ref, lq1_ref, lk1_ref, lq2_ref, lk2_ref, g_ref, o_ref,
                 acc1_ref, acc2_ref, *, tq, lam_init):
    qi = pl.program_id(2)
    d = HEAD_DIM
    q1 = q_ref[0, :, :d]
    q2 = q_ref[0, :, d:]
    row = lax.broadcasted_iota(jnp.int32, (tq, tq), 0)
    col = lax.broadcasted_iota(jnp.int32, (tq, tq), 1)
    visible = (col // CHUNK) <= (row // CHUNK)

    def probabilities(s, m, l):
        m_new = jnp.maximum(m, jnp.max(s, axis=-1, keepdims=True))
        alpha = jnp.exp2(m - m_new)
        p = jnp.exp2(s - m_new)
        return p.astype(BF16), alpha, m_new, alpha * l + jnp.sum(p, axis=-1, keepdims=True)

    def block(start, width, carry, masked):
        m1, l1, m2, l2 = carry
        k = k_ref[0, pl.ds(start, width), :]
        v = v_ref[0, pl.ds(start, width), :]
        s1 = _dot_nt(q1, k[:, :d])
        s2 = _dot_nt(q2, k[:, d:])
        if masked:
            s1 = jnp.where(visible, s1, NEG_INF)
            s2 = jnp.where(visible, s2, NEG_INF)
        p1, alpha1, m1, l1 = probabilities(s1, m1, l1)
        p2, alpha2, m2, l2 = probabilities(s2, m2, l2)
        pv = _dot(jnp.concatenate([p1, p2], axis=0), v)
        acc1_ref[...] = alpha1 * acc1_ref[...] + pv[:tq]
        acc2_ref[...] = alpha2 * acc2_ref[...] + pv[tq:]
        return m1, l1, m2, l2

    acc1_ref[...] = jnp.zeros_like(acc1_ref)
    acc2_ref[...] = jnp.zeros_like(acc2_ref)
    neg = jnp.full((tq, 1), NEG_INF, F32)
    zero = jnp.zeros((tq, 1), F32)
    carry = block(pl.multiple_of(qi * tq, tq), tq, (neg, zero, neg, zero), True)
    odd = qi % 2
    carry = lax.cond(odd == 1, lambda c: block(pl.multiple_of((qi - 1) * tq, tq), tq, c, False),
                     lambda c: c, carry)
    _, l1, _, l2 = lax.fori_loop(
        0, qi // 2, lambda i, c: block(pl.multiple_of(2 * i * tq, 2 * tq), 2 * tq, c, False), carry)

    lam = (jnp.exp(jnp.sum(lq1_ref[...] * lk1_ref[...], axis=-1, keepdims=True))
           - jnp.exp(jnp.sum(lq2_ref[...] * lk2_ref[...], axis=-1, keepdims=True)) + lam_init)
    o = acc1_ref[...] / l1 - lam * (acc2_ref[...] / l2)
    ms = jnp.mean(o * o, axis=-1, keepdims=True)
    o_ref[0] = (o * lax.rsqrt(ms + NORM_EPS) * g_ref[...] * (1.0 - lam_init)).astype(BF16)


def _diff_call(proj3d, lq1, lk1, lq2, lk2, gain, lam_init, *, tq):
    b, s, _ = proj3d.shape
    assert s % tq == 0 and tq % CHUNK == 0
    w = 2 * HEAD_DIM
    hq, hk, hv = 3 * SECTION // w, 4 * SECTION // w, 5 * SECTION // w
    vec = pl.BlockSpec((1, HEAD_DIM), lambda bi, h, i: (0, 0))
    return pl.pallas_call(
        functools.partial(_diff_kernel, tq=tq, lam_init=lam_init),
        grid=(b, N_DIFF_HEADS, s // tq),
        in_specs=[
            pl.BlockSpec((1, tq, w), lambda bi, h, i: (bi, i, hq + h)),
            pl.BlockSpec((1, s, w), lambda bi, h, i: (bi, 0, hk + h)),
            pl.BlockSpec((1, s, w), lambda bi, h, i: (bi, 0, hv + h)),
            vec, vec, vec, vec,
            pl.BlockSpec((1, w), lambda bi, h, i: (0, 0)),
        ],
        out_specs=pl.BlockSpec((1, tq, w), lambda bi, h, i: (bi, i, h)),
        out_shape=jax.ShapeDtypeStruct((b, s, N_DIFF_HEADS * w), BF16),
        scratch_shapes=[pltpu.VMEM((tq, w), F32), pltpu.VMEM((tq, w), F32)],
        compiler_params=_cparams(("arbitrary", "arbitrary", "arbitrary")),
        name="diff_attn",
    )(proj3d, proj3d, proj3d, lq1, lk1, lq2, lk2, gain)


def _outproj_kernel(x_ref, sb_ref, df_ref, wo_ref, g_ref, wr_ref, br_ref, tri_ref,
                    x1_ref, h2_ref, ri_ref, rf_ref, cnt_ref, wo_bf16_ref):
    i = pl.program_id(0)
    tm = x_ref.shape[0]
    half = sb_ref.shape[1]

    @pl.when(i == 0)
    def _():
        wo_bf16_ref[...] = wo_ref[...].astype(BF16)

    x1 = (x_ref[...] + _dot(sb_ref[...], wo_bf16_ref[:half, :])
          + _dot(df_ref[...], wo_bf16_ref[half:, :]))
    x1_ref[...] = x1
    ms = jnp.mean(x1 * x1, axis=-1, keepdims=True)
    h2 = x1 * lax.rsqrt(ms + NORM_EPS) * g_ref[...]
    h2_ref[...] = h2

    logits = _dot(h2.astype(BF16), wr_ref[...]) + br_ref[...]
    lane = lax.broadcasted_iota(jnp.int32, (tm, LANES), 1).astype(F32)
    ninf = -jnp.inf

    def first_argmax(vals):
        top = jnp.max(vals, axis=-1, keepdims=True)
        idx = jnp.min(jnp.where(vals == top, lane, float(LANES)), axis=-1, keepdims=True)
        return top, idx

    gl = jnp.where(lane < N_GROUPS, logits, ninf)
    gmax, gidx = first_argmax(gl)
    g_val = 1.0 / jnp.sum(jnp.exp(gl - gmax), axis=-1, keepdims=True)
    lo = N_GROUPS + EXPERTS_PER_GROUP * gidx
    el = jnp.where((lane >= lo) & (lane < lo + EXPERTS_PER_GROUP), logits, ninf)
    l1, i1 = first_argmax(el)
    l2, i2 = first_argmax(jnp.where(lane == i1, ninf, el))
    r = jnp.exp(l2 - l1)
    w1 = g_val / (1.0 + r)
    w2 = g_val * r / (1.0 + r)
    e1 = i1 - N_GROUPS
    e2 = i2 - N_GROUPS

    @pl.when(i == 0)
    def _():
        cnt_ref[...] = jnp.zeros_like(cnt_ref)

    onehot = jnp.where((lane == e1) | (lane == e2), 1.0, 0.0)
    before = _dot(tri_ref[...], onehot.astype(BF16)) + cnt_ref[0:1, :]
    rank1 = jnp.sum(jnp.where(lane == e1, before, 0.0), axis=-1, keepdims=True)
    rank2 = jnp.sum(jnp.where(lane == e2, before, 0.0), axis=-1, keepdims=True)
    cnt_ref[0:1, :] = cnt_ref[0:1, :] + jnp.sum(onehot, axis=0, keepdims=True)

    ri = jnp.where(lane == 0, e1, jnp.where(lane == 1, e2,
                   jnp.where(lane == 2, rank1, jnp.where(lane == 3, rank2, 0.0))))
    ri_ref[0] = jnp.transpose(ri)[:8].astype(jnp.int32)
    rf_ref[...] = jnp.where(lane == 0, w1, jnp.where(lane == 1, w2, 0.0))


def _outproj_call(x2d, sb_out, d_out, w_out, gain, w_router, b_router, tri, *, tm):
    t, d = x2d.shape
    half = sb_out.shape[1]
    assert t % tm == 0
    row = lambda i: (i, 0)
    fixed = lambda i: (0, 0)
    return pl.pallas_call(
        _outproj_kernel,
        grid=(t // tm,),
        in_specs=[
            pl.BlockSpec((tm, d), row),
            pl.BlockSpec((tm, half), row),
            pl.BlockSpec((tm, half), row),
            pl.BlockSpec((2 * half, d), fixed, pipeline_mode=pl.Buffered(1)),
            pl.BlockSpec((1, d), fixed),
            pl.BlockSpec((d, LANES), fixed),
            pl.BlockSpec((1, LANES), fixed),
            pl.BlockSpec((tm, tm), fixed),
        ],
        out_specs=[
            pl.BlockSpec((tm, d), row),
            pl.BlockSpec((tm, d), row),
            pl.BlockSpec((1, 8, tm), lambda i: (i, 0, 0)),
            pl.BlockSpec((tm, LANES), row),
            pl.BlockSpec((8, LANES), fixed),
        ],
        out_shape=[
            jax.ShapeDtypeStruct((t, d), F32),
            jax.ShapeDtypeStruct((t, d), F32),
            jax.ShapeDtypeStruct((t // tm, 8, tm), jnp.int32),
            jax.ShapeDtypeStruct((t, LANES), F32),
            jax.ShapeDtypeStruct((8, LANES), F32),
        ],
        scratch_shapes=[pltpu.VMEM((2 * half, d), BF16)],
        compiler_params=_cparams(("arbitrary",)),
        name="outproj_router",
    )(x2d, sb_out, d_out, w_out, gain, w_router, b_router, tri)


def _dispatch_kernel(pos_hbm, h_ref, xs_hbm, pos_smem, sem_idx, sem, *, td):
    i = pl.program_id(0)
    idx_copy = pltpu.make_async_copy(pos_hbm.at[i], pos_smem, sem_idx)
    idx_copy.start()
    idx_copy.wait()

    def row_copy(t, k):
        return pltpu.make_async_copy(h_ref.at[pl.ds(t, 1)],
                                     xs_hbm.at[pl.ds(pos_smem[0, k * td + t], 1)], sem)

    def issue(t, c):
        row_copy(t, 0).start()
        row_copy(t, 1).start()
        return c

    lax.fori_loop(0, td, issue, 0, unroll=DMA_UNROLL)

    def drain(t, c):
        row_copy(t, 0).wait()
        row_copy(t, 1).wait()
        return c

    lax.fori_loop(0, td, drain, 0, unroll=DMA_UNROLL)


def _dispatch_call(pos3, h2, n_rows):
    t, d = h2.shape
    td = pos3.shape[2] // 2
    assert pos3.shape[0] * td == t
    return pl.pallas_call(
        functools.partial(_dispatch_kernel, td=td),
        grid=(t // td,),
        in_specs=[pl.BlockSpec(memory_space=pl.ANY), pl.BlockSpec((td, d), lambda i: (i, 0))],
        out_specs=pl.BlockSpec(memory_space=pl.ANY),
        out_shape=jax.ShapeDtypeStruct((n_rows, d), h2.dtype),
        scratch_shapes=[pltpu.SMEM((1, 2 * td), jnp.int32), pltpu.SemaphoreType.DMA,
                        pltpu.SemaphoreType.DMA],
        compiler_params=_cparams(("arbitrary",)),
        name="dispatch",
    )(pos3, h2)


def _experts_kernel(wt_ref, we_ref, nw_ref, lo_ref, hi_ref, xs_ref, wg_ref, wu_ref, wd_ref, ys_ref):
    w = pl.program_id(0)
    tx = xs_ref.shape[0]
    tile = wt_ref[w]
    first = (w == 0) | (wt_ref[jnp.maximum(w - 1, 0)] != tile)

    @pl.when(w < nw_ref[0])
    def _():
        e = we_ref[w]
        rows = tile * tx + lax.broadcasted_iota(jnp.int32, (tx, 1), 0)
        member = (rows >= lo_ref[e]) & (rows < hi_ref[e])
        x = xs_ref[...].astype(BF16)
        gate = _dot(x, wg_ref[0].astype(BF16))
        up = _dot(x, wu_ref[0].astype(BF16))
        hid = gate * (1.0 / (1.0 + jnp.exp(-gate))) * up
        y = _dot(jnp.where(member, hid, 0.0).astype(BF16), wd_ref[0].astype(BF16))

        @pl.when(first)
        def _():
            ys_ref[...] = y

        @pl.when(jnp.logical_not(first))
        def _():
            ys_ref[...] += y


def _experts_call(work_tile, work_expert, n_work, seg_lo, seg_hi, xs, w_gate, w_up, w_down, *, tx):
    p, d = xs.shape
    de = w_gate.shape[2]
    assert p % tx == 0
    n_items = work_tile.shape[0]
    tile = lambda w, wt, we, nw, lo, hi: (wt[w], 0)
    expert = lambda w, wt, we, nw, lo, hi: (we[w], 0, 0)
    return pl.pallas_call(
        _experts_kernel,
        grid_spec=pltpu.PrefetchScalarGridSpec(
            num_scalar_prefetch=5,
            grid=(n_items,),
            in_specs=[
                pl.BlockSpec((tx, d), tile),
                pl.BlockSpec((1, d, de), expert),
                pl.BlockSpec((1, d, de), expert),
                pl.BlockSpec((1, de, d), expert),
            ],
            out_specs=pl.BlockSpec((tx, d), tile),
        ),
        out_shape=jax.ShapeDtypeStruct((p, d), F32),
        compiler_params=_cparams(("arbitrary",)),
        name="experts",
    )(work_tile, work_expert, n_work, seg_lo, seg_hi, xs, w_gate, w_up, w_down)


def _combine_kernel(pos_hbm, x1_ref, rf_ref, g_ref, ys_hbm, o_ref, pos_smem, y0_ref, y1_ref,
                    sem_idx, sem, *, tc):
    i = pl.program_id(0)
    idx_copy = pltpu.make_async_copy(pos_hbm.at[i], pos_smem, sem_idx)
    idx_copy.start()
    idx_copy.wait()

    def row_copy(t, k):
        dst = y0_ref if k == 0 else y1_ref
        return pltpu.make_async_copy(ys_hbm.at[pl.ds(pos_smem[0, k * tc + t], 1)],
                                     dst.at[pl.ds(t, 1)], sem)

    def issue(t, c):
        row_copy(t, 0).start()
        row_copy(t, 1).start()
        return c

    lax.fori_loop(0, tc, issue, 0, unroll=DMA_UNROLL)

    def drain(t, c):
        row_copy(t, 0).wait()
        row_copy(t, 1).wait()
        return c

    lax.fori_loop(0, tc, drain, 0, unroll=DMA_UNROLL)

    x = x1_ref[...] + rf_ref[:, 0:1] * y0_ref[...] + rf_ref[:, 1:2] * y1_ref[...]
    ms = jnp.mean(x * x, axis=-1, keepdims=True)
    o_ref[...] = x * lax.rsqrt(ms + NORM_EPS) * g_ref[...]


def _combine_call(pos3, x1, rf, gain, ys):
    t, d = x1.shape
    tc = pos3.shape[2] // 2
    assert pos3.shape[0] * tc == t
    row = lambda i: (i, 0)
    return pl.pallas_call(
        functools.partial(_combine_kernel, tc=tc),
        grid=(t // tc,),
        in_specs=[
            pl.BlockSpec(memory_space=pl.ANY),
            pl.BlockSpec((tc, d), row),
            pl.BlockSpec((tc, LANES), row),
            pl.BlockSpec((1, d), lambda i: (0, 0)),
            pl.BlockSpec(memory_space=pl.ANY),
        ],
        out_specs=pl.BlockSpec((tc, d), row),
        out_shape=jax.ShapeDtypeStruct((t, d), F32),
        scratch_shapes=[pltpu.SMEM((1, 2 * tc), jnp.int32), pltpu.VMEM((tc, d), F32),
                        pltpu.VMEM((tc, d), F32), pltpu.SemaphoreType.DMA,
                        pltpu.SemaphoreType.DMA],
        compiler_params=_cparams(("arbitrary",)),
        name="combine",
    )(pos3, x1, rf, gain, ys)


def _pick(n, pref):
    while n % pref:
        pref //= 2
    return pref


def _layer(x2d, b, s, layer, attn_norm_gain, w_in, sb_norm_gain, lq1, lk1, lq2, lk2, subln_gain,
           w_out, ffn_norm_gain, w_gr, b_gr, w_er, b_er, w_gate, w_up, w_down):
    t, d = x2d.shape
    tb = _pick(s, 256)
    tm_proj = _pick(s, 1024)
    tm_out = _pick(t, 256)
    tx = 256

    half = HEAD_DIM // 2
    inv_freq = 1.0 / (ROPE_THETA ** (np.arange(half, dtype=np.float64) / half))
    ang = np.arange(s, dtype=np.float64)[:, None] * inv_freq[None, :]
    cos = jnp.asarray(np.concatenate([np.cos(ang), np.cos(ang)], axis=-1), F32)
    sin = jnp.asarray(np.concatenate([-np.sin(ang), np.sin(ang)], axis=-1), F32)

    proj = _proj_call(x2d, attn_norm_gain.reshape(1, d), w_in, cos, sin, s,
                      tm=tm_proj, tn=512)
    proj3d = proj.reshape(b, s, proj.shape[1])

    r = jnp.arange(tb)
    tri_suffix = (r[:, None] >= r[None, :]).astype(BF16)
    sb_out = _sb_call(proj3d, tri_suffix, sb_norm_gain.reshape(1, HEAD_DIM), tb=tb,
                      nsub=_pick(s // tb, 8))

    lam_init = 0.8 - 0.6 * math.exp(-0.3 * layer)
    d_out = _diff_call(proj3d, lq1.reshape(1, -1), lk1.reshape(1, -1), lq2.reshape(1, -1),
                       lk2.reshape(1, -1), subln_gain.reshape(1, -1), lam_init, tq=2 * tb)

    w_router = jnp.zeros((d, LANES), F32)
    w_router = w_router.at[:, :N_GROUPS].set(w_gr).at[:, N_GROUPS:N_GROUPS + N_EXPERTS].set(w_er)
    b_router = jnp.zeros((1, LANES), F32)
    b_router = b_router.at[0, :N_GROUPS].set(b_gr).at[0, N_GROUPS:N_GROUPS + N_EXPERTS].set(b_er)
    rr = jnp.arange(tm_out)
    tri_before = (rr[None, :] < rr[:, None]).astype(BF16)
    x1, h2, ri, rf, cnt = _outproj_call(
        x2d, sb_out.reshape(t, -1), d_out.reshape(t, -1), w_out,
        ffn_norm_gain.reshape(1, d), w_router.astype(BF16), b_router, tri_before, tm=tm_out)

    counts = cnt[0, :N_EXPERTS].astype(jnp.int32)
    seg_hi = jnp.cumsum(counts)
    seg_lo = seg_hi - counts
    experts = jnp.arange(N_EXPERTS, dtype=jnp.int32)
    ri = ri.transpose(1, 0, 2).reshape(8, t)

    def sorted_position(slot):
        hit = ri[slot][None, :] == experts[:, None]
        return jnp.sum(jnp.where(hit, seg_lo[:, None], 0), axis=0) + ri[2 + slot]

    def tiled_positions(tile):
        return jnp.stack([sorted_position(0).reshape(-1, tile), sorted_position(1).reshape(-1, tile)],
                         axis=1).reshape(-1, 1, 2 * tile)

    first_tile = seg_lo // tx
    items = jnp.where(counts > 0, (seg_hi - 1) // tx - first_tile + 1, 0)
    item_hi = jnp.cumsum(items)
    n_work = item_hi[-1]
    w = jnp.minimum(jnp.arange(2 * t // tx + N_EXPERTS - 1, dtype=jnp.int32), n_work - 1)
    work_expert = jnp.sum((item_hi[None, :] <= w[:, None]).astype(jnp.int32), axis=1)
    work_tile = (first_tile[work_expert] + w - (item_hi - items)[work_expert]).astype(jnp.int32)

    xs = _dispatch_call(tiled_positions(_pick(t, 1024)), h2, 2 * t)
    ys = _experts_call(work_tile, work_expert, n_work.reshape(1).astype(jnp.int32), seg_lo, seg_hi,
                       xs, w_gate, w_up, w_down, tx=tx)
    return x1, tiled_positions(_pick(t, 512)), rf, ys


def kernel(x, attn_norm_gain, w_in, sb_norm_gain, diff_lambda_q1, diff_lambda_k1, diff_lambda_q2,
           diff_lambda_k2, diff_subln_gain, w_out, ffn_norm_gain, w_group_router, b_group_router,
           w_expert_router, b_expert_router, w_gate, w_up, w_down, final_norm_gain):
    b, s, d = x.shape
    assert w_in.shape[0] == 1, "the combine stage fuses the final norm: single-layer stacks only"
    layer = 0
    x2d = x.reshape(b * s, d)
    x1, pos, rf, ys = _layer(
        x2d, b, s, layer, attn_norm_gain[layer], w_in[layer], sb_norm_gain[layer],
        diff_lambda_q1[layer], diff_lambda_k1[layer], diff_lambda_q2[layer],
        diff_lambda_k2[layer], diff_subln_gain[layer], w_out[layer], ffn_norm_gain[layer],
        w_group_router[layer], b_group_router[layer], w_expert_router[layer],
        b_expert_router[layer], w_gate[layer], w_up[layer], w_down[layer])
    out = _combine_call(pos, x1, rf, final_norm_gain.reshape(1, d), ys)
    return out.reshape(b, s, d)
```

```python
import functools
import math

import jax
import jax.numpy as jnp
import numpy as np
from jax import lax
from jax.experimental import pallas as pl
from jax.experimental.pallas import tpu as pltpu

F32 = jnp.float32
BF16 = jnp.bfloat16

HEAD_DIM = 128
N_SB_HEADS = 8
N_DIFF_HEADS = 4
SECTION = 1024
CHUNK = 64
ROPE_THETA = 10000.0
N_GROUPS = 4
EXPERTS_PER_GROUP = 8
N_EXPERTS = N_GROUPS * EXPERTS_PER_GROUP
NORM_EPS = 1e-6
NEG_INF = -1e30
LANES = 128
ROW_CHUNK = 32
UNDERFLOW_LOG2 = -200.0
DMA_UNROLL = 8
SCALE_LOG2E = math.log2(math.e) / math.sqrt(HEAD_DIM)

VMEM_LIMIT = 56 * 1024 * 1024


def _cparams(sem):
    return pltpu.CompilerParams(dimension_semantics=sem, vmem_limit_bytes=VMEM_LIMIT)


def _dot(a, b):
    return jnp.dot(a, b, preferred_element_type=F32)


def _dot_nt(a, b):
    return lax.dot_general(a, b, (((1,), (1,)), ((), ())), preferred_element_type=F32)


def _proj_kernel(x_ref, g_ref, w_ref, cos_ref, sin_ref, o_ref, h_ref, *, tn, rows):
    j = pl.program_id(1)
    tm = x_ref.shape[0]
    sec = j // (SECTION // tn)

    def normalize(rs):
        x = x_ref[rs, :]
        ms = jnp.mean(x * x, axis=-1, keepdims=True)
        h_ref[rs, :] = (x * lax.rsqrt(ms + NORM_EPS) * g_ref[...]).astype(BF16)

    def chunks(epilogue, first_visit=False):
        for r in range(0, tm, rows):
            rs = slice(r, r + rows)
            if first_visit:
                normalize(rs)
            epilogue(rs, _dot(h_ref[rs, :], w_ref[...].astype(BF16)))

    def rotary(scale):
        def epilogue(rs, acc):
            for c in range(tn // HEAD_DIM):
                cs = slice(c * HEAD_DIM, (c + 1) * HEAD_DIM)
                a = acc[:, cs]
                rot = a * cos_ref[rs, :] + pltpu.roll(a, HEAD_DIM // 2, 1) * sin_ref[rs, :]
                if scale != 1.0:
                    rot = rot * scale
                o_ref[rs, cs] = rot.astype(BF16)
        chunks(epilogue)

    def scaled(scale, first_visit=False):
        def epilogue(rs, acc):
            o_ref[rs, :] = (acc if scale == 1.0 else acc * scale).astype(BF16)
        chunks(epilogue, first_visit)

    @pl.when(j == 0)
    def _():
        scaled(-SCALE_LOG2E, first_visit=True)

    @pl.when((sec == 0) & (j > 0))
    def _():
        scaled(-SCALE_LOG2E)

    @pl.when((sec == 1) | (sec == 2) | (sec == 5))
    def _():
        scaled(1.0)

    @pl.when(sec == 3)
    def _():
        rotary(SCALE_LOG2E)

    @pl.when(sec == 4)
    def _():
        rotary(1.0)


def _proj_call(x2d, gain, w_in, cos, sin, seq, *, tm, tn):
    t, d = x2d.shape
    n = w_in.shape[1]
    assert t % tm == 0 and seq % tm == 0 and n % tn == 0 and SECTION % tn == 0
    rows = min(tm, 256)
    nseq = seq // tm
    return pl.pallas_call(
        functools.partial(_proj_kernel, tn=tn, rows=rows),
        grid=(t // tm, n // tn),
        in_specs=[
            pl.BlockSpec((tm, d), lambda i, j: (i, 0)),
            pl.BlockSpec((1, d), lambda i, j: (0, 0)),
            pl.BlockSpec((d, tn), lambda i, j: (0, j)),
            pl.BlockSpec((tm, HEAD_DIM), lambda i, j: (i % nseq, 0)),
            pl.BlockSpec((tm, HEAD_DIM), lambda i, j: (i % nseq, 0)),
        ],
        out_specs=pl.BlockSpec((tm, tn), lambda i, j: (i, j)),
        out_shape=jax.ShapeDtypeStruct((t, n), BF16),
        scratch_shapes=[pltpu.VMEM((tm, d), BF16)],
        compiler_params=_cparams(("arbitrary", "arbitrary")),
        name="proj",
    )(x2d, gain, w_in, cos, sin)


def _sb_kernel(q_ref, k_ref, v_ref, tri_ref, g_ref, o_ref, acc_ref, *, tb, nsub):
    qi = pl.program_id(2)
    tq = nsub * tb
    row = lax.broadcasted_iota(jnp.int32, (tb, tb), 0)
    col = lax.broadcasted_iota(jnp.int32, (tb, tb), 1)
    strict = col < row

    def scores(rows, kj):
        start = pl.multiple_of(kj * tb, tb)
        return _dot_nt(q_ref[0, rows, :], k_ref[0, pl.ds(start, tb), :])

    def keep_logs(n, mask):
        lks = []
        for r in range(0, n.shape[0], ROW_CHUNK):
            nc = n[r:r + ROW_CHUNK]
            lk = jnp.minimum(nc, 0.0) - jnp.log2(1.0 + jnp.exp2(-jnp.abs(nc)))
            if mask is not None:
                lk = jnp.where(mask[r:r + ROW_CHUNK], lk, 0.0)
            lks.append(lk.astype(BF16))
        return jnp.concatenate(lks, axis=0)

    def accumulate(rows, kj, n, lk, carry, mask):
        start = pl.multiple_of(kj * tb, tb)
        cum = _dot(lk, tri_ref[...])
        parts = []
        for r in range(0, n.shape[0], ROW_CHUNK):
            chunk = slice(r, r + ROW_CHUNK)
            a = jnp.exp2(cum[chunk] + carry[chunk] - n[chunk])
            if mask is not None:
                a = jnp.where(mask[chunk], a, 0.0)
            parts.append(a.astype(BF16))
        acc_ref[rows, :] += _dot(jnp.concatenate(parts, axis=0), v_ref[0, pl.ds(start, tb), :])
        return carry + cum[:, 0:1]

    def several(rows, kjs, carry, mask=None):
        staged = []
        for kj in kjs:
            n = scores(rows, kj)
            staged.append((kj, n, keep_logs(n, mask)))
        for kj, n, lk in staged:
            carry = accumulate(rows, kj, n, lk, carry, mask)
        return carry

    acc_ref[...] = jnp.zeros_like(acc_ref)
    base = qi * nsub
    blocks = [slice(r * tb, (r + 1) * tb) for r in range(nsub)]
    chains = []
    for r in range(nsub):
        chains.append((r, base + r, strict))
        chains.append((r, jnp.maximum(base + r - 1, 0),
                       jnp.broadcast_to(base > 0, (tb, tb)) if r == 0 else None))
    staged = []
    for r, kj, mask in chains:
        n = scores(blocks[r], kj)
        staged.append((n, keep_logs(n, mask)))
    carry = [jnp.zeros((tb, 1), F32)] * nsub
    for (r, kj, mask), (n, lk) in zip(chains, staged):
        carry[r] = accumulate(blocks[r], kj, n, lk, carry[r], mask)

    def walk_back(rows, first, carry):
        def still_live(kj, c):
            return (kj >= 0) & (jnp.max(c) > UNDERFLOW_LOG2)

        def one_block(state):
            kj, c, _ = state
            c = several(rows, [kj], c)
            return kj - 1, c, still_live(kj - 1, c)

        lax.while_loop(lambda state: state[2], one_block, (first, carry, still_live(first, carry)))

    for r in range(nsub):
        walk_back(blocks[r], base + r - 2, carry[r])

    o = acc_ref[...]
    ms = jnp.mean(o * o, axis=-1, keepdims=True)
    o_ref[0] = (o * lax.rsqrt(ms + NORM_EPS) * g_ref[...]).astype(BF16)


def _sb_call(proj3d, tri, gain, *, tb, nsub):
    b, s, _ = proj3d.shape
    tq = nsub * tb
    assert s % tq == 0 and tb % ROW_CHUNK == 0
    hq, hk, hv = 0, SECTION // HEAD_DIM, 2 * SECTION // HEAD_DIM
    return pl.pallas_call(
        functools.partial(_sb_kernel, tb=tb, nsub=nsub),
        grid=(b, N_SB_HEADS, s // tq),
        in_specs=[
            pl.BlockSpec((1, tq, HEAD_DIM), lambda bi, h, i: (bi, i, hq + h)),
            pl.BlockSpec((1, s, HEAD_DIM), lambda bi, h, i: (bi, 0, hk + h)),
            pl.BlockSpec((1, s, HEAD_DIM), lambda bi, h, i: (bi, 0, hv + h)),
            pl.BlockSpec((tb, tb), lambda bi, h, i: (0, 0)),
            pl.BlockSpec((1, HEAD_DIM), lambda bi, h, i: (0, 0)),
        ],
        out_specs=pl.BlockSpec((1, tq, HEAD_DIM), lambda bi, h, i: (bi, i, h)),
        out_shape=jax.ShapeDtypeStruct((b, s, N_SB_HEADS * HEAD_DIM), BF16),
        scratch_shapes=[pltpu.VMEM((tq, HEAD_DIM), F32)],
        compiler_params=_cparams(("arbitrary", "arbitrary", "arbitrary")),
        name="sb_attn",
    )(proj3d, proj3d, proj3d, tri, gain)


def _diff_kernel(q_ref, k_ref, v_ref, lq1_ref, lk1_ref, lq2_ref, lk2_ref, g_ref, o_ref,
                 acc1_ref, acc2_ref, *, tq, lam_init):
    qi = pl.program_id(2)
    d = HEAD_DIM
    q1 = q_ref[0, :, :d]
    q2 = q_ref[0, :, d:]
    row = lax.broadcasted_iota(jnp.int32, (tq, tq), 0)
    col = lax.broadcasted_iota(jnp.int32, (tq, tq), 1)
    visible = (col // CHUNK) <= (row // CHUNK)

    def probabilities(s, m, l):
        m_new = jnp.maximum(m, jnp.max(s, axis=-1, keepdims=True))
        alpha = jnp.exp2(m - m_new)
        p = jnp.exp2(s - m_new)
        return p.astype(BF16), alpha, m_new, alpha * l + jnp.sum(p, axis=-1, keepdims=True)

    def block(start, width, carry, masked):
        m1, l1, m2, l2 = carry
        k = k_ref[0, pl.ds(start, width), :]
        v = v_ref[0, pl.ds(start, width), :]
        s1 = _dot_nt(q1, k[:, :d])
        s2 = _dot_nt(q2, k[:, d:])
        if masked:
            s1 = jnp.where(visible, s1, NEG_INF)
            s2 = jnp.where(visible, s2, NEG_INF)
        p1, alpha1, m1, l1 = probabilities(s1, m1, l1)
        p2, alpha2, m2, l2 = probabilities(s2, m2, l2)
        pv = _dot(jnp.concatenate([p1, p2], axis=0), v)
        acc1_ref[...] = alpha1 * acc1_ref[...] + pv[:tq]
        acc2_ref[...] = alpha2 * acc2_ref[...] + pv[tq:]
        return m1, l1, m2, l2

    acc1_ref[...] = jnp.zeros_like(acc1_ref)
    acc2_ref[...] = jnp.zeros_like(acc2_ref)
    neg = jnp.full((tq, 1), NEG_INF, F32)
    zero = jnp.zeros((tq, 1), F32)
    carry = block(pl.multiple_of(qi * tq, tq), tq, (neg, zero, neg, zero), True)
    odd = qi % 2
    carry = lax.cond(odd == 1, lambda c: block(pl.multiple_of((qi - 1) * tq, tq), tq, c, False),
                     lambda c: c, carry)
    _, l1, _, l2 = lax.fori_loop(
        0, qi // 2, lambda i, c: block(pl.multiple_of(2 * i * tq, 2 * tq), 2 * tq, c, False), carry)

    lam = (jnp.exp(jnp.sum(lq1_ref[...] * lk1_ref[...], axis=-1, keepdims=True))
           - jnp.exp(jnp.sum(lq2_ref[...] * lk2_ref[...], axis=-1, keepdims=True)) + lam_init)
    o = acc1_ref[...] / l1 - lam * (acc2_ref[...] / l2)
    ms = jnp.mean(o * o, axis=-1, keepdims=True)
    o_ref[0] = (o * lax.rsqrt(ms + NORM_EPS) * g_ref[...] * (1.0 - lam_init)).astype(BF16)


def _diff_call(proj3d, lq1, lk1, lq2, lk2, gain, lam_init, *, tq):
    b, s, _ = proj3d.shape
    assert s % tq == 0 and tq % CHUNK == 0
    w = 2 * HEAD_DIM
    hq, hk, hv = 3 * SECTION // w, 4 * SECTION // w, 5 * SECTION // w
    vec = pl.BlockSpec((1, HEAD_DIM), lambda bi, h, i: (0, 0))
    return pl.pallas_call(
        functools.partial(_diff_kernel, tq=tq, lam_init=lam_init),
        grid=(b, N_DIFF_HEADS, s // tq),
        in_specs=[
            pl.BlockSpec((1, tq, w), lambda bi, h, i: (bi, i, hq + h)),
            pl.BlockSpec((1, s, w), lambda bi, h, i: (bi, 0, hk + h)),
            pl.BlockSpec((1, s, w), lambda bi, h, i: (bi, 0, hv + h)),
            vec, vec, vec, vec,
            pl.BlockSpec((1, w), lambda bi, h, i: (0, 0)),
        ],
        out_specs=pl.BlockSpec((1, tq, w), lambda bi, h, i: (bi, i, h)),
        out_shape=jax.ShapeDtypeStruct((b, s, N_DIFF_HEADS * w), BF16),
        scratch_shapes=[pltpu.VMEM((tq, w), F32), pltpu.VMEM((tq, w), F32)],
        compiler_params=_cparams(("arbitrary", "arbitrary", "arbitrary")),
        name="diff_attn",
    )(proj3d, proj3d, proj3d, lq1, lk1, lq2, lk2, gain)


def _outproj_kernel(x_ref, sb_ref, df_ref, wo_ref, g_ref, wr_ref, br_ref, tri_ref,
                    x1_ref, h2_ref, ri_ref, rf_ref, cnt_ref, wo_bf16_ref):
    i = pl.program_id(0)
    tm = x_ref.shape[0]
    half = sb_ref.shape[1]

    @pl.when(i == 0)
    def _():
        wo_bf16_ref[...] = wo_ref[...].astype(BF16)

    x1 = (x_ref[...] + _dot(sb_ref[...], wo_bf16_ref[:half, :])
          + _dot(df_ref[...], wo_bf16_ref[half:, :]))
    x1_ref[...] = x1
    ms = jnp.mean(x1 * x1, axis=-1, keepdims=True)
    h2 = x1 * lax.rsqrt(ms + NORM_EPS) * g_ref[...]
    h2_ref[...] = h2

    logits = _dot(h2.astype(BF16), wr_ref[...]) + br_ref[...]
    lane = lax.broadcasted_iota(jnp.int32, (tm, LANES), 1).astype(F32)
    ninf = -jnp.inf

    def first_argmax(vals):
        top = jnp.max(vals, axis=-1, keepdims=True)
        idx = jnp.min(jnp.where(vals == top, lane, float(LANES)), axis=-1, keepdims=True)
        return top, idx

    gl = jnp.where(lane < N_GROUPS, logits, ninf)
    gmax, gidx = first_argmax(gl)
    g_val = 1.0 / jnp.sum(jnp.exp(gl - gmax), axis=-1, keepdims=True)
    lo = N_GROUPS + EXPERTS_PER_GROUP * gidx
    el = jnp.where((lane >= lo) & (lane < lo + EXPERTS_PER_GROUP), logits, ninf)
    l1, i1 = first_argmax(el)
    l2, i2 = first_argmax(jnp.where(lane == i1, ninf, el))
    r = jnp.exp(l2 - l1)
    w1 = g_val / (1.0 + r)
    w2 = g_val * r / (1.0 + r)
    e1 = i1 - N_GROUPS
    e2 = i2 - N_GROUPS

    @pl.when(i == 0)
    def _():
        cnt_ref[...] = jnp.zeros_like(cnt_ref)

    onehot = jnp.where((lane == e1) | (lane == e2), 1.0, 0.0)
    before = _dot(tri_ref[...], onehot.astype(BF16)) + cnt_ref[0:1, :]
    rank1 = jnp.sum(jnp.where(lane == e1, before, 0.0), axis=-1, keepdims=True)
    rank2 = jnp.sum(jnp.where(lane == e2, before, 0.0), axis=-1, keepdims=True)
    cnt_ref[0:1, :] = cnt_ref[0:1, :] + jnp.sum(onehot, axis=0, keepdims=True)

    ri = jnp.where(lane == 0, e1, jnp.where(lane == 1, e2,
                   jnp.where(lane == 2, rank1, jnp.where(lane == 3, rank2, 0.0))))
    ri_ref[0] = jnp.transpose(ri)[:8].astype(jnp.int32)
    rf_ref[...] = jnp.where(lane == 0, w1, jnp.where(lane == 1, w2, 0.0))


def _outproj_call(x2d, sb_out, d_out, w_out, gain, w_router, b_router, tri, *, tm):
    t, d = x2d.shape
    half = sb_out.shape[1]
    assert t % tm == 0
    row = lambda i: (i, 0)
    fixed = lambda i: (0, 0)
    return pl.pallas_call(
        _outproj_kernel,
        grid=(t // tm,),
        in_specs=[
            pl.BlockSpec((tm, d), row),
            pl.BlockSpec((tm, half), row),
            pl.BlockSpec((tm, half), row),
            pl.BlockSpec((2 * half, d), fixed, pipeline_mode=pl.Buffered(1)),
            pl.BlockSpec((1, d), fixed),
            pl.BlockSpec((d, LANES), fixed),
            pl.BlockSpec((1, LANES), fixed),
            pl.BlockSpec((tm, tm), fixed),
        ],
        out_specs=[
            pl.BlockSpec((tm, d), row),
            pl.BlockSpec((tm, d), row),
            pl.BlockSpec((1, 8, tm), lambda i: (i, 0, 0)),
            pl.BlockSpec((tm, LANES), row),
            pl.BlockSpec((8, LANES), fixed),
        ],
        out_shape=[
            jax.ShapeDtypeStruct((t, d), F32),
            jax.ShapeDtypeStruct((t, d), F32),
            jax.ShapeDtypeStruct((t // tm, 8, tm), jnp.int32),
            jax.ShapeDtypeStruct((t, LANES), F32),
            jax.ShapeDtypeStruct((8, LANES), F32),
        ],
        scratch_shapes=[pltpu.VMEM((2 * half, d), BF16)],
        compiler_params=_cparams(("arbitrary",)),
        name="outproj_router",
    )(x2d, sb_out, d_out, w_out, gain, w_router, b_router, tri)


def _dispatch_kernel(pos_hbm, h_ref, xs_hbm, pos_smem, sem_idx, sem, *, td):
    i = pl.program_id(0)
    idx_copy = pltpu.make_async_copy(pos_hbm.at[i], pos_smem, sem_idx)
    idx_copy.start()
    idx_copy.wait()

    def row_copy(t, k):
        return pltpu.make_async_copy(h_ref.at[pl.ds(t, 1)],
                                     xs_hbm.at[pl.ds(pos_smem[0, k * td + t], 1)], sem)

    def issue(t, c):
        row_copy(t, 0).start()
        row_copy(t, 1).start()
        return c

    lax.fori_loop(0, td, issue, 0, unroll=DMA_UNROLL)

    def drain(t, c):
        row_copy(t, 0).wait()
        row_copy(t, 1).wait()
        return c

    lax.fori_loop(0, td, drain, 0, unroll=DMA_UNROLL)


def _dispatch_call(pos3, h2, n_rows):
    t, d = h2.shape
    td = pos3.shape[2] // 2
    assert pos3.shape[0] * td == t
    return pl.pallas_call(
        functools.partial(_dispatch_kernel, td=td),
        grid=(t // td,),
        in_specs=[pl.BlockSpec(memory_space=pl.ANY), pl.BlockSpec((td, d), lambda i: (i, 0))],
        out_specs=pl.BlockSpec(memory_space=pl.ANY),
        out_shape=jax.ShapeDtypeStruct((n_rows, d), h2.dtype),
        scratch_shapes=[pltpu.SMEM((1, 2 * td), jnp.int32), pltpu.SemaphoreType.DMA,
                        pltpu.SemaphoreType.DMA],
        compiler_params=_cparams(("arbitrary",)),
        name="dispatch",
    )(pos3, h2)


def _experts_kernel(wt_ref, we_ref, nw_ref, lo_ref, hi_ref, slot_ref, nxt_ref, xs_ref, wg_hbm, wu_hbm,
                    wd_hbm, ys_ref, wg_buf, wu_buf, wd_buf, sem):
    w = pl.program_id(0)
    tx = xs_ref.shape[0]
    tile = wt_ref[w]
    e = we_ref[w]
    slot = slot_ref[w]
    before = jnp.maximum(w - 1, 0)
    first = (w == 0) | (wt_ref[before] != tile)
    first_of_expert = (w == 0) | (we_ref[before] != e)

    def weight_copies(expert, s):
        return [pltpu.make_async_copy(hbm.at[expert], buf.at[s], sem.at[s, i])
                for i, (hbm, buf) in enumerate(((wg_hbm, wg_buf), (wu_hbm, wu_buf), (wd_hbm, wd_buf)))]

    @pl.when(w == 0)
    def _():
        for copy in weight_copies(e, slot):
            copy.start()

    @pl.when(first_of_expert & (w < nw_ref[0]))
    def _():
        for copy in weight_copies(e, slot):
            copy.wait()

        @pl.when(nxt_ref[w] >= 0)
        def _():
            for copy in weight_copies(nxt_ref[w], 1 - slot):
                copy.start()

    @pl.when(w < nw_ref[0])
    def _():
        rows = tile * tx + lax.broadcasted_iota(jnp.int32, (tx, 1), 0)
        member = (rows >= lo_ref[e]) & (rows < hi_ref[e])
        x = xs_ref[...].astype(BF16)
        gate = _dot(x, wg_buf[slot].astype(BF16))
        up = _dot(x, wu_buf[slot].astype(BF16))
        hid = gate * (1.0 / (1.0 + jnp.exp(-gate))) * up
        y = _dot(jnp.where(member, hid, 0.0).astype(BF16), wd_buf[slot].astype(BF16))

        @pl.when(first)
        def _():
            ys_ref[...] = y

        @pl.when(jnp.logical_not(first))
        def _():
            ys_ref[...] += y


def _experts_call(work_tile, work_expert, n_work, seg_lo, seg_hi, work_slot, work_next, xs, w_gate,
                  w_up, w_down, *, tx):
    p, d = xs.shape
    de = w_gate.shape[2]
    assert p % tx == 0
    n_items = work_tile.shape[0]
    tile = lambda w, wt, *_: (wt[w], 0)
    any_space = pl.BlockSpec(memory_space=pl.ANY)
    return pl.pallas_call(
        _experts_kernel,
        grid_spec=pltpu.PrefetchScalarGridSpec(
            num_scalar_prefetch=7,
            grid=(n_items,),
            in_specs=[pl.BlockSpec((tx, d), tile), any_space, any_space, any_space],
            out_specs=pl.BlockSpec((tx, d), tile),
            scratch_shapes=[pltpu.VMEM((2, d, de), F32), pltpu.VMEM((2, d, de), F32),
                            pltpu.VMEM((2, de, d), F32), pltpu.SemaphoreType.DMA((2, 3))],
        ),
        out_shape=jax.ShapeDtypeStruct((p, d), F32),
        compiler_params=_cparams(("arbitrary",)),
        name="experts",
    )(work_tile, work_expert, n_work, seg_lo, seg_hi, work_slot, work_next, xs, w_gate, w_up, w_down)


def _combine_kernel(pos_hbm, x1_ref, rf_ref, g_ref, ys_hbm, o_ref, pos_smem, y0_ref, y1_ref,
                    sem_idx, sem, *, tc):
    i = pl.program_id(0)
    idx_copy = pltpu.make_async_copy(pos_hbm.at[i], pos_smem, sem_idx)
    idx_copy.start()
    idx_copy.wait()

    def row_copy(t, k):
        dst = y0_ref if k == 0 else y1_ref
        return pltpu.make_async_copy(ys_hbm.at[pl.ds(pos_smem[0, k * tc + t], 1)],
                                     dst.at[pl.ds(t, 1)], sem)

    def issue(t, c):
        row_copy(t, 0).start()
        row_copy(t, 1).start()
        return c

    lax.fori_loop(0, tc, issue, 0, unroll=DMA_UNROLL)

    def drain(t, c):
        row_copy(t, 0).wait()
        row_copy(t, 1).wait()
        return c

    lax.fori_loop(0, tc, drain, 0, unroll=DMA_UNROLL)

    x = x1_ref[...] + rf_ref[:, 0:1] * y0_ref[...] + rf_ref[:, 1:2] * y1_ref[...]
    ms = jnp.mean(x * x, axis=-1, keepdims=True)
    o_ref[...] = x * lax.rsqrt(ms + NORM_EPS) * g_ref[...]


def _combine_call(pos3, x1, rf, gain, ys):
    t, d = x1.shape
    tc = pos3.shape[2] // 2
    assert pos3.shape[0] * tc == t
    row = lambda i: (i, 0)
    return pl.pallas_call(
        functools.partial(_combine_kernel, tc=tc),
        grid=(t // tc,),
        in_specs=[
            pl.BlockSpec(memory_space=pl.ANY),
            pl.BlockSpec((tc, d), row),
            pl.BlockSpec((tc, LANES), row),
            pl.BlockSpec((1, d), lambda i: (0, 0)),
            pl.BlockSpec(memory_space=pl.ANY),
        ],
        out_specs=pl.BlockSpec((tc, d), row),
        out_shape=jax.ShapeDtypeStruct((t, d), F32),
        scratch_shapes=[pltpu.SMEM((1, 2 * tc), jnp.int32), pltpu.VMEM((tc, d), F32),
                        pltpu.VMEM((tc, d), F32), pltpu.SemaphoreType.DMA,
                        pltpu.SemaphoreType.DMA],
        compiler_params=_cparams(("arbitrary",)),
        name="combine",
    )(pos3, x1, rf, gain, ys)


def _pick(n, pref):
    while n % pref:
        pref //= 2
    return pref


def _layer(x2d, b, s, layer, attn_norm_gain, w_in, sb_norm_gain, lq1, lk1, lq2, lk2, subln_gain,
           w_out, ffn_norm_gain, w_gr, b_gr, w_er, b_er, w_gate, w_up, w_down):
    t, d = x2d.shape
    tb = _pick(s, 256)
    tm_proj = _pick(s, 1024)
    tm_out = _pick(t, 256)
    tx = 256

    half = HEAD_DIM // 2
    inv_freq = 1.0 / (ROPE_THETA ** (np.arange(half, dtype=np.float64) / half))
    ang = np.arange(s, dtype=np.float64)[:, None] * inv_freq[None, :]
    cos = jnp.asarray(np.concatenate([np.cos(ang), np.cos(ang)], axis=-1), F32)
    sin = jnp.asarray(np.concatenate([-np.sin(ang), np.sin(ang)], axis=-1), F32)

    proj = _proj_call(x2d, attn_norm_gain.reshape(1, d), w_in, cos, sin, s,
                      tm=tm_proj, tn=512)
    proj3d = proj.reshape(b, s, proj.shape[1])

    r = jnp.arange(tb)
    tri_suffix = (r[:, None] >= r[None, :]).astype(BF16)
    sb_out = _sb_call(proj3d, tri_suffix, sb_norm_gain.reshape(1, HEAD_DIM), tb=tb,
                      nsub=_pick(s // tb, 8))

    lam_init = 0.8 - 0.6 * math.exp(-0.3 * layer)
    d_out = _diff_call(proj3d, lq1.reshape(1, -1), lk1.reshape(1, -1), lq2.reshape(1, -1),
                       lk2.reshape(1, -1), subln_gain.reshape(1, -1), lam_init, tq=2 * tb)

    w_router = jnp.zeros((d, LANES), F32)
    w_router = w_router.at[:, :N_GROUPS].set(w_gr).at[:, N_GROUPS:N_GROUPS + N_EXPERTS].set(w_er)
    b_router = jnp.zeros((1, LANES), F32)
    b_router = b_router.at[0, :N_GROUPS].set(b_gr).at[0, N_GROUPS:N_GROUPS + N_EXPERTS].set(b_er)
    rr = jnp.arange(tm_out)
    tri_before = (rr[None, :] < rr[:, None]).astype(BF16)
    x1, h2, ri, rf, cnt = _outproj_call(
        x2d, sb_out.reshape(t, -1), d_out.reshape(t, -1), w_out,
        ffn_norm_gain.reshape(1, d), w_router.astype(BF16), b_router, tri_before, tm=tm_out)

    counts = cnt[0, :N_EXPERTS].astype(jnp.int32)
    seg_hi = jnp.cumsum(counts)
    seg_lo = seg_hi - counts
    experts = jnp.arange(N_EXPERTS, dtype=jnp.int32)
    ri = ri.transpose(1, 0, 2).reshape(8, t)

    def sorted_position(slot):
        hit = ri[slot][None, :] == experts[:, None]
        return jnp.sum(jnp.where(hit, seg_lo[:, None], 0), axis=0) + ri[2 + slot]

    def tiled_positions(tile):
        return jnp.stack([sorted_position(0).reshape(-1, tile), sorted_position(1).reshape(-1, tile)],
                         axis=1).reshape(-1, 1, 2 * tile)

    first_tile = seg_lo // tx
    items = jnp.where(counts > 0, (seg_hi - 1) // tx - first_tile + 1, 0)
    item_hi = jnp.cumsum(items)
    n_work = item_hi[-1]
    w = jnp.minimum(jnp.arange(2 * t // tx + N_EXPERTS - 1, dtype=jnp.int32), n_work - 1)
    work_expert = jnp.sum((item_hi[None, :] <= w[:, None]).astype(jnp.int32), axis=1)
    work_tile = (first_tile[work_expert] + w - (item_hi - items)[work_expert]).astype(jnp.int32)

    xs = _dispatch_call(tiled_positions(_pick(t, 1024)), h2, 2 * t)
    used = counts > 0
    work_slot = ((jnp.cumsum(used.astype(jnp.int32)) - 1) % 2)[work_expert].astype(jnp.int32)
    later_used = (experts[None, :] > experts[:, None]) & used[None, :]
    next_used = jnp.min(jnp.where(later_used, experts[None, :], N_EXPERTS), axis=1)
    next_used = jnp.where(next_used < N_EXPERTS, next_used, -1)
    run_start = jnp.concatenate([jnp.ones((1,), bool), work_expert[1:] != work_expert[:-1]])
    work_next = jnp.where(run_start, next_used[work_expert], -1).astype(jnp.int32)
    ys = _experts_call(work_tile, work_expert, n_work.reshape(1).astype(jnp.int32), seg_lo, seg_hi,
                       work_slot, work_next, xs, w_gate, w_up, w_down, tx=tx)
    return x1, tiled_positions(_pick(t, 512)), rf, ys


def kernel(x, attn_norm_gain, w_in, sb_norm_gain, diff_lambda_q1, diff_lambda_k1, diff_lambda_q2,
           diff_lambda_k2, diff_subln_gain, w_out, ffn_norm_gain, w_group_router, b_group_router,
           w_expert_router, b_expert_router, w_gate, w_up, w_down, final_norm_gain):
    b, s, d = x.shape
    assert w_in.shape[0] == 1, "the combine stage fuses the final norm: single-layer stacks only"
    layer = 0
    x2d = x.reshape(b * s, d)
    x1, pos, rf, ys = _layer(
        x2d, b, s, layer, attn_norm_gain[layer], w_in[layer], sb_norm_gain[layer],
        diff_lambda_q1[layer], diff_lambda_k1[layer], diff_lambda_q2[layer],
        diff_lambda_k2[layer], diff_subln_gain[layer], w_out[layer], ffn_norm_gain[layer],
        w_group_router[layer], b_group_router[layer], w_expert_router[layer],
        b_expert_router[layer], w_gate[layer], w_up[layer], w_down[layer])
    out = _combine_call(pos, x1, rf, final_norm_gain.reshape(1, d), ys)
    return out.reshape(b, s, d)
```

```python
import functools
import math

import jax
import jax.numpy as jnp
import numpy as np
from jax import lax
from jax.experimental import pallas as pl
from jax.experimental.pallas import tpu as pltpu

F32 = jnp.float32
BF16 = jnp.bfloat16

HEAD_DIM = 128
N_SB_HEADS = 8
N_DIFF_HEADS = 4
SECTION = 1024
CHUNK = 64
ROPE_THETA = 10000.0
N_GROUPS = 4
EXPERTS_PER_GROUP = 8
N_EXPERTS = N_GROUPS * EXPERTS_PER_GROUP
NORM_EPS = 1e-6
NEG_INF = -1e30
LANES = 128
UNDERFLOW_LOG2 = -200.0
DMA_UNROLL = 8
SCALE_LOG2E = math.log2(math.e) / math.sqrt(HEAD_DIM)

VMEM_LIMIT = 56 * 1024 * 1024


def _cparams(sem):
    return pltpu.CompilerParams(dimension_semantics=sem, vmem_limit_bytes=VMEM_LIMIT)


def _dot(a, b):
    return jnp.dot(a, b, preferred_element_type=F32)


def _dot_nt(a, b):
    return lax.dot_general(a, b, (((1,), (1,)), ((), ())), preferred_element_type=F32)


def _proj_kernel(x_ref, g_ref, w_ref, cos_ref, sin_ref, o_ref, h_ref, *, tn, rows):
    j = pl.program_id(1)
    tm = x_ref.shape[0]
    sec = j // (SECTION // tn)

    def normalize(rs):
        x = x_ref[rs, :]
        ms = jnp.mean(x * x, axis=-1, keepdims=True)
        h_ref[rs, :] = (x * lax.rsqrt(ms + NORM_EPS) * g_ref[...]).astype(BF16)

    def chunks(epilogue, first_visit=False):
        for r in range(0, tm, rows):
            rs = slice(r, r + rows)
            if first_visit:
                normalize(rs)
            epilogue(rs, _dot(h_ref[rs, :], w_ref[...].astype(BF16)))

    def rotary(scale):
        def epilogue(rs, acc):
            for c in range(tn // HEAD_DIM):
                cs = slice(c * HEAD_DIM, (c + 1) * HEAD_DIM)
                a = acc[:, cs]
                rot = a * cos_ref[rs, :] + pltpu.roll(a, HEAD_DIM // 2, 1) * sin_ref[rs, :]
                if scale != 1.0:
                    rot = rot * scale
                o_ref[rs, cs] = rot.astype(BF16)
        chunks(epilogue)

    def scaled(scale, first_visit=False):
        def epilogue(rs, acc):
            o_ref[rs, :] = (acc if scale == 1.0 else acc * scale).astype(BF16)
        chunks(epilogue, first_visit)

    @pl.when(j == 0)
    def _():
        scaled(-SCALE_LOG2E, first_visit=True)

    @pl.when((sec == 0) & (j > 0))
    def _():
        scaled(-SCALE_LOG2E)

    @pl.when((sec == 1) | (sec == 2) | (sec == 5))
    def _():
        scaled(1.0)

    @pl.when(sec == 3)
    def _():
        rotary(SCALE_LOG2E)

    @pl.when(sec == 4)
    def _():
        rotary(1.0)


def _proj_call(x2d, gain, w_in, cos, sin, seq, *, tm, tn):
    t, d = x2d.shape
    n = w_in.shape[1]
    assert t % tm == 0 and seq % tm == 0 and n % tn == 0 and SECTION % tn == 0
    rows = min(tm, 256)
    nseq = seq // tm
    return pl.pallas_call(
        functools.partial(_proj_kernel, tn=tn, rows=rows),
        grid=(t // tm, n // tn),
        in_specs=[
            pl.BlockSpec((tm, d), lambda i, j: (i, 0)),
            pl.BlockSpec((1, d), lambda i, j: (0, 0)),
            pl.BlockSpec((d, tn), lambda i, j: (0, j)),
            pl.BlockSpec((tm, HEAD_DIM), lambda i, j: (i % nseq, 0)),
            pl.BlockSpec((tm, HEAD_DIM), lambda i, j: (i % nseq, 0)),
        ],
        out_specs=pl.BlockSpec((tm, tn), lambda i, j: (i, j)),
        out_shape=jax.ShapeDtypeStruct((t, n), BF16),
        scratch_shapes=[pltpu.VMEM((tm, d), BF16)],
        compiler_params=_cparams(("arbitrary", "arbitrary")),
        name="proj",
    )(x2d, gain, w_in, cos, sin)


def _sb_kernel(q_ref, k_ref, v_ref, tri_ref, g_ref, o_ref, acc_ref, *, tb, nsub):
    qi = pl.program_id(2)
    tq = nsub * tb
    row = lax.broadcasted_iota(jnp.int32, (tb, tb), 0)
    col = lax.broadcasted_iota(jnp.int32, (tb, tb), 1)
    strict = col < row

    def scores(rows, kj):
        start = pl.multiple_of(kj * tb, tb)
        return _dot_nt(q_ref[0, rows, :], k_ref[0, pl.ds(start, tb), :])

    def keep_logs(n, mask):
        lk = jnp.minimum(n, 0.0) - jnp.log2(1.0 + jnp.exp2(-jnp.abs(n)))
        if mask is not None:
            lk = jnp.where(mask, lk, 0.0)
        return lk.astype(BF16)

    def accumulate(rows, kj, n, lk, carry, mask):
        start = pl.multiple_of(kj * tb, tb)
        cum = _dot(lk, tri_ref[...])
        a = jnp.exp2(cum + carry - n)
        if mask is not None:
            a = jnp.where(mask, a, 0.0)
        acc_ref[rows, :] += _dot(a.astype(BF16), v_ref[0, pl.ds(start, tb), :])
        return carry + cum[:, 0:1]

    def several(rows, kjs, carry, mask=None):
        staged = []
        for kj in kjs:
            n = scores(rows, kj)
            staged.append((kj, n, keep_logs(n, mask)))
        for kj, n, lk in staged:
            carry = accumulate(rows, kj, n, lk, carry, mask)
        return carry

    acc_ref[...] = jnp.zeros_like(acc_ref)
    base = qi * nsub
    blocks = [slice(r * tb, (r + 1) * tb) for r in range(nsub)]
    chains = []
    for r in range(nsub):
        chains.append((r, base + r, strict))
        chains.append((r, jnp.maximum(base + r - 1, 0),
                       jnp.broadcast_to(base > 0, (tb, tb)) if r == 0 else None))
    staged = []
    for r, kj, mask in chains:
        n = scores(blocks[r], kj)
        staged.append((n, keep_logs(n, mask)))
    carry = [jnp.zeros((tb, 1), F32)] * nsub
    for (r, kj, mask), (n, lk) in zip(chains, staged):
        carry[r] = accumulate(blocks[r], kj, n, lk, carry[r], mask)

    def walk_back(rows, first, carry):
        def still_live(kj, c):
            return (kj >= 0) & (jnp.max(c) > UNDERFLOW_LOG2)

        def one_block(state):
            kj, c, _ = state
            c = several(rows, [kj], c)
            return kj - 1, c, still_live(kj - 1, c)

        lax.while_loop(lambda state: state[2], one_block, (first, carry, still_live(first, carry)))

    for r in range(nsub):
        walk_back(blocks[r], base + r - 2, carry[r])

    o = acc_ref[...]
    ms = jnp.mean(o * o, axis=-1, keepdims=True)
    o_ref[0] = (o * lax.rsqrt(ms + NORM_EPS) * g_ref[...]).astype(BF16)


def _sb_call(proj3d, tri, gain, *, tb, nsub):
    b, s, _ = proj3d.shape
    tq = nsub * tb
    assert s % tq == 0
    hq, hk, hv = 0, SECTION // HEAD_DIM, 2 * SECTION // HEAD_DIM
    return pl.pallas_call(
        functools.partial(_sb_kernel, tb=tb, nsub=nsub),
        grid=(b, N_SB_HEADS, s // tq),
        in_specs=[
            pl.BlockSpec((1, tq, HEAD_DIM), lambda bi, h, i: (bi, i, hq + h)),
            pl.BlockSpec((1, s, HEAD_DIM), lambda bi, h, i: (bi, 0, hk + h)),
            pl.BlockSpec((1, s, HEAD_DIM), lambda bi, h, i: (bi, 0, hv + h)),
            pl.BlockSpec((tb, tb), lambda bi, h, i: (0, 0)),
            pl.BlockSpec((1, HEAD_DIM), lambda bi, h, i: (0, 0)),
        ],
        out_specs=pl.BlockSpec((1, tq, HEAD_DIM), lambda bi, h, i: (bi, i, h)),
        out_shape=jax.ShapeDtypeStruct((b, s, N_SB_HEADS * HEAD_DIM), BF16),
        scratch_shapes=[pltpu.VMEM((tq, HEAD_DIM), F32)],
        compiler_params=_cparams(("arbitrary", "arbitrary", "arbitrary")),
        name="sb_attn",
    )(proj3d, proj3d, proj3d, tri, gain)


def _diff_kernel(q_ref, k_ref, v_ref, lq1_ref, lk1_ref, lq2_ref, lk2_ref, g_ref, o_ref,
                 acc1_ref, acc2_ref, *, tq, lam_init):
    qi = pl.program_id(2)
    d = HEAD_DIM
    q1 = q_ref[0, :, :d]
    q2 = q_ref[0, :, d:]
    row = lax.broadcasted_iota(jnp.int32, (tq, tq), 0)
    col = lax.broadcasted_iota(jnp.int32, (tq, tq), 1)
    visible = (col // CHUNK) <= (row // CHUNK)

    def probabilities(s, m, l):
        m_new = jnp.maximum(m, jnp.max(s, axis=-1, keepdims=True))
        alpha = jnp.exp2(m - m_new)
        p = jnp.exp2(s - m_new)
        return p.astype(BF16), alpha, m_new, alpha * l + jnp.sum(p, axis=-1, keepdims=True)

    def block(start, width, carry, masked):
        m1, l1, m2, l2 = carry
        k = k_ref[0, pl.ds(start, width), :]
        v = v_ref[0, pl.ds(start, width), :]
        s1 = _dot_nt(q1, k[:, :d])
        s2 = _dot_nt(q2, k[:, d:])
        if masked:
            s1 = jnp.where(visible, s1, NEG_INF)
            s2 = jnp.where(visible, s2, NEG_INF)
        p1, alpha1, m1, l1 = probabilities(s1, m1, l1)
        p2, alpha2, m2, l2 = probabilities(s2, m2, l2)
        pv = _dot(jnp.concatenate([p1, p2], axis=0), v)
        acc1_ref[...] = alpha1 * acc1_ref[...] + pv[:tq]
        acc2_ref[...] = alpha2 * acc2_ref[...] + pv[tq:]
        return m1, l1, m2, l2

    acc1_ref[...] = jnp.zeros_like(acc1_ref)
    acc2_ref[...] = jnp.zeros_like(acc2_ref)
    neg = jnp.full((tq, 1), NEG_INF, F32)
    zero = jnp.zeros((tq, 1), F32)
    carry = block(pl.multiple_of(qi * tq, tq), tq, (neg, zero, neg, zero), True)
    odd = qi % 2
    carry = lax.cond(odd == 1, lambda c: block(pl.multiple_of((qi - 1) * tq, tq), tq, c, False),
                     lambda c: c, carry)
    _, l1, _, l2 = lax.fori_loop(
        0, qi // 2, lambda i, c: block(pl.multiple_of(2 * i * tq, 2 * tq), 2 * tq, c, False), carry)

    lam = (jnp.exp(jnp.sum(lq1_ref[...] * lk1_ref[...], axis=-1, keepdims=True))
           - jnp.exp(jnp.sum(lq2_ref[...] * lk2_ref[...], axis=-1, keepdims=True)) + lam_init)
    o = acc1_ref[...] / l1 - lam * (acc2_ref[...] / l2)
    ms = jnp.mean(o * o, axis=-1, keepdims=True)
    o_ref[0] = (o * lax.rsqrt(ms + NORM_EPS) * g_ref[...] * (1.0 - lam_init)).astype(BF16)


def _diff_call(proj3d, lq1, lk1, lq2, lk2, gain, lam_init, *, tq):
    b, s, _ = proj3d.shape
    assert s % tq == 0 and tq % CHUNK == 0
    w = 2 * HEAD_DIM
    hq, hk, hv = 3 * SECTION // w, 4 * SECTION // w, 5 * SECTION // w
    vec = pl.BlockSpec((1, HEAD_DIM), lambda bi, h, i: (0, 0))
    return pl.pallas_call(
        functools.partial(_diff_kernel, tq=tq, lam_init=lam_init),
        grid=(b, N_DIFF_HEADS, s // tq),
        in_specs=[
            pl.BlockSpec((1, tq, w), lambda bi, h, i: (bi, i, hq + h)),
            pl.BlockSpec((1, s, w), lambda bi, h, i: (bi, 0, hk + h)),
            pl.BlockSpec((1, s, w), lambda bi, h, i: (bi, 0, hv + h)),
            vec, vec, vec, vec,
            pl.BlockSpec((1, w), lambda bi, h, i: (0, 0)),
        ],
        out_specs=pl.BlockSpec((1, tq, w), lambda bi, h, i: (bi, i, h)),
        out_shape=jax.ShapeDtypeStruct((b, s, N_DIFF_HEADS * w), BF16),
        scratch_shapes=[pltpu.VMEM((tq, w), F32), pltpu.VMEM((tq, w), F32)],
        compiler_params=_cparams(("arbitrary", "arbitrary", "arbitrary")),
        name="diff_attn",
    )(proj3d, proj3d, proj3d, lq1, lk1, lq2, lk2, gain)


def _outproj_kernel(x_ref, sb_ref, df_ref, wo_ref, g_ref, wr_ref, br_ref, tri_ref,
                    x1_ref, h2_ref, ri_ref, rf_ref, cnt_ref, wo_bf16_ref):
    i = pl.program_id(0)
    tm = x_ref.shape[0]
    half = sb_ref.shape[1]

    @pl.when(i == 0)
    def _():
        wo_bf16_ref[...] = wo_ref[...].astype(BF16)

    x1 = (x_ref[...] + _dot(sb_ref[...], wo_bf16_ref[:half, :])
          + _dot(df_ref[...], wo_bf16_ref[half:, :]))
    x1_ref[...] = x1
    ms = jnp.mean(x1 * x1, axis=-1, keepdims=True)
    h2 = x1 * lax.rsqrt(ms + NORM_EPS) * g_ref[...]
    h2_ref[...] = h2

    logits = _dot(h2.astype(BF16), wr_ref[...]) + br_ref[...]
    lane = lax.broadcasted_iota(jnp.int32, (tm, LANES), 1).astype(F32)
    ninf = -jnp.inf

    def first_argmax(vals):
        top = jnp.max(vals, axis=-1, keepdims=True)
        idx = jnp.min(jnp.where(vals == top, lane, float(LANES)), axis=-1, keepdims=True)
        return top, idx

    gl = jnp.where(lane < N_GROUPS, logits, ninf)
    gmax, gidx = first_argmax(gl)
    g_val = 1.0 / jnp.sum(jnp.exp(gl - gmax), axis=-1, keepdims=True)
    lo = N_GROUPS + EXPERTS_PER_GROUP * gidx
    el = jnp.where((lane >= lo) & (lane < lo + EXPERTS_PER_GROUP), logits, ninf)
    l1, i1 = first_argmax(el)
    l2, i2 = first_argmax(jnp.where(lane == i1, ninf, el))
    r = jnp.exp(l2 - l1)
    w1 = g_val / (1.0 + r)
    w2 = g_val * r / (1.0 + r)
    e1 = i1 - N_GROUPS
    e2 = i2 - N_GROUPS

    @pl.when(i == 0)
    def _():
        cnt_ref[...] = jnp.zeros_like(cnt_ref)

    onehot = jnp.where((lane == e1) | (lane == e2), 1.0, 0.0)
    before = _dot(tri_ref[...], onehot.astype(BF16)) + cnt_ref[0:1, :]
    rank1 = jnp.sum(jnp.where(lane == e1, before, 0.0), axis=-1, keepdims=True)
    rank2 = jnp.sum(jnp.where(lane == e2, before, 0.0), axis=-1, keepdims=True)
    cnt_ref[0:1, :] = cnt_ref[0:1, :] + jnp.sum(onehot, axis=0, keepdims=True)

    ri = jnp.where(lane == 0, e1, jnp.where(lane == 1, e2,
                   jnp.where(lane == 2, rank1, jnp.where(lane == 3, rank2, 0.0))))
    ri_ref[0] = jnp.transpose(ri)[:8].astype(jnp.int32)
    rf_ref[...] = jnp.where(lane == 0, w1, jnp.where(lane == 1, w2, 0.0))


def _outproj_call(x2d, sb_out, d_out, w_out, gain, w_router, b_router, tri, *, tm):
    t, d = x2d.shape
    half = sb_out.shape[1]
    assert t % tm == 0
    row = lambda i: (i, 0)
    fixed = lambda i: (0, 0)
    return pl.pallas_call(
        _outproj_kernel,
        grid=(t // tm,),
        in_specs=[
            pl.BlockSpec((tm, d), row),
            pl.BlockSpec((tm, half), row),
            pl.BlockSpec((tm, half), row),
            pl.BlockSpec((2 * half, d), fixed, pipeline_mode=pl.Buffered(1)),
            pl.BlockSpec((1, d), fixed),
            pl.BlockSpec((d, LANES), fixed),
            pl.BlockSpec((1, LANES), fixed),
            pl.BlockSpec((tm, tm), fixed),
        ],
        out_specs=[
            pl.BlockSpec((tm, d), row),
            pl.BlockSpec((tm, d), row),
            pl.BlockSpec((1, 8, tm), lambda i: (i, 0, 0)),
            pl.BlockSpec((tm, LANES), row),
            pl.BlockSpec((8, LANES), fixed),
        ],
        out_shape=[
            jax.ShapeDtypeStruct((t, d), F32),
            jax.ShapeDtypeStruct((t, d), F32),
            jax.ShapeDtypeStruct((t // tm, 8, tm), jnp.int32),
            jax.ShapeDtypeStruct((t, LANES), F32),
            jax.ShapeDtypeStruct((8, LANES), F32),
        ],
        scratch_shapes=[pltpu.VMEM((2 * half, d), BF16)],
        compiler_params=_cparams(("arbitrary",)),
        name="outproj_router",
    )(x2d, sb_out, d_out, w_out, gain, w_router, b_router, tri)


def _dispatch_kernel(pos_hbm, h_ref, xs_hbm, pos_smem, sem_idx, sem, *, td):
    i = pl.program_id(0)
    idx_copy = pltpu.make_async_copy(pos_hbm.at[i], pos_smem, sem_idx)
    idx_copy.start()
    idx_copy.wait()

    def row_copy(t, k):
        return pltpu.make_async_copy(h_ref.at[pl.ds(t, 1)],
                                     xs_hbm.at[pl.ds(pos_smem[0, k * td + t], 1)], sem)

    def issue(t, c):
        row_copy(t, 0).start()
        row_copy(t, 1).start()
        return c

    lax.fori_loop(0, td, issue, 0, unroll=DMA_UNROLL)

    def drain(t, c):
        row_copy(t, 0).wait()
        row_copy(t, 1).wait()
        return c

    lax.fori_loop(0, td, drain, 0, unroll=DMA_UNROLL)


def _dispatch_call(pos3, h2, n_rows):
    t, d = h2.shape
    td = pos3.shape[2] // 2
    assert pos3.shape[0] * td == t
    return pl.pallas_call(
        functools.partial(_dispatch_kernel, td=td),
        grid=(t // td,),
        in_specs=[pl.BlockSpec(memory_space=pl.ANY), pl.BlockSpec((td, d), lambda i: (i, 0))],
        out_specs=pl.BlockSpec(memory_space=pl.ANY),
        out_shape=jax.ShapeDtypeStruct((n_rows, d), h2.dtype),
        scratch_shapes=[pltpu.SMEM((1, 2 * td), jnp.int32), pltpu.SemaphoreType.DMA,
                        pltpu.SemaphoreType.DMA],
        compiler_params=_cparams(("arbitrary",)),
        name="dispatch",
    )(pos3, h2)


def _experts_kernel(wt_ref, we_ref, nw_ref, lo_ref, hi_ref, slot_ref, nxt_ref, xs_ref, wg_hbm, wu_hbm,
                    wd_hbm, ys_ref, wg_buf, wu_buf, wd_buf, sem):
    w = pl.program_id(0)
    tx = xs_ref.shape[0]
    tile = wt_ref[w]
    e = we_ref[w]
    slot = slot_ref[w]
    before = jnp.maximum(w - 1, 0)
    first = (w == 0) | (wt_ref[before] != tile)
    first_of_expert = (w == 0) | (we_ref[before] != e)

    def weight_copies(expert, s):
        return [pltpu.make_async_copy(hbm.at[expert], buf.at[s], sem.at[s, i])
                for i, (hbm, buf) in enumerate(((wg_hbm, wg_buf), (wu_hbm, wu_buf), (wd_hbm, wd_buf)))]

    @pl.when(w == 0)
    def _():
        for copy in weight_copies(e, slot):
            copy.start()

    @pl.when(first_of_expert & (w < nw_ref[0]))
    def _():
        for copy in weight_copies(e, slot):
            copy.wait()

        @pl.when(nxt_ref[w] >= 0)
        def _():
            for copy in weight_copies(nxt_ref[w], 1 - slot):
                copy.start()

    @pl.when(w < nw_ref[0])
    def _():
        rows = tile * tx + lax.broadcasted_iota(jnp.int32, (tx, 1), 0)
        member = (rows >= lo_ref[e]) & (rows < hi_ref[e])
        x = xs_ref[...].astype(BF16)
        gate = _dot(x, wg_buf[slot].astype(BF16))
        up = _dot(x, wu_buf[slot].astype(BF16))
        hid = gate * (1.0 / (1.0 + jnp.exp(-gate))) * up
        y = _dot(jnp.where(member, hid, 0.0).astype(BF16), wd_buf[slot].astype(BF16))

        @pl.when(first)
        def _():
            ys_ref[...] = y

        @pl.when(jnp.logical_not(first))
        def _():
            ys_ref[...] += y


def _experts_call(work_tile, work_expert, n_work, seg_lo, seg_hi, work_slot, work_next, xs, w_gate,
                  w_up, w_down, *, tx):
    p, d = xs.shape
    de = w_gate.shape[2]
    assert p % tx == 0
    n_items = work_tile.shape[0]
    tile = lambda w, wt, *_: (wt[w], 0)
    any_space = pl.BlockSpec(memory_space=pl.ANY)
    return pl.pallas_call(
        _experts_kernel,
        grid_spec=pltpu.PrefetchScalarGridSpec(
            num_scalar_prefetch=7,
            grid=(n_items,),
            in_specs=[pl.BlockSpec((tx, d), tile), any_space, any_space, any_space],
            out_specs=pl.BlockSpec((tx, d), tile),
            scratch_shapes=[pltpu.VMEM((2, d, de), F32), pltpu.VMEM((2, d, de), F32),
                            pltpu.VMEM((2, de, d), F32), pltpu.SemaphoreType.DMA((2, 3))],
        ),
        out_shape=jax.ShapeDtypeStruct((p, d), F32),
        compiler_params=_cparams(("arbitrary",)),
        name="experts",
    )(work_tile, work_expert, n_work, seg_lo, seg_hi, work_slot, work_next, xs, w_gate, w_up, w_down)


def _combine_kernel(pos_hbm, x1_ref, rf_ref, g_ref, ys_hbm, o_ref, pos_smem, y0_ref, y1_ref,
                    sem_idx, sem, *, tc):
    i = pl.program_id(0)
    idx_copy = pltpu.make_async_copy(pos_hbm.at[i], pos_smem, sem_idx)
    idx_copy.start()
    idx_copy.wait()

    def row_copy(t, k):
        dst = y0_ref if k == 0 else y1_ref
        return pltpu.make_async_copy(ys_hbm.at[pl.ds(pos_smem[0, k * tc + t], 1)],
                                     dst.at[pl.ds(t, 1)], sem)

    def issue(t, c):
        row_copy(t, 0).start()
        row_copy(t, 1).start()
        return c

    lax.fori_loop(0, tc, issue, 0, unroll=DMA_UNROLL)

    def drain(t, c):
        row_copy(t, 0).wait()
        row_copy(t, 1).wait()
        return c

    lax.fori_loop(0, tc, drain, 0, unroll=DMA_UNROLL)

    x = x1_ref[...] + rf_ref[:, 0:1] * y0_ref[...] + rf_ref[:, 1:2] * y1_ref[...]
    ms = jnp.mean(x * x, axis=-1, keepdims=True)
    o_ref[...] = x * lax.rsqrt(ms + NORM_EPS) * g_ref[...]


def _combine_call(pos3, x1, rf, gain, ys):
    t, d = x1.shape
    tc = pos3.shape[2] // 2
    assert pos3.shape[0] * tc == t
    row = lambda i: (i, 0)
    return pl.pallas_call(
        functools.partial(_combine_kernel, tc=tc),
        grid=(t // tc,),
        in_specs=[
            pl.BlockSpec(memory_space=pl.ANY),
            pl.BlockSpec((tc, d), row),
            pl.BlockSpec((tc, LANES), row),
            pl.BlockSpec((1, d), lambda i: (0, 0)),
            pl.BlockSpec(memory_space=pl.ANY),
        ],
        out_specs=pl.BlockSpec((tc, d), row),
        out_shape=jax.ShapeDtypeStruct((t, d), F32),
        scratch_shapes=[pltpu.SMEM((1, 2 * tc), jnp.int32), pltpu.VMEM((tc, d), F32),
                        pltpu.VMEM((tc, d), F32), pltpu.SemaphoreType.DMA,
                        pltpu.SemaphoreType.DMA],
        compiler_params=_cparams(("arbitrary",)),
        name="combine",
    )(pos3, x1, rf, gain, ys)


def _pick(n, pref):
    while n % pref:
        pref //= 2
    return pref


def _layer(x2d, b, s, layer, attn_norm_gain, w_in, sb_norm_gain, lq1, lk1, lq2, lk2, subln_gain,
           w_out, ffn_norm_gain, w_gr, b_gr, w_er, b_er, w_gate, w_up, w_down):
    t, d = x2d.shape
    tb = _pick(s, 256)
    tm_proj = _pick(s, 1024)
    tm_out = _pick(t, 256)
    tx = 256

    half = HEAD_DIM // 2
    inv_freq = 1.0 / (ROPE_THETA ** (np.arange(half, dtype=np.float64) / half))
    ang = np.arange(s, dtype=np.float64)[:, None] * inv_freq[None, :]
    cos = jnp.asarray(np.concatenate([np.cos(ang), np.cos(ang)], axis=-1), F32)
    sin = jnp.asarray(np.concatenate([-np.sin(ang), np.sin(ang)], axis=-1), F32)

    proj = _proj_call(x2d, attn_norm_gain.reshape(1, d), w_in, cos, sin, s,
                      tm=tm_proj, tn=512)
    proj3d = proj.reshape(b, s, proj.shape[1])

    r = jnp.arange(tb)
    tri_suffix = (r[:, None] >= r[None, :]).astype(BF16)
    sb_out = _sb_call(proj3d, tri_suffix, sb_norm_gain.reshape(1, HEAD_DIM), tb=tb,
                      nsub=_pick(s // tb, 8))

    lam_init = 0.8 - 0.6 * math.exp(-0.3 * layer)
    d_out = _diff_call(proj3d, lq1.reshape(1, -1), lk1.reshape(1, -1), lq2.reshape(1, -1),
                       lk2.reshape(1, -1), subln_gain.reshape(1, -1), lam_init, tq=2 * tb)

    w_router = jnp.zeros((d, LANES), F32)
    w_router = w_router.at[:, :N_GROUPS].set(w_gr).at[:, N_GROUPS:N_GROUPS + N_EXPERTS].set(w_er)
    b_router = jnp.zeros((1, LANES), F32)
    b_router = b_router.at[0, :N_GROUPS].set(b_gr).at[0, N_GROUPS:N_GROUPS + N_EXPERTS].set(b_er)
    rr = jnp.arange(tm_out)
    tri_before = (rr[None, :] < rr[:, None]).astype(BF16)
    x1, h2, ri, rf, cnt = _outproj_call(
        x2d, sb_out.reshape(t, -1), d_out.reshape(t, -1), w_out,
        ffn_norm_gain.reshape(1, d), w_router.astype(BF16), b_router, tri_before, tm=tm_out)

    counts = cnt[0, :N_EXPERTS].astype(jnp.int32)
    seg_hi = jnp.cumsum(counts)
    seg_lo = seg_hi - counts
    experts = jnp.arange(N_EXPERTS, dtype=jnp.int32)
    ri = ri.transpose(1, 0, 2).reshape(8, t)

    def sorted_position(slot):
        hit = ri[slot][None, :] == experts[:, None]
        return jnp.sum(jnp.where(hit, seg_lo[:, None], 0), axis=0) + ri[2 + slot]

    def tiled_positions(tile):
        return jnp.stack([sorted_position(0).reshape(-1, tile), sorted_position(1).reshape(-1, tile)],
                         axis=1).reshape(-1, 1, 2 * tile)

    first_tile = seg_lo // tx
    items = jnp.where(counts > 0, (seg_hi - 1) // tx - first_tile + 1, 0)
    item_hi = jnp.cumsum(items)
    n_work = item_hi[-1]
    w = jnp.minimum(jnp.arange(2 * t // tx + N_EXPERTS - 1, dtype=jnp.int32), n_work - 1)
    work_expert = jnp.sum((item_hi[None, :] <= w[:, None]).astype(jnp.int32), axis=1)
    work_tile = (first_tile[work_expert] + w - (item_hi - items)[work_expert]).astype(jnp.int32)

    xs = _dispatch_call(tiled_positions(_pick(t, 1024)), h2, 2 * t)
    used = counts > 0
    work_slot = ((jnp.cumsum(used.astype(jnp.int32)) - 1) % 2)[work_expert].astype(jnp.int32)
    later_used = (experts[None, :] > experts[:, None]) & used[None, :]
    next_used = jnp.min(jnp.where(later_used, experts[None, :], N_EXPERTS), axis=1)
    next_used = jnp.where(next_used < N_EXPERTS, next_used, -1)
    run_start = jnp.concatenate([jnp.ones((1,), bool), work_expert[1:] != work_expert[:-1]])
    work_next = jnp.where(run_start, next_used[work_expert], -1).astype(jnp.int32)
    ys = _experts_call(work_tile, work_expert, n_work.reshape(1).astype(jnp.int32), seg_lo, seg_hi,
                       work_slot, work_next, xs, w_gate, w_up, w_down, tx=tx)
    return x1, tiled_positions(_pick(t, 512)), rf, ys


def kernel(x, attn_norm_gain, w_in, sb_norm_gain, diff_lambda_q1, diff_lambda_k1, diff_lambda_q2,
           diff_lambda_k2, diff_subln_gain, w_out, ffn_norm_gain, w_group_router, b_group_router,
           w_expert_router, b_expert_router, w_gate, w_up, w_down, final_norm_gain):
    b, s, d = x.shape
    assert w_in.shape[0] == 1, "the combine stage fuses the final norm: single-layer stacks only"
    layer = 0
    x2d = x.reshape(b * s, d)
    x1, pos, rf, ys = _layer(
        x2d, b, s, layer, attn_norm_gain[layer], w_in[layer], sb_norm_gain[layer],
        diff_lambda_q1[layer], diff_lambda_k1[layer], diff_lambda_q2[layer],
        diff_lambda_k2[layer], diff_subln_gain[layer], w_out[layer], ffn_norm_gain[layer],
        w_group_router[layer], b_group_router[layer], w_expert_router[layer],
        b_expert_router[layer], w_gate[layer], w_up[layer], w_down[layer])
    out = _combine_call(pos, x1, rf, final_norm_gain.reshape(1, d), ys)
    return out.reshape(b, s, d)
```

```python
import functools
import math

import jax
import jax.numpy as jnp
import numpy as np
from jax import lax
from jax.experimental import pallas as pl
from jax.experimental.pallas import tpu as pltpu

F32 = jnp.float32
BF16 = jnp.bfloat16

HEAD_DIM = 128
N_SB_HEADS = 8
N_DIFF_HEADS = 4
SECTION = 1024
CHUNK = 64
ROPE_THETA = 10000.0
N_GROUPS = 4
EXPERTS_PER_GROUP = 8
N_EXPERTS = N_GROUPS * EXPERTS_PER_GROUP
NORM_EPS = 1e-6
NEG_INF = -1e30
LANES = 128
UNDERFLOW_LOG2 = -200.0
DMA_UNROLL = 8
SCALE_LOG2E = math.log2(math.e) / math.sqrt(HEAD_DIM)

VMEM_LIMIT = 56 * 1024 * 1024


def _cparams(sem):
    return pltpu.CompilerParams(dimension_semantics=sem, vmem_limit_bytes=VMEM_LIMIT)


def _dot(a, b):
    return jnp.dot(a, b, preferred_element_type=F32)


def _dot_nt(a, b):
    return lax.dot_general(a, b, (((1,), (1,)), ((), ())), preferred_element_type=F32)


def _proj_kernel(x_ref, g_ref, w_ref, cos_ref, sin_ref, o_ref, h_ref, *, tn, rows):
    j = pl.program_id(1)
    tm = x_ref.shape[0]
    sec = j // (SECTION // tn)

    def normalize(rs):
        x = x_ref[rs, :]
        ms = jnp.mean(x * x, axis=-1, keepdims=True)
        h_ref[rs, :] = (x * lax.rsqrt(ms + NORM_EPS) * g_ref[...]).astype(BF16)

    def chunks(epilogue, first_visit=False):
        for r in range(0, tm, rows):
            rs = slice(r, r + rows)
            if first_visit:
                normalize(rs)
            epilogue(rs, _dot(h_ref[rs, :], w_ref[...].astype(BF16)))

    def rotary(scale):
        def epilogue(rs, acc):
            for c in range(tn // HEAD_DIM):
                cs = slice(c * HEAD_DIM, (c + 1) * HEAD_DIM)
                a = acc[:, cs]
                rot = a * cos_ref[rs, :] + pltpu.roll(a, HEAD_DIM // 2, 1) * sin_ref[rs, :]
                if scale != 1.0:
                    rot = rot * scale
                o_ref[rs, cs] = rot.astype(BF16)
        chunks(epilogue)

    def scaled(scale, first_visit=False):
        def epilogue(rs, acc):
            o_ref[rs, :] = (acc if scale == 1.0 else acc * scale).astype(BF16)
        chunks(epilogue, first_visit)

    @pl.when(j == 0)
    def _():
        scaled(-SCALE_LOG2E, first_visit=True)

    @pl.when((sec == 0) & (j > 0))
    def _():
        scaled(-SCALE_LOG2E)

    @pl.when((sec == 1) | (sec == 2) | (sec == 5))
    def _():
        scaled(1.0)

    @pl.when(sec == 3)
    def _():
        rotary(SCALE_LOG2E)

    @pl.when(sec == 4)
    def _():
        rotary(1.0)


def _proj_call(x2d, gain, w_in, cos, sin, seq, *, tm, tn):
    t, d = x2d.shape
    n = w_in.shape[1]
    assert t % tm == 0 and seq % tm == 0 and n % tn == 0 and SECTION % tn == 0
    rows = min(tm, 256)
    nseq = seq // tm
    return pl.pallas_call(
        functools.partial(_proj_kernel, tn=tn, rows=rows),
        grid=(t // tm, n // tn),
        in_specs=[
            pl.BlockSpec((tm, d), lambda i, j: (i, 0)),
            pl.BlockSpec((1, d), lambda i, j: (0, 0)),
            pl.BlockSpec((d, tn), lambda i, j: (0, j)),
            pl.BlockSpec((tm, HEAD_DIM), lambda i, j: (i % nseq, 0)),
            pl.BlockSpec((tm, HEAD_DIM), lambda i, j: (i % nseq, 0)),
        ],
        out_specs=pl.BlockSpec((tm, tn), lambda i, j: (i, j)),
        out_shape=jax.ShapeDtypeStruct((t, n), BF16),
        scratch_shapes=[pltpu.VMEM((tm, d), BF16)],
        compiler_params=_cparams(("arbitrary", "arbitrary")),
        name="proj",
    )(x2d, gain, w_in, cos, sin)


def _sb_kernel(q_ref, k_ref, v_ref, tri_ref, g_ref, o_ref, acc_ref, *, tb, nsub):
    qi = pl.program_id(2)
    tq = nsub * tb
    row = lax.broadcasted_iota(jnp.int32, (tb, tb), 0)
    col = lax.broadcasted_iota(jnp.int32, (tb, tb), 1)
    strict = col < row

    def scores(rows, kj):
        start = pl.multiple_of(kj * tb, tb)
        return _dot_nt(q_ref[0, rows, :], k_ref[0, pl.ds(start, tb), :])

    def keep_logs(n, mask):
        lk = jnp.minimum(n, 0.0) - jnp.log2(1.0 + jnp.exp2(-jnp.abs(n)))
        if mask is not None:
            lk = jnp.where(mask, lk, 0.0)
        return lk.astype(BF16)

    def accumulate(rows, kj, n, lk, carry, mask):
        start = pl.multiple_of(kj * tb, tb)
        cum = _dot(lk, tri_ref[...])
        a = jnp.exp2(cum + carry - n)
        if mask is not None:
            a = jnp.where(mask, a, 0.0)
        acc_ref[rows, :] += _dot(a.astype(BF16), v_ref[0, pl.ds(start, tb), :])
        return carry + cum[:, 0:1]

    def several(rows, kjs, carry, mask=None):
        staged = []
        for kj in kjs:
            n = scores(rows, kj)
            staged.append((kj, n, keep_logs(n, mask)))
        for kj, n, lk in staged:
            carry = accumulate(rows, kj, n, lk, carry, mask)
        return carry

    acc_ref[...] = jnp.zeros_like(acc_ref)
    base = qi * nsub
    blocks = [slice(r * tb, (r + 1) * tb) for r in range(nsub)]
    chains = []
    for r in range(nsub):
        chains.append((r, base + r, strict))
        chains.append((r, jnp.maximum(base + r - 1, 0),
                       jnp.broadcast_to(base > 0, (tb, tb)) if r == 0 else None))
    staged = []
    for r, kj, mask in chains:
        n = scores(blocks[r], kj)
        staged.append((n, keep_logs(n, mask)))
    carry = [jnp.zeros((tb, 1), F32)] * nsub
    for (r, kj, mask), (n, lk) in zip(chains, staged):
        carry[r] = accumulate(blocks[r], kj, n, lk, carry[r], mask)

    def walk_back(rows, first, carry):
        def still_live(kj, c):
            return (kj >= 0) & (jnp.max(c) > UNDERFLOW_LOG2)

        def one_block(state):
            kj, c, _ = state
            c = several(rows, [kj], c)
            return kj - 1, c, still_live(kj - 1, c)

        lax.while_loop(lambda state: state[2], one_block, (first, carry, still_live(first, carry)))

    for r in range(nsub):
        walk_back(blocks[r], base + r - 2, carry[r])

    o = acc_ref[...]
    ms = jnp.mean(o * o, axis=-1, keepdims=True)
    o_ref[0] = (o * lax.rsqrt(ms + NORM_EPS) * g_ref[...]).astype(BF16)


def _sb_call(proj3d, tri, gain, *, tb, nsub):
    b, s, _ = proj3d.shape
    tq = nsub * tb
    assert s % tq == 0
    hq, hk, hv = 0, SECTION // HEAD_DIM, 2 * SECTION // HEAD_DIM
    return pl.pallas_call(
        functools.partial(_sb_kernel, tb=tb, nsub=nsub),
        grid=(b, N_SB_HEADS, s // tq),
        in_specs=[
            pl.BlockSpec((1, tq, HEAD_DIM), lambda bi, h, i: (bi, i, hq + h)),
            pl.BlockSpec((1, s, HEAD_DIM), lambda bi, h, i: (bi, 0, hk + h)),
            pl.BlockSpec((1, s, HEAD_DIM), lambda bi, h, i: (bi, 0, hv + h)),
            pl.BlockSpec((tb, tb), lambda bi, h, i: (0, 0)),
            pl.BlockSpec((1, HEAD_DIM), lambda bi, h, i: (0, 0)),
        ],
        out_specs=pl.BlockSpec((1, tq, HEAD_DIM), lambda bi, h, i: (bi, i, h)),
        out_shape=jax.ShapeDtypeStruct((b, s, N_SB_HEADS * HEAD_DIM), BF16),
        scratch_shapes=[pltpu.VMEM((tq, HEAD_DIM), F32)],
        compiler_params=_cparams(("arbitrary", "arbitrary", "arbitrary")),
        name="sb_attn",
    )(proj3d, proj3d, proj3d, tri, gain)


def _diff_kernel(q_ref, k_ref, v_ref, lq1_ref, lk1_ref, lq2_ref, lk2_ref, g_ref, o_ref,
                 acc1_ref, acc2_ref, *, tq, lam_init):
    qi = pl.program_id(2)
    d = HEAD_DIM
    q1 = q_ref[0, :, :d]
    q2 = q_ref[0, :, d:]
    row = lax.broadcasted_iota(jnp.int32, (tq, tq), 0)
    col = lax.broadcasted_iota(jnp.int32, (tq, tq), 1)
    visible = (col // CHUNK) <= (row // CHUNK)

    def probabilities(s, m, l):
        m_new = jnp.maximum(m, jnp.max(s, axis=-1, keepdims=True))
        alpha = jnp.exp2(m - m_new)
        p = jnp.exp2(s - m_new)
        return p.astype(BF16), alpha, m_new, alpha * l + jnp.sum(p, axis=-1, keepdims=True)

    def block(start, width, carry, masked):
        m1, l1, m2, l2 = carry
        k = k_ref[0, pl.ds(start, width), :]
        v = v_ref[0, pl.ds(start, width), :]
        s1 = _dot_nt(q1, k[:, :d])
        s2 = _dot_nt(q2, k[:, d:])
        if masked:
            s1 = jnp.where(visible, s1, NEG_INF)
            s2 = jnp.where(visible, s2, NEG_INF)
        p1, alpha1, m1, l1 = probabilities(s1, m1, l1)
        p2, alpha2, m2, l2 = probabilities(s2, m2, l2)
        pv = _dot(jnp.concatenate([p1, p2], axis=0), v)
        acc1_ref[...] = alpha1 * acc1_ref[...] + pv[:tq]
        acc2_ref[...] = alpha2 * acc2_ref[...] + pv[tq:]
        return m1, l1, m2, l2

    acc1_ref[...] = jnp.zeros_like(acc1_ref)
    acc2_ref[...] = jnp.zeros_like(acc2_ref)
    neg = jnp.full((tq, 1), NEG_INF, F32)
    zero = jnp.zeros((tq, 1), F32)
    carry = block(pl.multiple_of(qi * tq, tq), tq, (neg, zero, neg, zero), True)
    odd = qi % 2
    carry = lax.cond(odd == 1, lambda c: block(pl.multiple_of((qi - 1) * tq, tq), tq, c, False),
                     lambda c: c, carry)
    _, l1, _, l2 = lax.fori_loop(
        0, qi // 2, lambda i, c: block(pl.multiple_of(2 * i * tq, 2 * tq), 2 * tq, c, False), carry)

    lam = (jnp.exp(jnp.sum(lq1_ref[...] * lk1_ref[...], axis=-1, keepdims=True))
           - jnp.exp(jnp.sum(lq2_ref[...] * lk2_ref[...], axis=-1, keepdims=True)) + lam_init)
    o = acc1_ref[...] / l1 - lam * (acc2_ref[...] / l2)
    ms = jnp.mean(o * o, axis=-1, keepdims=True)
    o_ref[0] = (o * lax.rsqrt(ms + NORM_EPS) * g_ref[...] * (1.0 - lam_init)).astype(BF16)


def _diff_call(proj3d, lq1, lk1, lq2, lk2, gain, lam_init, *, tq):
    b, s, _ = proj3d.shape
    assert s % tq == 0 and tq % CHUNK == 0
    w = 2 * HEAD_DIM
    hq, hk, hv = 3 * SECTION // w, 4 * SECTION // w, 5 * SECTION // w
    vec = pl.BlockSpec((1, HEAD_DIM), lambda bi, h, i: (0, 0))
    return pl.pallas_call(
        functools.partial(_diff_kernel, tq=tq, lam_init=lam_init),
        grid=(b, N_DIFF_HEADS, s // tq),
        in_specs=[
            pl.BlockSpec((1, tq, w), lambda bi, h, i: (bi, i, hq + h)),
            pl.BlockSpec((1, s, w), lambda bi, h, i: (bi, 0, hk + h)),
            pl.BlockSpec((1, s, w), lambda bi, h, i: (bi, 0, hv + h)),
            vec, vec, vec, vec,
            pl.BlockSpec((1, w), lambda bi, h, i: (0, 0)),
        ],
        out_specs=pl.BlockSpec((1, tq, w), lambda bi, h, i: (bi, i, h)),
        out_shape=jax.ShapeDtypeStruct((b, s, N_DIFF_HEADS * w), BF16),
        scratch_shapes=[pltpu.VMEM((tq, w), F32), pltpu.VMEM((tq, w), F32)],
        compiler_params=_cparams(("arbitrary", "arbitrary", "arbitrary")),
        name="diff_attn",
    )(proj3d, proj3d, proj3d, lq1, lk1, lq2, lk2, gain)


def _outproj_kernel(x_ref, sb_ref, df_ref, wo_ref, g_ref, wr_ref, br_ref, tri_ref,
                    x1_ref, h2_ref, ri_ref, rf_ref, cnt_ref, wo_bf16_ref, prev_ref):
    i = pl.program_id(0)
    tm = x_ref.shape[0]
    half = sb_ref.shape[1]

    @pl.when(i == 0)
    def _():
        wo_bf16_ref[...] = wo_ref[...].astype(BF16)
        prev_ref[...] = jnp.zeros_like(prev_ref)
        cnt_ref[...] = jnp.zeros_like(cnt_ref)

    def project(cols):
        x1_ref[:, cols] = (x_ref[:, cols] + _dot(sb_ref[...], wo_bf16_ref[:half, cols])
                           + _dot(df_ref[...], wo_bf16_ref[half:, cols]))

    d = x_ref.shape[1]
    project(slice(0, d // 2))

    x1 = prev_ref[...]
    ms = jnp.mean(x1 * x1, axis=-1, keepdims=True)
    h2 = x1 * lax.rsqrt(ms + NORM_EPS) * g_ref[...]
    h2_ref[...] = h2

    logits = _dot(h2.astype(BF16), wr_ref[...]) + br_ref[...]
    project(slice(d // 2, d))
    lane = lax.broadcasted_iota(jnp.int32, (tm, LANES), 1).astype(F32)
    ninf = -jnp.inf

    def first_argmax(vals):
        top = jnp.max(vals, axis=-1, keepdims=True)
        idx = jnp.min(jnp.where(vals == top, lane, float(LANES)), axis=-1, keepdims=True)
        return top, idx

    gl = jnp.where(lane < N_GROUPS, logits, ninf)
    gmax, gidx = first_argmax(gl)
    g_val = 1.0 / jnp.sum(jnp.exp(gl - gmax), axis=-1, keepdims=True)
    lo = N_GROUPS + EXPERTS_PER_GROUP * gidx
    el = jnp.where((lane >= lo) & (lane < lo + EXPERTS_PER_GROUP), logits, ninf)
    l1, i1 = first_argmax(el)
    l2, i2 = first_argmax(jnp.where(lane == i1, ninf, el))
    r = jnp.exp(l2 - l1)
    w1 = g_val / (1.0 + r)
    w2 = g_val * r / (1.0 + r)
    e1 = i1 - N_GROUPS
    e2 = i2 - N_GROUPS

    onehot = jnp.where(((lane == e1) | (lane == e2)) & (i > 0), 1.0, 0.0)
    before = _dot(tri_ref[...], onehot.astype(BF16)) + cnt_ref[0:1, :]
    rank1 = jnp.sum(jnp.where(lane == e1, before, 0.0), axis=-1, keepdims=True)
    rank2 = jnp.sum(jnp.where(lane == e2, before, 0.0), axis=-1, keepdims=True)
    cnt_ref[0:1, :] = cnt_ref[0:1, :] + jnp.sum(onehot, axis=0, keepdims=True)

    ri = jnp.where(lane == 0, e1, jnp.where(lane == 1, e2,
                   jnp.where(lane == 2, rank1, jnp.where(lane == 3, rank2, 0.0))))
    ri_ref[0] = jnp.transpose(ri)[:8].astype(jnp.int32)
    rf_ref[...] = jnp.where(lane == 0, w1, jnp.where(lane == 1, w2, 0.0))

    prev_ref[...] = x1_ref[...]


def _outproj_call(x2d, sb_out, d_out, w_out, gain, w_router, b_router, tri, *, tm):
    t, d = x2d.shape
    half = sb_out.shape[1]
    assert t % tm == 0
    n = t // tm
    row = lambda i: (jnp.minimum(i, n - 1), 0)
    routed = lambda i: (jnp.maximum(i - 1, 0), 0)
    fixed = lambda i: (0, 0)
    return pl.pallas_call(
        _outproj_kernel,
        grid=(n + 1,),
        in_specs=[
            pl.BlockSpec((tm, d), row),
            pl.BlockSpec((tm, half), row),
            pl.BlockSpec((tm, half), row),
            pl.BlockSpec((2 * half, d), fixed, pipeline_mode=pl.Buffered(1)),
            pl.BlockSpec((1, d), fixed),
            pl.BlockSpec((d, LANES), fixed),
            pl.BlockSpec((1, LANES), fixed),
            pl.BlockSpec((tm, tm), fixed),
        ],
        out_specs=[
            pl.BlockSpec((tm, d), row),
            pl.BlockSpec((tm, d), routed),
            pl.BlockSpec((1, 8, tm), lambda i: (jnp.maximum(i - 1, 0), 0, 0)),
            pl.BlockSpec((tm, LANES), routed),
            pl.BlockSpec((8, LANES), fixed),
        ],
        out_shape=[
            jax.ShapeDtypeStruct((t, d), F32),
            jax.ShapeDtypeStruct((t, d), F32),
            jax.ShapeDtypeStruct((t // tm, 8, tm), jnp.int32),
            jax.ShapeDtypeStruct((t, LANES), F32),
            jax.ShapeDtypeStruct((8, LANES), F32),
        ],
        scratch_shapes=[pltpu.VMEM((2 * half, d), BF16), pltpu.VMEM((tm, d), F32)],
        compiler_params=_cparams(("arbitrary",)),
        name="outproj_router",
    )(x2d, sb_out, d_out, w_out, gain, w_router, b_router, tri)


def _dispatch_kernel(pos_hbm, h_ref, xs_hbm, pos_smem, sem_idx, sem, *, td):
    i = pl.program_id(0)
    idx_copy = pltpu.make_async_copy(pos_hbm.at[i], pos_smem, sem_idx)
    idx_copy.start()
    idx_copy.wait()

    def row_copy(t, k):
        return pltpu.make_async_copy(h_ref.at[pl.ds(t, 1)],
                                     xs_hbm.at[pl.ds(pos_smem[0, k * td + t], 1)], sem)

    def issue(t, c):
        row_copy(t, 0).start()
        row_copy(t, 1).start()
        return c

    lax.fori_loop(0, td, issue, 0, unroll=DMA_UNROLL)

    def drain(t, c):
        row_copy(t, 0).wait()
        row_copy(t, 1).wait()
        return c

    lax.fori_loop(0, td, drain, 0, unroll=DMA_UNROLL)


def _dispatch_call(pos3, h2, n_rows):
    t, d = h2.shape
    td = pos3.shape[2] // 2
    assert pos3.shape[0] * td == t
    return pl.pallas_call(
        functools.partial(_dispatch_kernel, td=td),
        grid=(t // td,),
        in_specs=[pl.BlockSpec(memory_space=pl.ANY), pl.BlockSpec((td, d), lambda i: (i, 0))],
        out_specs=pl.BlockSpec(memory_space=pl.ANY),
        out_shape=jax.ShapeDtypeStruct((n_rows, d), h2.dtype),
        scratch_shapes=[pltpu.SMEM((1, 2 * td), jnp.int32), pltpu.SemaphoreType.DMA,
                        pltpu.SemaphoreType.DMA],
        compiler_params=_cparams(("arbitrary",)),
        name="dispatch",
    )(pos3, h2)


def _experts_kernel(wt_ref, we_ref, nw_ref, lo_ref, hi_ref, slot_ref, nxt_ref, xs_ref, wg_hbm, wu_hbm,
                    wd_hbm, ys_ref, wg_buf, wu_buf, wd_buf, sem):
    w = pl.program_id(0)
    tx = xs_ref.shape[0]
    tile = wt_ref[w]
    e = we_ref[w]
    slot = slot_ref[w]
    before = jnp.maximum(w - 1, 0)
    first = (w == 0) | (wt_ref[before] != tile)
    first_of_expert = (w == 0) | (we_ref[before] != e)

    def weight_copies(expert, s):
        return [pltpu.make_async_copy(hbm.at[expert], buf.at[s], sem.at[s, i])
                for i, (hbm, buf) in enumerate(((wg_hbm, wg_buf), (wu_hbm, wu_buf), (wd_hbm, wd_buf)))]

    @pl.when(w == 0)
    def _():
        for copy in weight_copies(e, slot):
            copy.start()

    @pl.when(first_of_expert & (w < nw_ref[0]))
    def _():
        for copy in weight_copies(e, slot):
            copy.wait()

        @pl.when(nxt_ref[w] >= 0)
        def _():
            for copy in weight_copies(nxt_ref[w], 1 - slot):
                copy.start()

    @pl.when(w < nw_ref[0])
    def _():
        rows = tile * tx + lax.broadcasted_iota(jnp.int32, (tx, 1), 0)
        member = (rows >= lo_ref[e]) & (rows < hi_ref[e])
        x = xs_ref[...].astype(BF16)
        gate = _dot(x, wg_buf[slot].astype(BF16))
        up = _dot(x, wu_buf[slot].astype(BF16))
        hid = gate * (1.0 / (1.0 + jnp.exp(-gate))) * up
        y = _dot(jnp.where(member, hid, 0.0).astype(BF16), wd_buf[slot].astype(BF16))

        @pl.when(first)
        def _():
            ys_ref[...] = y

        @pl.when(jnp.logical_not(first))
        def _():
            ys_ref[...] += y


def _experts_call(work_tile, work_expert, n_work, seg_lo, seg_hi, work_slot, work_next, xs, w_gate,
                  w_up, w_down, *, tx):
    p, d = xs.shape
    de = w_gate.shape[2]
    assert p % tx == 0
    n_items = work_tile.shape[0]
    tile = lambda w, wt, *_: (wt[w], 0)
    any_space = pl.BlockSpec(memory_space=pl.ANY)
    return pl.pallas_call(
        _experts_kernel,
        grid_spec=pltpu.PrefetchScalarGridSpec(
            num_scalar_prefetch=7,
            grid=(n_items,),
            in_specs=[pl.BlockSpec((tx, d), tile), any_space, any_space, any_space],
            out_specs=pl.BlockSpec((tx, d), tile),
            scratch_shapes=[pltpu.VMEM((2, d, de), F32), pltpu.VMEM((2, d, de), F32),
                            pltpu.VMEM((2, de, d), F32), pltpu.SemaphoreType.DMA((2, 3))],
        ),
        out_shape=jax.ShapeDtypeStruct((p, d), F32),
        compiler_params=_cparams(("arbitrary",)),
        name="experts",
    )(work_tile, work_expert, n_work, seg_lo, seg_hi, work_slot, work_next, xs, w_gate, w_up, w_down)


def _combine_kernel(pos_hbm, x1_ref, rf_ref, g_ref, ys_hbm, o_ref, pos_smem, y0_ref, y1_ref,
                    sem_idx, sem, *, tc):
    i = pl.program_id(0)
    idx_copy = pltpu.make_async_copy(pos_hbm.at[i], pos_smem, sem_idx)
    idx_copy.start()
    idx_copy.wait()

    def row_copy(t, k):
        dst = y0_ref if k == 0 else y1_ref
        return pltpu.make_async_copy(ys_hbm.at[pl.ds(pos_smem[0, k * tc + t], 1)],
                                     dst.at[pl.ds(t, 1)], sem)

    def issue(t, c):
        row_copy(t, 0).start()
        row_copy(t, 1).start()
        return c

    lax.fori_loop(0, tc, issue, 0, unroll=DMA_UNROLL)

    def drain(t, c):
        row_copy(t, 0).wait()
        row_copy(t, 1).wait()
        return c

    lax.fori_loop(0, tc, drain, 0, unroll=DMA_UNROLL)

    x = x1_ref[...] + rf_ref[:, 0:1] * y0_ref[...] + rf_ref[:, 1:2] * y1_ref[...]
    ms = jnp.mean(x * x, axis=-1, keepdims=True)
    o_ref[...] = x * lax.rsqrt(ms + NORM_EPS) * g_ref[...]


def _combine_call(pos3, x1, rf, gain, ys):
    t, d = x1.shape
    tc = pos3.shape[2] // 2
    assert pos3.shape[0] * tc == t
    row = lambda i: (i, 0)
    return pl.pallas_call(
        functools.partial(_combine_kernel, tc=tc),
        grid=(t // tc,),
        in_specs=[
            pl.BlockSpec(memory_space=pl.ANY),
            pl.BlockSpec((tc, d), row),
            pl.BlockSpec((tc, LANES), row),
            pl.BlockSpec((1, d), lambda i: (0, 0)),
            pl.BlockSpec(memory_space=pl.ANY),
        ],
        out_specs=pl.BlockSpec((tc, d), row),
        out_shape=jax.ShapeDtypeStruct((t, d), F32),
        scratch_shapes=[pltpu.SMEM((1, 2 * tc), jnp.int32), pltpu.VMEM((tc, d), F32),
                        pltpu.VMEM((tc, d), F32), pltpu.SemaphoreType.DMA,
                        pltpu.SemaphoreType.DMA],
        compiler_params=_cparams(("arbitrary",)),
        name="combine",
    )(pos3, x1, rf, gain, ys)


def _pick(n, pref):
    while n % pref:
        pref //= 2
    return pref


def _layer(x2d, b, s, layer, attn_norm_gain, w_in, sb_norm_gain, lq1, lk1, lq2, lk2, subln_gain,
           w_out, ffn_norm_gain, w_gr, b_gr, w_er, b_er, w_gate, w_up, w_down):
    t, d = x2d.shape
    tb = _pick(s, 256)
    tm_proj = _pick(s, 1024)
    tm_out = _pick(t, 256)
    tx = 256

    half = HEAD_DIM // 2
    inv_freq = 1.0 / (ROPE_THETA ** (np.arange(half, dtype=np.float64) / half))
    ang = np.arange(s, dtype=np.float64)[:, None] * inv_freq[None, :]
    cos = jnp.asarray(np.concatenate([np.cos(ang), np.cos(ang)], axis=-1), F32)
    sin = jnp.asarray(np.concatenate([-np.sin(ang), np.sin(ang)], axis=-1), F32)

    proj = _proj_call(x2d, attn_norm_gain.reshape(1, d), w_in, cos, sin, s,
                      tm=tm_proj, tn=512)
    proj3d = proj.reshape(b, s, proj.shape[1])

    r = jnp.arange(tb)
    tri_suffix = (r[:, None] >= r[None, :]).astype(BF16)
    sb_out = _sb_call(proj3d, tri_suffix, sb_norm_gain.reshape(1, HEAD_DIM), tb=tb,
                      nsub=_pick(s // tb, 8))

    lam_init = 0.8 - 0.6 * math.exp(-0.3 * layer)
    d_out = _diff_call(proj3d, lq1.reshape(1, -1), lk1.reshape(1, -1), lq2.reshape(1, -1),
                       lk2.reshape(1, -1), subln_gain.reshape(1, -1), lam_init, tq=2 * tb)

    w_router = jnp.zeros((d, LANES), F32)
    w_router = w_router.at[:, :N_GROUPS].set(w_gr).at[:, N_GROUPS:N_GROUPS + N_EXPERTS].set(w_er)
    b_router = jnp.zeros((1, LANES), F32)
    b_router = b_router.at[0, :N_GROUPS].set(b_gr).at[0, N_GROUPS:N_GROUPS + N_EXPERTS].set(b_er)
    rr = jnp.arange(tm_out)
    tri_before = (rr[None, :] < rr[:, None]).astype(BF16)
    x1, h2, ri, rf, cnt = _outproj_call(
        x2d, sb_out.reshape(t, -1), d_out.reshape(t, -1), w_out,
        ffn_norm_gain.reshape(1, d), w_router.astype(BF16), b_router, tri_before, tm=tm_out)

    counts = cnt[0, :N_EXPERTS].astype(jnp.int32)
    seg_hi = jnp.cumsum(counts)
    seg_lo = seg_hi - counts
    experts = jnp.arange(N_EXPERTS, dtype=jnp.int32)
    ri = ri.transpose(1, 0, 2).reshape(8, t)

    def sorted_position(slot):
        hit = ri[slot][None, :] == experts[:, None]
        return jnp.sum(jnp.where(hit, seg_lo[:, None], 0), axis=0) + ri[2 + slot]

    def tiled_positions(tile):
        return jnp.stack([sorted_position(0).reshape(-1, tile), sorted_position(1).reshape(-1, tile)],
                         axis=1).reshape(-1, 1, 2 * tile)

    first_tile = seg_lo // tx
    items = jnp.where(counts > 0, (seg_hi - 1) // tx - first_tile + 1, 0)
    item_hi = jnp.cumsum(items)
    n_work = item_hi[-1]
    w = jnp.minimum(jnp.arange(2 * t // tx + N_EXPERTS - 1, dtype=jnp.int32), n_work - 1)
    work_expert = jnp.sum((item_hi[None, :] <= w[:, None]).astype(jnp.int32), axis=1)
    work_tile = (first_tile[work_expert] + w - (item_hi - items)[work_expert]).astype(jnp.int32)

    xs = _dispatch_call(tiled_positions(_pick(t, 1024)), h2, 2 * t)
    used = counts > 0
    work_slot = ((jnp.cumsum(used.astype(jnp.int32)) - 1) % 2)[work_expert].astype(jnp.int32)
    later_used = (experts[None, :] > experts[:, None]) & used[None, :]
    next_used = jnp.min(jnp.where(later_used, experts[None, :], N_EXPERTS), axis=1)
    next_used = jnp.where(next_used < N_EXPERTS, next_used, -1)
    run_start = jnp.concatenate([jnp.ones((1,), bool), work_expert[1:] != work_expert[:-1]])
    work_next = jnp.where(run_start, next_used[work_expert], -1).astype(jnp.int32)
    ys = _experts_call(work_tile, work_expert, n_work.reshape(1).astype(jnp.int32), seg_lo, seg_hi,
                       work_slot, work_next, xs, w_gate, w_up, w_down, tx=tx)
    return x1, tiled_positions(_pick(t, 512)), rf, ys


def kernel(x, attn_norm_gain, w_in, sb_norm_gain, diff_lambda_q1, diff_lambda_k1, diff_lambda_q2,
           diff_lambda_k2, diff_subln_gain, w_out, ffn_norm_gain, w_group_router, b_group_router,
           w_expert_router, b_expert_router, w_gate, w_up, w_down, final_norm_gain):
    b, s, d = x.shape
    assert w_in.shape[0] == 1, "the combine stage fuses the final norm: single-layer stacks only"
    layer = 0
    x2d = x.reshape(b * s, d)
    x1, pos, rf, ys = _layer(
        x2d, b, s, layer, attn_norm_gain[layer], w_in[layer], sb_norm_gain[layer],
        diff_lambda_q1[layer], diff_lambda_k1[layer], diff_lambda_q2[layer],
        diff_lambda_k2[layer], diff_subln_gain[layer], w_out[layer], ffn_norm_gain[layer],
        w_group_router[layer], b_group_router[layer], w_expert_router[layer],
        b_expert_router[layer], w_gate[layer], w_up[layer], w_down[layer])
    out = _combine_call(pos, x1, rf, final_norm_gain.reshape(1, d), ys)
    return out.reshape(b, s, d)
```

```python
import functools
import math

import jax
import jax.numpy as jnp
import numpy as np
from jax import lax
from jax.experimental import pallas as pl
from jax.experimental.pallas import tpu as pltpu

F32 = jnp.float32
BF16 = jnp.bfloat16

HEAD_DIM = 128
N_SB_HEADS = 8
N_DIFF_HEADS = 4
SECTION = 1024
CHUNK = 64
ROPE_THETA = 10000.0
N_GROUPS = 4
EXPERTS_PER_GROUP = 8
N_EXPERTS = N_GROUPS * EXPERTS_PER_GROUP
NORM_EPS = 1e-6
NEG_INF = -1e30
LANES = 128
UNDERFLOW_LOG2 = -200.0
DMA_UNROLL = 8
SCALE_LOG2E = math.log2(math.e) / math.sqrt(HEAD_DIM)

VMEM_LIMIT = 56 * 1024 * 1024


def _cparams(sem):
    return pltpu.CompilerParams(dimension_semantics=sem, vmem_limit_bytes=VMEM_LIMIT)


def _dot(a, b):
    return jnp.dot(a, b, preferred_element_type=F32)


def _dot_nt(a, b):
    return lax.dot_general(a, b, (((1,), (1,)), ((), ())), preferred_element_type=F32)


def _proj_kernel(x_ref, g_ref, w_ref, cos_ref, sin_ref, o_ref, h_ref, *, tn, rows):
    j = pl.program_id(1)
    tm = x_ref.shape[0]
    sec = j // (SECTION // tn)

    def normalize(rs):
        x = x_ref[rs, :]
        ms = jnp.mean(x * x, axis=-1, keepdims=True)
        h_ref[rs, :] = (x * lax.rsqrt(ms + NORM_EPS) * g_ref[...]).astype(BF16)

    def chunks(epilogue, first_visit=False):
        for r in range(0, tm, rows):
            rs = slice(r, r + rows)
            if first_visit:
                normalize(rs)
            epilogue(rs, _dot(h_ref[rs, :], w_ref[...].astype(BF16)))

    def rotary(scale):
        def epilogue(rs, acc):
            for c in range(tn // HEAD_DIM):
                cs = slice(c * HEAD_DIM, (c + 1) * HEAD_DIM)
                a = acc[:, cs]
                rot = a * cos_ref[rs, :] + pltpu.roll(a, HEAD_DIM // 2, 1) * sin_ref[rs, :]
                if scale != 1.0:
                    rot = rot * scale
                o_ref[rs, cs] = rot.astype(BF16)
        chunks(epilogue)

    def scaled(scale, first_visit=False):
        def epilogue(rs, acc):
            o_ref[rs, :] = (acc if scale == 1.0 else acc * scale).astype(BF16)
        chunks(epilogue, first_visit)

    @pl.when(j == 0)
    def _():
        scaled(-SCALE_LOG2E, first_visit=True)

    @pl.when((sec == 0) & (j > 0))
    def _():
        scaled(-SCALE_LOG2E)

    @pl.when((sec == 1) | (sec == 2) | (sec == 5))
    def _():
        scaled(1.0)

    @pl.when(sec == 3)
    def _():
        rotary(SCALE_LOG2E)

    @pl.when(sec == 4)
    def _():
        rotary(1.0)


def _proj_call(x2d, gain, w_in, cos, sin, seq, *, tm, tn):
    t, d = x2d.shape
    n = w_in.shape[1]
    assert t % tm == 0 and seq % tm == 0 and n % tn == 0 and SECTION % tn == 0
    rows = min(tm, 256)
    nseq = seq // tm
    return pl.pallas_call(
        functools.partial(_proj_kernel, tn=tn, rows=rows),
        grid=(t // tm, n // tn),
        in_specs=[
            pl.BlockSpec((tm, d), lambda i, j: (i, 0)),
            pl.BlockSpec((1, d), lambda i, j: (0, 0)),
            pl.BlockSpec((d, tn), lambda i, j: (0, j)),
            pl.BlockSpec((tm, HEAD_DIM), lambda i, j: (i % nseq, 0)),
            pl.BlockSpec((tm, HEAD_DIM), lambda i, j: (i % nseq, 0)),
        ],
        out_specs=pl.BlockSpec((tm, tn), lambda i, j: (i, j)),
        out_shape=jax.ShapeDtypeStruct((t, n), BF16),
        scratch_shapes=[pltpu.VMEM((tm, d), BF16)],
        compiler_params=_cparams(("arbitrary", "arbitrary")),
        name="proj",
    )(x2d, gain, w_in, cos, sin)


def _sb_kernel(q_ref, k_ref, v_ref, tri_ref, g_ref, o_ref, acc_ref, *, tb, nsub):
    qi = pl.program_id(2)
    tq = nsub * tb
    row = lax.broadcasted_iota(jnp.int32, (tb, tb), 0)
    col = lax.broadcasted_iota(jnp.int32, (tb, tb), 1)
    strict = col < row

    def scores(rows, kj):
        start = pl.multiple_of(kj * tb, tb)
        return _dot_nt(q_ref[0, rows, :], k_ref[0, pl.ds(start, tb), :])

    def keep_logs(n, mask):
        lk = jnp.minimum(n, 0.0) - jnp.log2(1.0 + jnp.exp2(-jnp.abs(n)))
        if mask is not None:
            lk = jnp.where(mask, lk, 0.0)
        return lk.astype(BF16)

    def accumulate(rows, kj, n, lk, carry, mask):
        start = pl.multiple_of(kj * tb, tb)
        cum = _dot(lk, tri_ref[...])
        a = jnp.exp2(cum + carry - n)
        if mask is not None:
            a = jnp.where(mask, a, 0.0)
        acc_ref[rows, :] += _dot(a.astype(BF16), v_ref[0, pl.ds(start, tb), :])
        return carry + cum[:, 0:1]

    def several(rows, kjs, carry, mask=None):
        staged = []
        for kj in kjs:
            n = scores(rows, kj)
            staged.append((kj, n, keep_logs(n, mask)))
        for kj, n, lk in staged:
            carry = accumulate(rows, kj, n, lk, carry, mask)
        return carry

    acc_ref[...] = jnp.zeros_like(acc_ref)
    base = qi * nsub
    blocks = [slice(r * tb, (r + 1) * tb) for r in range(nsub)]
    chains = []
    for r in range(nsub):
        chains.append((r, base + r, strict))
        chains.append((r, jnp.maximum(base + r - 1, 0),
                       jnp.broadcast_to(base > 0, (tb, tb)) if r == 0 else None))
    staged = []
    for r, kj, mask in chains:
        n = scores(blocks[r], kj)
        staged.append((n, keep_logs(n, mask)))
    carry = [jnp.zeros((tb, 1), F32)] * nsub
    for (r, kj, mask), (n, lk) in zip(chains, staged):
        carry[r] = accumulate(blocks[r], kj, n, lk, carry[r], mask)

    def walk_back(rows, first, carry):
        def still_live(kj, c):
            return (kj >= 0) & (jnp.max(c) > UNDERFLOW_LOG2)

        def one_block(state):
            kj, c, _ = state
            c = several(rows, [kj], c)
            return kj - 1, c, still_live(kj - 1, c)

        lax.while_loop(lambda state: state[2], one_block, (first, carry, still_live(first, carry)))

    for r in range(nsub):
        walk_back(blocks[r], base + r - 2, carry[r])

    o = acc_ref[...]
    ms = jnp.mean(o * o, axis=-1, keepdims=True)
    o_ref[0] = (o * lax.rsqrt(ms + NORM_EPS) * g_ref[...]).astype(BF16)


def _sb_call(proj3d, tri, gain, *, tb, nsub):
    b, s, _ = proj3d.shape
    tq = nsub * tb
    assert s % tq == 0
    hq, hk, hv = 0, SECTION // HEAD_DIM, 2 * SECTION // HEAD_DIM
    return pl.pallas_call(
        functools.partial(_sb_kernel, tb=tb, nsub=nsub),
        grid=(b, N_SB_HEADS, s // tq),
        in_specs=[
            pl.BlockSpec((1, tq, HEAD_DIM), lambda bi, h, i: (bi, i, hq + h)),
            pl.BlockSpec((1, s, HEAD_DIM), lambda bi, h, i: (bi, 0, hk + h)),
            pl.BlockSpec((1, s, HEAD_DIM), lambda bi, h, i: (bi, 0, hv + h)),
            pl.BlockSpec((tb, tb), lambda bi, h, i: (0, 0)),
            pl.BlockSpec((1, HEAD_DIM), lambda bi, h, i: (0, 0)),
        ],
        out_specs=pl.BlockSpec((1, tq, HEAD_DIM), lambda bi, h, i: (bi, i, h)),
        out_shape=jax.ShapeDtypeStruct((b, s, N_SB_HEADS * HEAD_DIM), BF16),
        scratch_shapes=[pltpu.VMEM((tq, HEAD_DIM), F32)],
        compiler_params=_cparams(("arbitrary", "arbitrary", "arbitrary")),
        name="sb_attn",
    )(proj3d, proj3d, proj3d, tri, gain)


def _diff_kernel(q_ref, k_ref, v_ref, lq1_ref, lk1_ref, lq2_ref, lk2_ref, g_ref, o_ref,
                 acc1_ref, acc2_ref, *, tq, lam_init):
    qi = pl.program_id(2)
    d = HEAD_DIM
    q1 = q_ref[0, :, :d]
    q2 = q_ref[0, :, d:]
    row = lax.broadcasted_iota(jnp.int32, (tq, tq), 0)
    col = lax.broadcasted_iota(jnp.int32, (tq, tq), 1)
    visible = (col // CHUNK) <= (row // CHUNK)

    def probabilities(s, m, l):
        m_new = jnp.maximum(m, jnp.max(s, axis=-1, keepdims=True))
        alpha = jnp.exp2(m - m_new)
        p = jnp.exp2(s - m_new)
        return p.astype(BF16), alpha, m_new, alpha * l + jnp.sum(p, axis=-1, keepdims=True)

    def block(start, width, carry, masked):
        m1, l1, m2, l2 = carry
        k = k_ref[0, pl.ds(start, width), :]
        v = v_ref[0, pl.ds(start, width), :]
        s1 = _dot_nt(q1, k[:, :d])
        s2 = _dot_nt(q2, k[:, d:])
        if masked:
            s1 = jnp.where(visible, s1, NEG_INF)
            s2 = jnp.where(visible, s2, NEG_INF)
        p1, alpha1, m1, l1 = probabilities(s1, m1, l1)
        p2, alpha2, m2, l2 = probabilities(s2, m2, l2)
        pv = _dot(jnp.concatenate([p1, p2], axis=0), v)
        acc1_ref[...] = alpha1 * acc1_ref[...] + pv[:tq]
        acc2_ref[...] = alpha2 * acc2_ref[...] + pv[tq:]
        return m1, l1, m2, l2

    acc1_ref[...] = jnp.zeros_like(acc1_ref)
    acc2_ref[...] = jnp.zeros_like(acc2_ref)
    neg = jnp.full((tq, 1), NEG_INF, F32)
    zero = jnp.zeros((tq, 1), F32)
    carry = block(pl.multiple_of(qi * tq, tq), tq, (neg, zero, neg, zero), True)
    odd = qi % 2
    carry = lax.cond(odd == 1, lambda c: block(pl.multiple_of((qi - 1) * tq, tq), tq, c, False),
                     lambda c: c, carry)
    _, l1, _, l2 = lax.fori_loop(
        0, qi // 2, lambda i, c: block(pl.multiple_of(2 * i * tq, 2 * tq), 2 * tq, c, False), carry)

    lam = (jnp.exp(jnp.sum(lq1_ref[...] * lk1_ref[...], axis=-1, keepdims=True))
           - jnp.exp(jnp.sum(lq2_ref[...] * lk2_ref[...], axis=-1, keepdims=True)) + lam_init)
    o = acc1_ref[...] / l1 - lam * (acc2_ref[...] / l2)
    ms = jnp.mean(o * o, axis=-1, keepdims=True)
    o_ref[0] = (o * lax.rsqrt(ms + NORM_EPS) * g_ref[...] * (1.0 - lam_init)).astype(BF16)


def _diff_call(proj3d, lq1, lk1, lq2, lk2, gain, lam_init, *, tq):
    b, s, _ = proj3d.shape
    assert s % tq == 0 and tq % CHUNK == 0
    w = 2 * HEAD_DIM
    hq, hk, hv = 3 * SECTION // w, 4 * SECTION // w, 5 * SECTION // w
    vec = pl.BlockSpec((1, HEAD_DIM), lambda bi, h, i: (0, 0))
    return pl.pallas_call(
        functools.partial(_diff_kernel, tq=tq, lam_init=lam_init),
        grid=(b, N_DIFF_HEADS, s // tq),
        in_specs=[
            pl.BlockSpec((1, tq, w), lambda bi, h, i: (bi, i, hq + h)),
            pl.BlockSpec((1, s, w), lambda bi, h, i: (bi, 0, hk + h)),
            pl.BlockSpec((1, s, w), lambda bi, h, i: (bi, 0, hv + h)),
            vec, vec, vec, vec,
            pl.BlockSpec((1, w), lambda bi, h, i: (0, 0)),
        ],
        out_specs=pl.BlockSpec((1, tq, w), lambda bi, h, i: (bi, i, h)),
        out_shape=jax.ShapeDtypeStruct((b, s, N_DIFF_HEADS * w), BF16),
        scratch_shapes=[pltpu.VMEM((tq, w), F32), pltpu.VMEM((tq, w), F32)],
        compiler_params=_cparams(("arbitrary", "arbitrary", "arbitrary")),
        name="diff_attn",
    )(proj3d, proj3d, proj3d, lq1, lk1, lq2, lk2, gain)


def _outproj_kernel(x_ref, sb_ref, df_ref, wo_ref, g_ref, wr_ref, br_ref, tri_ref,
                    x1_ref, h2_ref, ri_ref, rf_ref, cnt_ref, wo_bf16_ref, prev_ref):
    i = pl.program_id(0)
    tm = x_ref.shape[0]
    half = sb_ref.shape[1]

    @pl.when(i == 0)
    def _():
        wo_bf16_ref[...] = wo_ref[...].astype(BF16)
        prev_ref[...] = jnp.zeros_like(prev_ref)
        cnt_ref[...] = jnp.zeros_like(cnt_ref)

    def project(cols):
        x1_ref[:, cols] = (x_ref[:, cols] + _dot(sb_ref[...], wo_bf16_ref[:half, cols])
                           + _dot(df_ref[...], wo_bf16_ref[half:, cols]))

    d = x_ref.shape[1]
    project(slice(0, d // 2))

    x1 = prev_ref[...]
    ms = jnp.mean(x1 * x1, axis=-1, keepdims=True)
    h2 = x1 * lax.rsqrt(ms + NORM_EPS) * g_ref[...]
    h2_ref[...] = h2

    logits = _dot(h2.astype(BF16), wr_ref[...]) + br_ref[...]
    project(slice(d // 2, d))
    lane = lax.broadcasted_iota(jnp.int32, (tm, LANES), 1).astype(F32)
    ninf = -jnp.inf

    def first_argmax(vals):
        top = jnp.max(vals, axis=-1, keepdims=True)
        idx = jnp.min(jnp.where(vals == top, lane, float(LANES)), axis=-1, keepdims=True)
        return top, idx

    gl = jnp.where(lane < N_GROUPS, logits, ninf)
    gmax, gidx = first_argmax(gl)
    g_val = 1.0 / jnp.sum(jnp.exp(gl - gmax), axis=-1, keepdims=True)
    lo = N_GROUPS + EXPERTS_PER_GROUP * gidx
    el = jnp.where((lane >= lo) & (lane < lo + EXPERTS_PER_GROUP), logits, ninf)
    l1, i1 = first_argmax(el)
    l2, i2 = first_argmax(jnp.where(lane == i1, ninf, el))
    r = jnp.exp(l2 - l1)
    w1 = g_val / (1.0 + r)
    w2 = g_val * r / (1.0 + r)
    e1 = i1 - N_GROUPS
    e2 = i2 - N_GROUPS

    onehot = jnp.where(((lane == e1) | (lane == e2)) & (i > 0), 1.0, 0.0)
    before = _dot(tri_ref[...], onehot.astype(BF16)) + cnt_ref[0:1, :]
    rank1 = jnp.sum(jnp.where(lane == e1, before, 0.0), axis=-1, keepdims=True)
    rank2 = jnp.sum(jnp.where(lane == e2, before, 0.0), axis=-1, keepdims=True)
    cnt_ref[0:1, :] = cnt_ref[0:1, :] + jnp.sum(onehot, axis=0, keepdims=True)

    ri = jnp.where(lane == 0, e1, jnp.where(lane == 1, e2,
                   jnp.where(lane == 2, rank1, jnp.where(lane == 3, rank2, 0.0))))
    ri_ref[0] = jnp.transpose(ri)[:8].astype(jnp.int32)
    rf_ref[...] = jnp.where(lane == 0, w1, jnp.where(lane == 1, w2, 0.0))

    prev_ref[...] = x1_ref[...]


def _outproj_call(x2d, sb_out, d_out, w_out, gain, w_router, b_router, tri, *, tm):
    t, d = x2d.shape
    half = sb_out.shape[1]
    assert t % tm == 0
    n = t // tm
    row = lambda i: (jnp.minimum(i, n - 1), 0)
    routed = lambda i: (jnp.maximum(i - 1, 0), 0)
    fixed = lambda i: (0, 0)
    return pl.pallas_call(
        _outproj_kernel,
        grid=(n + 1,),
        in_specs=[
            pl.BlockSpec((tm, d), row),
            pl.BlockSpec((tm, half), row),
            pl.BlockSpec((tm, half), row),
            pl.BlockSpec((2 * half, d), fixed, pipeline_mode=pl.Buffered(1)),
            pl.BlockSpec((1, d), fixed),
            pl.BlockSpec((d, LANES), fixed),
            pl.BlockSpec((1, LANES), fixed),
            pl.BlockSpec((tm, tm), fixed),
        ],
        out_specs=[
            pl.BlockSpec((tm, d), row),
            pl.BlockSpec((tm, d), routed),
            pl.BlockSpec((1, 8, tm), lambda i: (jnp.maximum(i - 1, 0), 0, 0)),
            pl.BlockSpec((tm, LANES), routed),
            pl.BlockSpec((8, LANES), fixed),
        ],
        out_shape=[
            jax.ShapeDtypeStruct((t, d), F32),
            jax.ShapeDtypeStruct((t, d), F32),
            jax.ShapeDtypeStruct((t // tm, 8, tm), jnp.int32),
            jax.ShapeDtypeStruct((t, LANES), F32),
            jax.ShapeDtypeStruct((8, LANES), F32),
        ],
        scratch_shapes=[pltpu.VMEM((2 * half, d), BF16), pltpu.VMEM((tm, d), F32)],
        compiler_params=_cparams(("arbitrary",)),
        name="outproj_router",
    )(x2d, sb_out, d_out, w_out, gain, w_router, b_router, tri)


def _dispatch_kernel(pos_hbm, h_ref, xs_hbm, pos_smem, sem_idx, sem, *, td):
    i = pl.program_id(0)
    idx_copy = pltpu.make_async_copy(pos_hbm.at[i], pos_smem, sem_idx)
    idx_copy.start()
    idx_copy.wait()

    def row_copy(t, k):
        return pltpu.make_async_copy(h_ref.at[pl.ds(t, 1)],
                                     xs_hbm.at[pl.ds(pos_smem[0, k * td + t], 1)], sem)

    def issue(t, c):
        row_copy(t, 0).start()
        row_copy(t, 1).start()
        return c

    lax.fori_loop(0, td, issue, 0, unroll=DMA_UNROLL)

    def drain(t, c):
        row_copy(t, 0).wait()
        row_copy(t, 1).wait()
        return c

    lax.fori_loop(0, td, drain, 0, unroll=DMA_UNROLL)


def _dispatch_call(pos3, h2, n_rows):
    t, d = h2.shape
    td = pos3.shape[2] // 2
    assert pos3.shape[0] * td == t
    return pl.pallas_call(
        functools.partial(_dispatch_kernel, td=td),
        grid=(t // td,),
        in_specs=[pl.BlockSpec(memory_space=pl.ANY), pl.BlockSpec((td, d), lambda i: (i, 0))],
        out_specs=pl.BlockSpec(memory_space=pl.ANY),
        out_shape=jax.ShapeDtypeStruct((n_rows, d), h2.dtype),
        scratch_shapes=[pltpu.SMEM((1, 2 * td), jnp.int32), pltpu.SemaphoreType.DMA,
                        pltpu.SemaphoreType.DMA],
        compiler_params=_cparams(("arbitrary",)),
        name="dispatch",
    )(pos3, h2)


def _experts_kernel(wt_ref, we_ref, nw_ref, lo_ref, hi_ref, slot_ref, nxt_ref, xs_ref, wg_hbm, wu_hbm,
                    wd_hbm, ys_ref, wg_buf, wu_buf, wd_buf, sem):
    w = pl.program_id(0)
    tx = xs_ref.shape[0]
    tile = wt_ref[w]
    e = we_ref[w]
    slot = slot_ref[w]
    before = jnp.maximum(w - 1, 0)
    first = (w == 0) | (wt_ref[before] != tile)
    first_of_expert = (w == 0) | (we_ref[before] != e)

    def weight_copies(expert, s):
        return [pltpu.make_async_copy(hbm.at[expert], buf.at[s], sem.at[s, i])
                for i, (hbm, buf) in enumerate(((wg_hbm, wg_buf), (wu_hbm, wu_buf), (wd_hbm, wd_buf)))]

    @pl.when(w == 0)
    def _():
        for copy in weight_copies(e, slot):
            copy.start()

    @pl.when(first_of_expert & (w < nw_ref[0]))
    def _():
        for copy in weight_copies(e, slot):
            copy.wait()

        @pl.when(nxt_ref[w] >= 0)
        def _():
            for copy in weight_copies(nxt_ref[w], 1 - slot):
                copy.start()

    @pl.when(w < nw_ref[0])
    def _():
        rows = tile * tx + lax.broadcasted_iota(jnp.int32, (tx, 1), 0)
        member = (rows >= lo_ref[e]) & (rows < hi_ref[e])
        x = xs_ref[...].astype(BF16)
        gate = _dot(x, wg_buf[slot].astype(BF16))
        up = _dot(x, wu_buf[slot].astype(BF16))
        hid = gate * (1.0 / (1.0 + jnp.exp(-gate))) * up
        y = _dot(jnp.where(member, hid, 0.0).astype(BF16), wd_buf[slot].astype(BF16))

        @pl.when(first)
        def _():
            ys_ref[...] = y

        @pl.when(jnp.logical_not(first))
        def _():
            ys_ref[...] += y


def _experts_call(work_tile, work_expert, n_work, seg_lo, seg_hi, work_slot, work_next, xs, w_gate,
                  w_up, w_down, *, tx):
    p, d = xs.shape
    de = w_gate.shape[2]
    assert p % tx == 0
    n_items = work_tile.shape[0]
    tile = lambda w, wt, *_: (wt[w], 0)
    any_space = pl.BlockSpec(memory_space=pl.ANY)
    return pl.pallas_call(
        _experts_kernel,
        grid_spec=pltpu.PrefetchScalarGridSpec(
            num_scalar_prefetch=7,
            grid=(n_items,),
            in_specs=[pl.BlockSpec((tx, d), tile), any_space, any_space, any_space],
            out_specs=pl.BlockSpec((tx, d), tile),
            scratch_shapes=[pltpu.VMEM((2, d, de), F32), pltpu.VMEM((2, d, de), F32),
                            pltpu.VMEM((2, de, d), F32), pltpu.SemaphoreType.DMA((2, 3))],
        ),
        out_shape=jax.ShapeDtypeStruct((p, d), F32),
        compiler_params=_cparams(("arbitrary",)),
        name="experts",
    )(work_tile, work_expert, n_work, seg_lo, seg_hi, work_slot, work_next, xs, w_gate, w_up, w_down)


def _combine_kernel(pos_hbm, x1_ref, rf_ref, g_ref, ys_hbm, o_ref, pos_smem, y0_ref, y1_ref,
                    sem_idx, sem, *, tc):
    i = pl.program_id(0)
    idx_copy = pltpu.make_async_copy(pos_hbm.at[i], pos_smem, sem_idx)
    idx_copy.start()
    idx_copy.wait()

    def row_copy(t, k):
        dst = y0_ref if k == 0 else y1_ref
        return pltpu.make_async_copy(ys_hbm.at[pl.ds(pos_smem[0, k * tc + t], 1)],
                                     dst.at[pl.ds(t, 1)], sem)

    def issue(t, c):
        row_copy(t, 0).start()
        row_copy(t, 1).start()
        return c

    lax.fori_loop(0, tc, issue, 0, unroll=DMA_UNROLL)

    def drain(t, c):
        row_copy(t, 0).wait()
        row_copy(t, 1).wait()
        return c

    lax.fori_loop(0, tc, drain, 0, unroll=DMA_UNROLL)

    x = x1_ref[...] + rf_ref[:, 0:1] * y0_ref[...] + rf_ref[:, 1:2] * y1_ref[...]
    ms = jnp.mean(x * x, axis=-1, keepdims=True)
    o_ref[...] = x * lax.rsqrt(ms + NORM_EPS) * g_ref[...]


def _combine_call(pos3, x1, rf, gain, ys):
    t, d = x1.shape
    tc = pos3.shape[2] // 2
    assert pos3.shape[0] * tc == t
    row = lambda i: (i, 0)
    return pl.pallas_call(
        functools.partial(_combine_kernel, tc=tc),
        grid=(t // tc,),
        in_specs=[
            pl.BlockSpec(memory_space=pl.ANY),
            pl.BlockSpec((tc, d), row),
            pl.BlockSpec((tc, LANES), row),
            pl.BlockSpec((1, d), lambda i: (0, 0)),
            pl.BlockSpec(memory_space=pl.ANY),
        ],
        out_specs=pl.BlockSpec((tc, d), row),
        out_shape=jax.ShapeDtypeStruct((t, d), F32),
        scratch_shapes=[pltpu.SMEM((1, 2 * tc), jnp.int32), pltpu.VMEM((tc, d), F32),
                        pltpu.VMEM((tc, d), F32), pltpu.SemaphoreType.DMA,
                        pltpu.SemaphoreType.DMA],
        compiler_params=_cparams(("arbitrary",)),
        name="combine",
    )(pos3, x1, rf, gain, ys)


def _pick(n, pref):
    while n % pref:
        pref //= 2
    return pref


def _layer(x2d, b, s, layer, attn_norm_gain, w_in, sb_norm_gain, lq1, lk1, lq2, lk2, subln_gain,
           w_out, ffn_norm_gain, w_gr, b_gr, w_er, b_er, w_gate, w_up, w_down):
    t, d = x2d.shape
    tb = _pick(s, 256)
    tm_proj = _pick(s, 1024)
    tm_out = _pick(t, 256)
    tx = 256

    half = HEAD_DIM // 2
    inv_freq = 1.0 / (ROPE_THETA ** (np.arange(half, dtype=np.float64) / half))
    ang = np.arange(s, dtype=np.float64)[:, None] * inv_freq[None, :]
    cos = jnp.asarray(np.concatenate([np.cos(ang), np.cos(ang)], axis=-1), F32)
    sin = jnp.asarray(np.concatenate([-np.sin(ang), np.sin(ang)], axis=-1), F32)

    proj = _proj_call(x2d, attn_norm_gain.reshape(1, d), w_in, cos, sin, s,
                      tm=tm_proj, tn=512)
    proj3d = proj.reshape(b, s, proj.shape[1])

    r = jnp.arange(tb)
    tri_suffix = (r[:, None] >= r[None, :]).astype(BF16)
    sb_out = _sb_call(proj3d, tri_suffix, sb_norm_gain.reshape(1, HEAD_DIM), tb=tb,
                      nsub=_pick(s // tb, 16))

    lam_init = 0.8 - 0.6 * math.exp(-0.3 * layer)
    d_out = _diff_call(proj3d, lq1.reshape(1, -1), lk1.reshape(1, -1), lq2.reshape(1, -1),
                       lk2.reshape(1, -1), subln_gain.reshape(1, -1), lam_init, tq=2 * tb)

    w_router = jnp.zeros((d, LANES), F32)
    w_router = w_router.at[:, :N_GROUPS].set(w_gr).at[:, N_GROUPS:N_GROUPS + N_EXPERTS].set(w_er)
    b_router = jnp.zeros((1, LANES), F32)
    b_router = b_router.at[0, :N_GROUPS].set(b_gr).at[0, N_GROUPS:N_GROUPS + N_EXPERTS].set(b_er)
    rr = jnp.arange(tm_out)
    tri_before = (rr[None, :] < rr[:, None]).astype(BF16)
    x1, h2, ri, rf, cnt = _outproj_call(
        x2d, sb_out.reshape(t, -1), d_out.reshape(t, -1), w_out,
        ffn_norm_gain.reshape(1, d), w_router.astype(BF16), b_router, tri_before, tm=tm_out)

    counts = cnt[0, :N_EXPERTS].astype(jnp.int32)
    seg_hi = jnp.cumsum(counts)
    seg_lo = seg_hi - counts
    experts = jnp.arange(N_EXPERTS, dtype=jnp.int32)
    ri = ri.transpose(1, 0, 2).reshape(8, t)

    def sorted_position(slot):
        hit = ri[slot][None, :] == experts[:, None]
        return jnp.sum(jnp.where(hit, seg_lo[:, None], 0), axis=0) + ri[2 + slot]

    def tiled_positions(tile):
        return jnp.stack([sorted_position(0).reshape(-1, tile), sorted_position(1).reshape(-1, tile)],
                         axis=1).reshape(-1, 1, 2 * tile)

    first_tile = seg_lo // tx
    items = jnp.where(counts > 0, (seg_hi - 1) // tx - first_tile + 1, 0)
    item_hi = jnp.cumsum(items)
    n_work = item_hi[-1]
    w = jnp.minimum(jnp.arange(2 * t // tx + N_EXPERTS - 1, dtype=jnp.int32), n_work - 1)
    work_expert = jnp.sum((item_hi[None, :] <= w[:, None]).astype(jnp.int32), axis=1)
    work_tile = (first_tile[work_expert] + w - (item_hi - items)[work_expert]).astype(jnp.int32)

    xs = _dispatch_call(tiled_positions(_pick(t, 2048)), h2, 2 * t)
    used = counts > 0
    work_slot = ((jnp.cumsum(used.astype(jnp.int32)) - 1) % 2)[work_expert].astype(jnp.int32)
    later_used = (experts[None, :] > experts[:, None]) & used[None, :]
    next_used = jnp.min(jnp.where(later_used, experts[None, :], N_EXPERTS), axis=1)
    next_used = jnp.where(next_used < N_EXPERTS, next_used, -1)
    run_start = jnp.concatenate([jnp.ones((1,), bool), work_expert[1:] != work_expert[:-1]])
    work_next = jnp.where(run_start, next_used[work_expert], -1).astype(jnp.int32)
    ys = _experts_call(work_tile, work_expert, n_work.reshape(1).astype(jnp.int32), seg_lo, seg_hi,
                       work_slot, work_next, xs, w_gate, w_up, w_down, tx=tx)
    return x1, tiled_positions(_pick(t, 512)), rf, ys


def kernel(x, attn_norm_gain, w_in, sb_norm_gain, diff_lambda_q1, diff_lambda_k1, diff_lambda_q2,
           diff_lambda_k2, diff_subln_gain, w_out, ffn_norm_gain, w_group_router, b_group_router,
           w_expert_router, b_expert_router, w_gate, w_up, w_down, final_norm_gain):
    b, s, d = x.shape
    assert w_in.shape[0] == 1, "the combine stage fuses the final norm: single-layer stacks only"
    layer = 0
    x2d = x.reshape(b * s, d)
    x1, pos, rf, ys = _layer(
        x2d, b, s, layer, attn_norm_gain[layer], w_in[layer], sb_norm_gain[layer],
        diff_lambda_q1[layer], diff_lambda_k1[layer], diff_lambda_q2[layer],
        diff_lambda_k2[layer], diff_subln_gain[layer], w_out[layer], ffn_norm_gain[layer],
        w_group_router[layer], b_group_router[layer], w_expert_router[layer],
        b_expert_router[layer], w_gate[layer], w_up[layer], w_down[layer])
    out = _combine_call(pos, x1, rf, final_norm_gain.reshape(1, d), ys)
    return out.reshape(b, s, d)
```

```python
import functools
import math

import jax
import jax.numpy as jnp
import numpy as np
from jax import lax
from jax.experimental import pallas as pl
from jax.experimental.pallas import tpu as pltpu

F32 = jnp.float32
BF16 = jnp.bfloat16

HEAD_DIM = 128
N_SB_HEADS = 8
N_DIFF_HEADS = 4
SECTION = 1024
CHUNK = 64
ROPE_THETA = 10000.0
N_GROUPS = 4
EXPERTS_PER_GROUP = 8
N_EXPERTS = N_GROUPS * EXPERTS_PER_GROUP
NORM_EPS = 1e-6
NEG_INF = -1e30
LANES = 128
UNDERFLOW_LOG2 = -200.0
DMA_UNROLL = 8
SCALE_LOG2E = math.log2(math.e) / math.sqrt(HEAD_DIM)

VMEM_LIMIT = 56 * 1024 * 1024


def _cparams(sem):
    return pltpu.CompilerParams(dimension_semantics=sem, vmem_limit_bytes=VMEM_LIMIT)


def _dot(a, b):
    return jnp.dot(a, b, preferred_element_type=F32)


def _dot_nt(a, b):
    return lax.dot_general(a, b, (((1,), (1,)), ((), ())), preferred_element_type=F32)


def _proj_kernel(x_ref, g_ref, w_ref, cos_ref, sin_ref, o_ref, h_ref, *, tn, rows):
    j = pl.program_id(1)
    tm = x_ref.shape[0]
    sec = j // (SECTION // tn)

    def normalize(rs):
        x = x_ref[rs, :]
        ms = jnp.mean(x * x, axis=-1, keepdims=True)
        h_ref[rs, :] = (x * lax.rsqrt(ms + NORM_EPS) * g_ref[...]).astype(BF16)

    def chunks(epilogue, first_visit=False):
        for r in range(0, tm, rows):
            rs = slice(r, r + rows)
            if first_visit:
                normalize(rs)
            epilogue(rs, _dot(h_ref[rs, :], w_ref[...].astype(BF16)))

    def rotary(scale):
        def epilogue(rs, acc):
            for c in range(tn // HEAD_DIM):
                cs = slice(c * HEAD_DIM, (c + 1) * HEAD_DIM)
                a = acc[:, cs]
                rot = a * cos_ref[rs, :] + pltpu.roll(a, HEAD_DIM // 2, 1) * sin_ref[rs, :]
                if scale != 1.0:
                    rot = rot * scale
                o_ref[rs, cs] = rot.astype(BF16)
        chunks(epilogue)

    def scaled(scale, first_visit=False):
        def epilogue(rs, acc):
            o_ref[rs, :] = (acc if scale == 1.0 else acc * scale).astype(BF16)
        chunks(epilogue, first_visit)

    @pl.when(j == 0)
    def _():
        scaled(-SCALE_LOG2E, first_visit=True)

    @pl.when((sec == 0) & (j > 0))
    def _():
        scaled(-SCALE_LOG2E)

    @pl.when((sec == 1) | (sec == 2) | (sec == 5))
    def _():
        scaled(1.0)

    @pl.when(sec == 3)
    def _():
        rotary(SCALE_LOG2E)

    @pl.when(sec == 4)
    def _():
        rotary(1.0)


def _proj_call(x2d, gain, w_in, cos, sin, seq, *, tm, tn):
    t, d = x2d.shape
    n = w_in.shape[1]
    assert t % tm == 0 and seq % tm == 0 and n % tn == 0 and SECTION % tn == 0
    rows = min(tm, 256)
    nseq = seq // tm
    return pl.pallas_call(
        functools.partial(_proj_kernel, tn=tn, rows=rows),
        grid=(t // tm, n // tn),
        in_specs=[
            pl.BlockSpec((tm, d), lambda i, j: (i, 0)),
            pl.BlockSpec((1, d), lambda i, j: (0, 0)),
            pl.BlockSpec((d, tn), lambda i, j: (0, j)),
            pl.BlockSpec((tm, HEAD_DIM), lambda i, j: (i % nseq, 0)),
            pl.BlockSpec((tm, HEAD_DIM), lambda i, j: (i % nseq, 0)),
        ],
        out_specs=pl.BlockSpec((tm, tn), lambda i, j: (i, j)),
        out_shape=jax.ShapeDtypeStruct((t, n), BF16),
        scratch_shapes=[pltpu.VMEM((tm, d), BF16)],
        compiler_params=_cparams(("arbitrary", "arbitrary")),
        name="proj",
    )(x2d, gain, w_in, cos, sin)


def _sb_kernel(q_ref, k_ref, v_ref, tri_ref, g_ref, o_ref, acc_ref, *, tb, nsub):
    qi = pl.program_id(2)
    tq = nsub * tb
    row = lax.broadcasted_iota(jnp.int32, (tb, tb), 0)
    col = lax.broadcasted_iota(jnp.int32, (tb, tb), 1)
    strict = col < row

    def scores(rows, kj):
        start = pl.multiple_of(kj * tb, tb)
        return _dot_nt(q_ref[0, rows, :], k_ref[0, pl.ds(start, tb), :])

    def keep_logs(n, mask):
        lk = jnp.minimum(n, 0.0) - jnp.log2(1.0 + jnp.exp2(-jnp.abs(n)))
        if mask is not None:
            lk = jnp.where(mask, lk, 0.0)
        return lk.astype(BF16)

    def accumulate(rows, kj, n, lk, carry, mask):
        start = pl.multiple_of(kj * tb, tb)
        cum = _dot(lk, tri_ref[...])
        a = jnp.exp2(cum + carry - n)
        if mask is not None:
            a = jnp.where(mask, a, 0.0)
        acc_ref[rows, :] += _dot(a.astype(BF16), v_ref[0, pl.ds(start, tb), :])
        return carry + cum[:, 0:1]

    def several(rows, kjs, carry, mask=None):
        staged = []
        for kj in kjs:
            n = scores(rows, kj)
            staged.append((kj, n, keep_logs(n, mask)))
        for kj, n, lk in staged:
            carry = accumulate(rows, kj, n, lk, carry, mask)
        return carry

    acc_ref[...] = jnp.zeros_like(acc_ref)
    base = qi * nsub
    blocks = [slice(r * tb, (r + 1) * tb) for r in range(nsub)]
    chains = []
    for r in range(nsub):
        chains.append((r, base + r, strict))
        chains.append((r, jnp.maximum(base + r - 1, 0),
                       jnp.broadcast_to(base > 0, (tb, tb)) if r == 0 else None))
    staged = []
    for r, kj, mask in chains:
        n = scores(blocks[r], kj)
        staged.append((n, keep_logs(n, mask)))
    carry = [jnp.zeros((tb, 1), F32)] * nsub
    for (r, kj, mask), (n, lk) in zip(chains, staged):
        carry[r] = accumulate(blocks[r], kj, n, lk, carry[r], mask)

    def walk_back(rows, first, carry):
        def still_live(kj, c):
            return (kj >= 0) & (jnp.max(c) > UNDERFLOW_LOG2)

        def one_block(state):
            kj, c, _ = state
            c = several(rows, [kj], c)
            return kj - 1, c, still_live(kj - 1, c)

        lax.while_loop(lambda state: state[2], one_block, (first, carry, still_live(first, carry)))

    for r in range(nsub):
        walk_back(blocks[r], base + r - 2, carry[r])

    o = acc_ref[...]
    ms = jnp.mean(o * o, axis=-1, keepdims=True)
    o_ref[0] = (o * lax.rsqrt(ms + NORM_EPS) * g_ref[...]).astype(BF16)


def _sb_call(proj3d, tri, gain, *, tb, nsub):
    b, s, _ = proj3d.shape
    tq = nsub * tb
    assert s % tq == 0
    hq, hk, hv = 0, SECTION // HEAD_DIM, 2 * SECTION // HEAD_DIM
    return pl.pallas_call(
        functools.partial(_sb_kernel, tb=tb, nsub=nsub),
        grid=(b, N_SB_HEADS, s // tq),
        in_specs=[
            pl.BlockSpec((1, tq, HEAD_DIM), lambda bi, h, i: (bi, i, hq + h)),
            pl.BlockSpec((1, s, HEAD_DIM), lambda bi, h, i: (bi, 0, hk + h)),
            pl.BlockSpec((1, s, HEAD_DIM), lambda bi, h, i: (bi, 0, hv + h)),
            pl.BlockSpec((tb, tb), lambda bi, h, i: (0, 0)),
            pl.BlockSpec((1, HEAD_DIM), lambda bi, h, i: (0, 0)),
        ],
        out_specs=pl.BlockSpec((1, tq, HEAD_DIM), lambda bi, h, i: (bi, i, h)),
        out_shape=jax.ShapeDtypeStruct((b, s, N_SB_HEADS * HEAD_DIM), BF16),
        scratch_shapes=[pltpu.VMEM((tq, HEAD_DIM), F32)],
        compiler_params=_cparams(("arbitrary", "arbitrary", "arbitrary")),
        name="sb_attn",
    )(proj3d, proj3d, proj3d, tri, gain)


def _diff_kernel(q_ref, k_ref, v_ref, lq1_ref, lk1_ref, lq2_ref, lk2_ref, g_ref, o_ref,
                 acc1_ref, acc2_ref, *, tq, lam_init):
    qi = pl.program_id(2)
    d = HEAD_DIM
    q1 = q_ref[0, :, :d]
    q2 = q_ref[0, :, d:]
    row = lax.broadcasted_iota(jnp.int32, (tq, tq), 0)
    col = lax.broadcasted_iota(jnp.int32, (tq, tq), 1)
    visible = (col // CHUNK) <= (row // CHUNK)

    def probabilities(s, m, l):
        m_new = jnp.maximum(m, jnp.max(s, axis=-1, keepdims=True))
        alpha = jnp.exp2(m - m_new)
        p = jnp.exp2(s - m_new)
        return p.astype(BF16), alpha, m_new, alpha * l + jnp.sum(p, axis=-1, keepdims=True)

    def block(start, width, carry, masked):
        m1, l1, m2, l2 = carry
        k = k_ref[0, pl.ds(start, width), :]
        v = v_ref[0, pl.ds(start, width), :]
        s1 = _dot_nt(q1, k[:, :d])
        s2 = _dot_nt(q2, k[:, d:])
        if masked:
            s1 = jnp.where(visible, s1, NEG_INF)
            s2 = jnp.where(visible, s2, NEG_INF)
        p1, alpha1, m1, l1 = probabilities(s1, m1, l1)
        p2, alpha2, m2, l2 = probabilities(s2, m2, l2)
        pv = _dot(jnp.concatenate([p1, p2], axis=0), v)
        acc1_ref[...] = alpha1 * acc1_ref[...] + pv[:tq]
        acc2_ref[...] = alpha2 * acc2_ref[...] + pv[tq:]
        return m1, l1, m2, l2

    acc1_ref[...] = jnp.zeros_like(acc1_ref)
    acc2_ref[...] = jnp.zeros_like(acc2_ref)
    neg = jnp.full((tq, 1), NEG_INF, F32)
    zero = jnp.zeros((tq, 1), F32)
    carry = block(pl.multiple_of(qi * tq, tq), tq, (neg, zero, neg, zero), True)
    odd = qi % 2
    carry = lax.cond(odd == 1, lambda c: block(pl.multiple_of((qi - 1) * tq, tq), tq, c, False),
                     lambda c: c, carry)
    _, l1, _, l2 = lax.fori_loop(
        0, qi // 2, lambda i, c: block(pl.multiple_of(2 * i * tq, 2 * tq), 2 * tq, c, False), carry)

    lam = (jnp.exp(jnp.sum(lq1_ref[...] * lk1_ref[...], axis=-1, keepdims=True))
           - jnp.exp(jnp.sum(lq2_ref[...] * lk2_ref[...], axis=-1, keepdims=True)) + lam_init)
    o = acc1_ref[...] / l1 - lam * (acc2_ref[...] / l2)
    ms = jnp.mean(o * o, axis=-1, keepdims=True)
    o_ref[0] = (o * lax.rsqrt(ms + NORM_EPS) * g_ref[...] * (1.0 - lam_init)).astype(BF16)


def _diff_call(proj3d, lq1, lk1, lq2, lk2, gain, lam_init, *, tq):
    b, s, _ = proj3d.shape
    assert s % tq == 0 and tq % CHUNK == 0
    w = 2 * HEAD_DIM
    hq, hk, hv = 3 * SECTION // w, 4 * SECTION // w, 5 * SECTION // w
    vec = pl.BlockSpec((1, HEAD_DIM), lambda bi, h, i: (0, 0))
    return pl.pallas_call(
        functools.partial(_diff_kernel, tq=tq, lam_init=lam_init),
        grid=(b, N_DIFF_HEADS, s // tq),
        in_specs=[
            pl.BlockSpec((1, tq, w), lambda bi, h, i: (bi, i, hq + h)),
            pl.BlockSpec((1, s, w), lambda bi, h, i: (bi, 0, hk + h)),
            pl.BlockSpec((1, s, w), lambda bi, h, i: (bi, 0, hv + h)),
            vec, vec, vec, vec,
            pl.BlockSpec((1, w), lambda bi, h, i: (0, 0)),
        ],
        out_specs=pl.BlockSpec((1, tq, w), lambda bi, h, i: (bi, i, h)),
        out_shape=jax.ShapeDtypeStruct((b, s, N_DIFF_HEADS * w), BF16),
        scratch_shapes=[pltpu.VMEM((tq, w), F32), pltpu.VMEM((tq, w), F32)],
        compiler_params=_cparams(("arbitrary", "arbitrary", "arbitrary")),
        name="diff_attn",
    )(proj3d, proj3d, proj3d, lq1, lk1, lq2, lk2, gain)


def _outproj_kernel(x_ref, sb_ref, df_ref, wo_ref, g_ref, wr_ref, br_ref, tri_ref,
                    x1_ref, h2_ref, ri_ref, rf_ref, cnt_ref, wo_bf16_ref, prev_ref):
    i = pl.program_id(0)
    tm = x_ref.shape[0]
    half = sb_ref.shape[1]

    @pl.when(i == 0)
    def _():
        wo_bf16_ref[...] = wo_ref[...].astype(BF16)
        prev_ref[...] = jnp.zeros_like(prev_ref)
        cnt_ref[...] = jnp.zeros_like(cnt_ref)

    def project(cols):
        x1_ref[:, cols] = (x_ref[:, cols] + _dot(sb_ref[...], wo_bf16_ref[:half, cols])
                           + _dot(df_ref[...], wo_bf16_ref[half:, cols]))

    d = x_ref.shape[1]
    project(slice(0, d // 2))

    x1 = prev_ref[...]
    ms = jnp.mean(x1 * x1, axis=-1, keepdims=True)
    h2 = x1 * lax.rsqrt(ms + NORM_EPS) * g_ref[...]
    h2_ref[...] = h2

    logits = _dot(h2.astype(BF16), wr_ref[...]) + br_ref[...]
    project(slice(d // 2, d))
    lane = lax.broadcasted_iota(jnp.int32, (tm, LANES), 1).astype(F32)
    ninf = -jnp.inf

    def first_argmax(vals):
        top = jnp.max(vals, axis=-1, keepdims=True)
        idx = jnp.min(jnp.where(vals == top, lane, float(LANES)), axis=-1, keepdims=True)
        return top, idx

    gl = jnp.where(lane < N_GROUPS, logits, ninf)
    gmax, gidx = first_argmax(gl)
    g_val = 1.0 / jnp.sum(jnp.exp(gl - gmax), axis=-1, keepdims=True)
    lo = N_GROUPS + EXPERTS_PER_GROUP * gidx
    el = jnp.where((lane >= lo) & (lane < lo + EXPERTS_PER_GROUP), logits, ninf)
    l1, i1 = first_argmax(el)
    l2, i2 = first_argmax(jnp.where(lane == i1, ninf, el))
    r = jnp.exp(l2 - l1)
    w1 = g_val / (1.0 + r)
    w2 = g_val * r / (1.0 + r)
    e1 = i1 - N_GROUPS
    e2 = i2 - N_GROUPS

    onehot = jnp.where(((lane == e1) | (lane == e2)) & (i > 0), 1.0, 0.0)
    before = _dot(tri_ref[...], onehot.astype(BF16)) + cnt_ref[0:1, :]
    rank1 = jnp.sum(jnp.where(lane == e1, before, 0.0), axis=-1, keepdims=True)
    rank2 = jnp.sum(jnp.where(lane == e2, before, 0.0), axis=-1, keepdims=True)
    cnt_ref[0:1, :] = cnt_ref[0:1, :] + jnp.sum(onehot, axis=0, keepdims=True)

    ri = jnp.where(lane == 0, e1, jnp.where(lane == 1, e2,
                   jnp.where(lane == 2, rank1, jnp.where(lane == 3, rank2, 0.0))))
    ri_ref[0] = jnp.transpose(ri)[:8].astype(jnp.int32)
    rf_ref[...] = jnp.where(lane == 0, w1, jnp.where(lane == 1, w2, 0.0))

    prev_ref[...] = x1_ref[...]


def _outproj_call(x2d, sb_out, d_out, w_out, gain, w_router, b_router, tri, *, tm):
    t, d = x2d.shape
    half = sb_out.shape[1]
    assert t % tm == 0
    n = t // tm
    row = lambda i: (jnp.minimum(i, n - 1), 0)
    routed = lambda i: (jnp.maximum(i - 1, 0), 0)
    fixed = lambda i: (0, 0)
    return pl.pallas_call(
        _outproj_kernel,
        grid=(n + 1,),
        in_specs=[
            pl.BlockSpec((tm, d), row),
            pl.BlockSpec((tm, half), row),
            pl.BlockSpec((tm, half), row),
            pl.BlockSpec((2 * half, d), fixed, pipeline_mode=pl.Buffered(1)),
            pl.BlockSpec((1, d), fixed),
            pl.BlockSpec((d, LANES), fixed),
            pl.BlockSpec((1, LANES), fixed),
            pl.BlockSpec((tm, tm), fixed),
        ],
        out_specs=[
            pl.BlockSpec((tm, d), row),
            pl.BlockSpec((tm, d), routed),
            pl.BlockSpec((1, 8, tm), lambda i: (jnp.maximum(i - 1, 0), 0, 0)),
            pl.BlockSpec((tm, LANES), routed),
            pl.BlockSpec((8, LANES), fixed),
        ],
        out_shape=[
            jax.ShapeDtypeStruct((t, d), F32),
            jax.ShapeDtypeStruct((t, d), F32),
            jax.ShapeDtypeStruct((t // tm, 8, tm), jnp.int32),
            jax.ShapeDtypeStruct((t, LANES), F32),
            jax.ShapeDtypeStruct((8, LANES), F32),
        ],
        scratch_shapes=[pltpu.VMEM((2 * half, d), BF16), pltpu.VMEM((tm, d), F32)],
        compiler_params=_cparams(("arbitrary",)),
        name="outproj_router",
    )(x2d, sb_out, d_out, w_out, gain, w_router, b_router, tri)


def _dispatch_kernel(pos_hbm, h_ref, xs_hbm, pos_smem, sem_idx, sem, *, td):
    i = pl.program_id(0)
    idx_copy = pltpu.make_async_copy(pos_hbm.at[i], pos_smem, sem_idx)
    idx_copy.start()
    idx_copy.wait()

    def row_copy(t, k):
        return pltpu.make_async_copy(h_ref.at[pl.ds(t, 1)],
                                     xs_hbm.at[pl.ds(pos_smem[0, k * td + t], 1)], sem)

    def issue(t, c):
        row_copy(t, 0).start(priority=0)
        row_copy(t, 1).start(priority=1)
        return c

    lax.fori_loop(0, td, issue, 0, unroll=DMA_UNROLL)

    def drain(t, c):
        row_copy(t, 0).wait()
        row_copy(t, 1).wait()
        return c

    lax.fori_loop(0, td, drain, 0, unroll=DMA_UNROLL)


def _dispatch_call(pos3, h2, n_rows):
    t, d = h2.shape
    td = pos3.shape[2] // 2
    assert pos3.shape[0] * td == t
    return pl.pallas_call(
        functools.partial(_dispatch_kernel, td=td),
        grid=(t // td,),
        in_specs=[pl.BlockSpec(memory_space=pl.ANY), pl.BlockSpec((td, d), lambda i: (i, 0))],
        out_specs=pl.BlockSpec(memory_space=pl.ANY),
        out_shape=jax.ShapeDtypeStruct((n_rows, d), h2.dtype),
        scratch_shapes=[pltpu.SMEM((1, 2 * td), jnp.int32), pltpu.SemaphoreType.DMA,
                        pltpu.SemaphoreType.DMA],
        compiler_params=_cparams(("arbitrary",)),
        name="dispatch",
    )(pos3, h2)


def _experts_kernel(wt_ref, we_ref, nw_ref, lo_ref, hi_ref, slot_ref, nxt_ref, xs_ref, wg_hbm, wu_hbm,
                    wd_hbm, ys_ref, wg_buf, wu_buf, wd_buf, sem):
    w = pl.program_id(0)
    tx = xs_ref.shape[0]
    tile = wt_ref[w]
    e = we_ref[w]
    slot = slot_ref[w]
    before = jnp.maximum(w - 1, 0)
    first = (w == 0) | (wt_ref[before] != tile)
    first_of_expert = (w == 0) | (we_ref[before] != e)

    def weight_copies(expert, s):
        return [pltpu.make_async_copy(hbm.at[expert], buf.at[s], sem.at[s, i])
                for i, (hbm, buf) in enumerate(((wg_hbm, wg_buf), (wu_hbm, wu_buf), (wd_hbm, wd_buf)))]

    @pl.when(w == 0)
    def _():
        for copy in weight_copies(e, slot):
            copy.start()

    @pl.when(first_of_expert & (w < nw_ref[0]))
    def _():
        for copy in weight_copies(e, slot):
            copy.wait()

        @pl.when(nxt_ref[w] >= 0)
        def _():
            for copy in weight_copies(nxt_ref[w], 1 - slot):
                copy.start()

    @pl.when(w < nw_ref[0])
    def _():
        rows = tile * tx + lax.broadcasted_iota(jnp.int32, (tx, 1), 0)
        member = (rows >= lo_ref[e]) & (rows < hi_ref[e])
        x = xs_ref[...].astype(BF16)
        gate = _dot(x, wg_buf[slot].astype(BF16))
        up = _dot(x, wu_buf[slot].astype(BF16))
        hid = gate * (1.0 / (1.0 + jnp.exp(-gate))) * up
        y = _dot(jnp.where(member, hid, 0.0).astype(BF16), wd_buf[slot].astype(BF16))

        @pl.when(first)
        def _():
            ys_ref[...] = y

        @pl.when(jnp.logical_not(first))
        def _():
            ys_ref[...] += y


def _experts_call(work_tile, work_expert, n_work, seg_lo, seg_hi, work_slot, work_next, xs, w_gate,
                  w_up, w_down, *, tx):
    p, d = xs.shape
    de = w_gate.shape[2]
    assert p % tx == 0
    n_items = work_tile.shape[0]
    tile = lambda w, wt, *_: (wt[w], 0)
    any_space = pl.BlockSpec(memory_space=pl.ANY)
    return pl.pallas_call(
        _experts_kernel,
        grid_spec=pltpu.PrefetchScalarGridSpec(
            num_scalar_prefetch=7,
            grid=(n_items,),
            in_specs=[pl.BlockSpec((tx, d), tile), any_space, any_space, any_space],
            out_specs=pl.BlockSpec((tx, d), tile),
            scratch_shapes=[pltpu.VMEM((2, d, de), F32), pltpu.VMEM((2, d, de), F32),
                            pltpu.VMEM((2, de, d), F32), pltpu.SemaphoreType.DMA((2, 3))],
        ),
        out_shape=jax.ShapeDtypeStruct((p, d), F32),
        compiler_params=_cparams(("arbitrary",)),
        name="experts",
    )(work_tile, work_expert, n_work, seg_lo, seg_hi, work_slot, work_next, xs, w_gate, w_up, w_down)


def _combine_kernel(pos_hbm, x1_ref, rf_ref, g_ref, ys_hbm, o_ref, pos_smem, y0_ref, y1_ref,
                    sem_idx, sem, *, tc):
    i = pl.program_id(0)
    idx_copy = pltpu.make_async_copy(pos_hbm.at[i], pos_smem, sem_idx)
    idx_copy.start()
    idx_copy.wait()

    def row_copy(t, k):
        dst = y0_ref if k == 0 else y1_ref
        return pltpu.make_async_copy(ys_hbm.at[pl.ds(pos_smem[0, k * tc + t], 1)],
                                     dst.at[pl.ds(t, 1)], sem)

    def issue(t, c):
        row_copy(t, 0).start(priority=0)
        row_copy(t, 1).start(priority=1)
        return c

    lax.fori_loop(0, tc, issue, 0, unroll=DMA_UNROLL)

    def drain(t, c):
        row_copy(t, 0).wait()
        row_copy(t, 1).wait()
        return c

    lax.fori_loop(0, tc, drain, 0, unroll=DMA_UNROLL)

    x = x1_ref[...] + rf_ref[:, 0:1] * y0_ref[...] + rf_ref[:, 1:2] * y1_ref[...]
    ms = jnp.mean(x * x, axis=-1, keepdims=True)
    o_ref[...] = x * lax.rsqrt(ms + NORM_EPS) * g_ref[...]


def _combine_call(pos3, x1, rf, gain, ys):
    t, d = x1.shape
    tc = pos3.shape[2] // 2
    assert pos3.shape[0] * tc == t
    row = lambda i: (i, 0)
    return pl.pallas_call(
        functools.partial(_combine_kernel, tc=tc),
        grid=(t // tc,),
        in_specs=[
            pl.BlockSpec(memory_space=pl.ANY),
            pl.BlockSpec((tc, d), row),
            pl.BlockSpec((tc, LANES), row),
            pl.BlockSpec((1, d), lambda i: (0, 0)),
            pl.BlockSpec(memory_space=pl.ANY),
        ],
        out_specs=pl.BlockSpec((tc, d), row),
        out_shape=jax.ShapeDtypeStruct((t, d), F32),
        scratch_shapes=[pltpu.SMEM((1, 2 * tc), jnp.int32), pltpu.VMEM((tc, d), F32),
                        pltpu.VMEM((tc, d), F32), pltpu.SemaphoreType.DMA,
                        pltpu.SemaphoreType.DMA],
        compiler_params=_cparams(("arbitrary",)),
        name="combine",
    )(pos3, x1, rf, gain, ys)


def _pick(n, pref):
    while n % pref:
        pref //= 2
    return pref


def _layer(x2d, b, s, layer, attn_norm_gain, w_in, sb_norm_gain, lq1, lk1, lq2, lk2, subln_gain,
           w_out, ffn_norm_gain, w_gr, b_gr, w_er, b_er, w_gate, w_up, w_down):
    t, d = x2d.shape
    tb = _pick(s, 256)
    tm_proj = _pick(s, 1024)
    tm_out = _pick(t, 256)
    tx = 256

    half = HEAD_DIM // 2
    inv_freq = 1.0 / (ROPE_THETA ** (np.arange(half, dtype=np.float64) / half))
    ang = np.arange(s, dtype=np.float64)[:, None] * inv_freq[None, :]
    cos = jnp.asarray(np.concatenate([np.cos(ang), np.cos(ang)], axis=-1), F32)
    sin = jnp.asarray(np.concatenate([-np.sin(ang), np.sin(ang)], axis=-1), F32)

    proj = _proj_call(x2d, attn_norm_gain.reshape(1, d), w_in, cos, sin, s,
                      tm=tm_proj, tn=512)
    proj3d = proj.reshape(b, s, proj.shape[1])

    r = jnp.arange(tb)
    tri_suffix = (r[:, None] >= r[None, :]).astype(BF16)
    sb_out = _sb_call(proj3d, tri_suffix, sb_norm_gain.reshape(1, HEAD_DIM), tb=tb,
                      nsub=_pick(s // tb, 16))

    lam_init = 0.8 - 0.6 * math.exp(-0.3 * layer)
    d_out = _diff_call(proj3d, lq1.reshape(1, -1), lk1.reshape(1, -1), lq2.reshape(1, -1),
                       lk2.reshape(1, -1), subln_gain.reshape(1, -1), lam_init, tq=2 * tb)

    w_router = jnp.zeros((d, LANES), F32)
    w_router = w_router.at[:, :N_GROUPS].set(w_gr).at[:, N_GROUPS:N_GROUPS + N_EXPERTS].set(w_er)
    b_router = jnp.zeros((1, LANES), F32)
    b_router = b_router.at[0, :N_GROUPS].set(b_gr).at[0, N_GROUPS:N_GROUPS + N_EXPERTS].set(b_er)
    rr = jnp.arange(tm_out)
    tri_before = (rr[None, :] < rr[:, None]).astype(BF16)
    x1, h2, ri, rf, cnt = _outproj_call(
        x2d, sb_out.reshape(t, -1), d_out.reshape(t, -1), w_out,
        ffn_norm_gain.reshape(1, d), w_router.astype(BF16), b_router, tri_before, tm=tm_out)

    counts = cnt[0, :N_EXPERTS].astype(jnp.int32)
    seg_hi = jnp.cumsum(counts)
    seg_lo = seg_hi - counts
    experts = jnp.arange(N_EXPERTS, dtype=jnp.int32)
    ri = ri.transpose(1, 0, 2).reshape(8, t)

    def sorted_position(slot):
        hit = ri[slot][None, :] == experts[:, None]
        return jnp.sum(jnp.where(hit, seg_lo[:, None], 0), axis=0) + ri[2 + slot]

    def tiled_positions(tile):
        return jnp.stack([sorted_position(0).reshape(-1, tile), sorted_position(1).reshape(-1, tile)],
                         axis=1).reshape(-1, 1, 2 * tile)

    first_tile = seg_lo // tx
    items = jnp.where(counts > 0, (seg_hi - 1) // tx - first_tile + 1, 0)
    item_hi = jnp.cumsum(items)
    n_work = item_hi[-1]
    w = jnp.minimum(jnp.arange(2 * t // tx + N_EXPERTS - 1, dtype=jnp.int32), n_work - 1)
    work_expert = jnp.sum((item_hi[None, :] <= w[:, None]).astype(jnp.int32), axis=1)
    work_tile = (first_tile[work_expert] + w - (item_hi - items)[work_expert]).astype(jnp.int32)

    xs = _dispatch_call(tiled_positions(_pick(t, 2048)), h2, 2 * t)
    used = counts > 0
    work_slot = ((jnp.cumsum(used.astype(jnp.int32)) - 1) % 2)[work_expert].astype(jnp.int32)
    later_used = (experts[None, :] > experts[:, None]) & used[None, :]
    next_used = jnp.min(jnp.where(later_used, experts[None, :], N_EXPERTS), axis=1)
    next_used = jnp.where(next_used < N_EXPERTS, next_used, -1)
    run_start = jnp.concatenate([jnp.ones((1,), bool), work_expert[1:] != work_expert[:-1]])
    work_next = jnp.where(run_start, next_used[work_expert], -1).astype(jnp.int32)
    ys = _experts_call(work_tile, work_expert, n_work.reshape(1).astype(jnp.int32), seg_lo, seg_hi,
                       work_slot, work_next, xs, w_gate, w_up, w_down, tx=tx)
    return x1, tiled_positions(_pick(t, 512)), rf, ys


def kernel(x, attn_norm_gain, w_in, sb_norm_gain, diff_lambda_q1, diff_lambda_k1, diff_lambda_q2,
           diff_lambda_k2, diff_subln_gain, w_out, ffn_norm_gain, w_group_router, b_group_router,
           w_expert_router, b_expert_router, w_gate, w_up, w_down, final_norm_gain):
    b, s, d = x.shape
    assert w_in.shape[0] == 1, "the combine stage fuses the final norm: single-layer stacks only"
    layer = 0
    x2d = x.reshape(b * s, d)
    x1, pos, rf, ys = _layer(
        x2d, b, s, layer, attn_norm_gain[layer], w_in[layer], sb_norm_gain[layer],
        diff_lambda_q1[layer], diff_lambda_k1[layer], diff_lambda_q2[layer],
        diff_lambda_k2[layer], diff_subln_gain[layer], w_out[layer], ffn_norm_gain[layer],
        w_group_router[layer], b_group_router[layer], w_expert_router[layer],
        b_expert_router[layer], w_gate[layer], w_up[layer], w_down[layer])
    out = _combine_call(pos, x1, rf, final_norm_gain.reshape(1, d), ys)
    return out.reshape(b, s, d)
```
